```python
import jax
import jax.numpy as jnp
from jax import lax
import numpy as np

D_MODEL = 2048
BATCH = 4
SEQ = 2048
DEPTH = 1
DEC_BATCH = 128
DEC_SEQ = 4
PAST_LEN = 16384
PAGE_SIZE = 128

D_MIX = D_MODEL
D_A = D_MIX // 2
D_B = D_MIX - D_A
HEAD_A = 64
H_A = D_A // HEAD_A
R_W = D_A // 16
R_A = D_A // 16
R_G = D_A // 8
C_A = 3 * D_A + R_W + R_A + R_G
C_TOT = C_A + 2 * D_B
RWKV_SPLITS = (D_A, 2 * D_A, 3 * D_A, 3 * D_A + R_W, 3 * D_A + R_W + R_A)
N_LRU_BLOCKS = 16
LRU_BW = D_B // N_LRU_BLOCKS
CONV_W = 4
C_LRU = 8.0
P_HEADS = 8
N_KEYS = 128
N_EXPERTS = N_KEYS * N_KEYS
TOPK = 16
D_Q = 256
D_QH = D_Q // 2
PEER_BLOCK = 128
LN_EPS = 1e-5
GN_EPS = 64e-5
ALPHA = (2 * DEPTH) ** 0.25
BETA = (8 * DEPTH) ** -0.25

kernel_name = 'hymba_rwkv7_rglru_peer_decoder_step'


def _layernorm(x, w, b):
    xf = x.astype(jnp.float32)
    mean = jnp.mean(xf, axis=-1, keepdims=True)
    var = jnp.mean(jnp.square(xf - mean), axis=-1, keepdims=True)
    return ((xf - mean) * lax.rsqrt(var + LN_EPS)).astype(x.dtype) * w + b


def _wkv_step(S, inp):
    r, w, k, v, kk, kka = inp
    sa = jnp.einsum('bhij,bhj->bhi', S, -kk)
    S = S * w[:, :, None, :] + sa[..., None] * kka[:, :, None, :] + v[..., None] * k[:, :, None, :]
    y = jnp.einsum('bhij,bhj->bhi', S, r)
    return S, y


def _lru_step(h, inp):
    a_t, b_t = inp
    h = a_t * h + b_t
    return h, h


def _rwkv_time_mix(p, p_prev, S0, lp):
    B, T, _ = p.shape
    f32 = jnp.float32
    shifted = jnp.concatenate([p_prev[:, None, :], p[:, :-1]], axis=1)
    p = p + (shifted - p) * lp['mu_shift']
    r, k, v, xw, xa, xg = jnp.split(p, RWKV_SPLITS, axis=-1)
    w_log = -jax.nn.softplus(-(lp['w0'] + jnp.tanh(xw) @ lp['w_w2']).astype(f32)) - 0.5
    decay = jnp.exp(-jnp.exp(w_log))
    a = jax.nn.sigmoid(lp['a0'] + xa @ lp['w_a2'])
    g = jax.nn.sigmoid(xg) @ lp['w_g2']

    def heads(t):
        return t.reshape(B, T, H_A, HEAD_A)

    kk = heads(k * lp['k_k']).astype(f32)
    kk = kk / jnp.maximum(jnp.sqrt(jnp.sum(kk * kk, axis=-1, keepdims=True)), 1e-12)
    k = k * (1 + (a - 1) * lp['k_a'])
    r_h, k_h, v_h, a_h = heads(r), heads(k), heads(v), heads(a)
    seq = tuple(jnp.moveaxis(t.astype(f32), 1, 0)
                for t in (r_h, heads(decay), k_h, v_h, kk, kk * a_h.astype(f32)))
    S, y = lax.scan(_wkv_step, S0.astype(f32), seq)
    y = jnp.moveaxis(y, 0, 1)
    mean = jnp.mean(y, axis=-1, keepdims=True)
    var = jnp.mean(jnp.square(y - mean), axis=-1, keepdims=True)
    yn = ((y - mean) * lax.rsqrt(var + GN_EPS)).reshape(B, T, D_A).astype(p.dtype)
    yn = yn * lp['lnx_w'] + lp['lnx_b']
    bonus = (jnp.sum(r_h * k_h * lp['r_k'], axis=-1, keepdims=True) * v_h).reshape(B, T, D_A)
    return (yn + bonus) * g, S.astype(S0.dtype)


def _rglru_mix(xb, gb, conv_buf, h0, lp):
    B, T, _ = xb.shape
    f32 = jnp.float32
    xpad = jnp.concatenate([conv_buf.astype(xb.dtype), xb], axis=1)
    xc = lp['conv_b']
    for j in range(CONV_W):
        xc = xc + xpad[:, j:j + T] * lp['conv_w'][j]
    xh = xc.reshape(B, T, N_LRU_BLOCKS, LRU_BW)
    r = jax.nn.sigmoid(jnp.einsum('btni,nij->btnj', xh, lp['w_gate_a']).reshape(B, T, D_B) + lp['b_gate_a'])
    i = jax.nn.sigmoid(jnp.einsum('btni,nij->btnj', xh, lp['w_gate_i']).reshape(B, T, D_B) + lp['b_gate_i'])
    log_a = (-C_LRU * r * jax.nn.softplus(-lp['lru_lambda'])).astype(f32)
    a = jnp.exp(log_a)
    gated = jnp.sqrt(-jnp.expm1(2.0 * log_a)) * (i * xc).astype(f32)
    hN, hs = lax.scan(_lru_step, h0.astype(f32), (jnp.moveaxis(a, 1, 0), jnp.moveaxis(gated, 1, 0)))
    y = jnp.moveaxis(hs, 0, 1).astype(xb.dtype) * jax.nn.gelu(gb)
    return y, hN.astype(h0.dtype), xpad[:, -(CONV_W - 1):]


def _peer(h, w_q, sub_keys, peer_u, peer_v):
    B, T, D = h.shape
    n = B * T
    nb = -(-n // PEER_BLOCK)
    xt = jnp.pad(h.reshape(n, D), ((0, nb * PEER_BLOCK - n), (0, 0))).reshape(nb, PEER_BLOCK, D)

    def block(xb):
        q = (xb @ w_q).reshape(PEER_BLOCK, P_HEADS, 2, D_QH)
        s1 = jnp.einsum('thd,kd->thk', q[:, :, 0], sub_keys[0]).astype(jnp.float32)
        s2 = jnp.einsum('thd,kd->thk', q[:, :, 1], sub_keys[1]).astype(jnp.float32)
        v1, i1 = lax.top_k(s1, TOPK)
        v2, i2 = lax.top_k(s2, TOPK)
        cand = (v1[..., :, None] + v2[..., None, :]).reshape(PEER_BLOCK, P_HEADS, TOPK * TOPK)
        sv, si = lax.top_k(cand, TOPK)
        e = (jnp.take_along_axis(i1, si // TOPK, axis=-1) * N_KEYS
             + jnp.take_along_axis(i2, si % TOPK, axis=-1))
        gate = jax.nn.softmax(sv, axis=-1).astype(xb.dtype)
        act = jax.nn.gelu(jnp.einsum('thkd,td->thk', peer_u[e], xb))
        return jnp.einsum('thk,thkd->td', gate * act, peer_v[e])

    out = lax.map(block, xt).reshape(nb * PEER_BLOCK, D)[:n]
    return out.reshape(B, T, D)


def _layer(x, c, shift_prev, wkv0, lru0, conv0, lp):
    mod = jax.nn.silu(c) @ lp['w_ada'] + lp['b_ada']
    sh1, sc1, g1, sh2, sc2, g2 = [m[:, None, :] for m in jnp.split(mod, 6, axis=-1)]
    h = x * (1 + sc1) + sh1
    proj = h @ lp['w_in']
    p_prev = shift_prev @ lp['w_in'][:, :C_A]
    ya, wkv = _rwkv_time_mix(proj[..., :C_A], p_prev, wkv0, lp)
    yb, lru, conv = _rglru_mix(proj[..., C_A:C_A + D_B], proj[..., C_A + D_B:], conv0, lru0, lp)
    mix = jnp.concatenate([ya, yb], axis=-1) @ lp['w_out']
    x = _layernorm(ALPHA * x + (1 + g1) * mix, lp['ln1_w'], lp['ln1_b'])
    h2 = x * (1 + sc2) + sh2
    ff = _peer(h2, lp['w_q'], lp['sub_keys'], lp['peer_u'], lp['peer_v'])
    x = _layernorm(ALPHA * x + (1 + g2) * ff, lp['ln2_w'], lp['ln2_b'])
    return x, h[:, -1], wkv, lru, conv


def setup_inputs(seed: int = 0) -> dict:
    key = jax.random.key(seed)
    kit = iter(list(jax.random.split(key, 48)))
    f32 = jnp.float32
    L = DEPTH

    def nrm(shape, scale):
        return jax.random.normal(next(kit), shape, f32) * scale

    def unif(shape, lo, hi):
        return jax.random.uniform(next(kit), shape, f32, lo, hi)

    sig = unif((L, D_B), 0.9, 0.999) ** (1.0 / C_LRU)
    return {
        'x_prompt': nrm((BATCH, SEQ, D_MODEL), 1.0),
        'x_sample': nrm((DEC_BATCH, DEC_SEQ, D_MODEL), 1.0),
        'c_prompt': nrm((BATCH, D_MODEL), 1.0),
        'c_sample': nrm((DEC_BATCH, D_MODEL), 1.0),
        'state_shift': nrm((L, DEC_BATCH, D_MODEL), 1.0),
        'state_wkv': nrm((L, DEC_BATCH, H_A, HEAD_A, HEAD_A), 0.3),
        'state_lru': nrm((L, DEC_BATCH, D_B), 0.5),
        'state_conv': nrm((L, DEC_BATCH, CONV_W - 1, D_B), 1.0),
        'w_ada': nrm((L, D_MODEL, 6 * D_MODEL), 0.3 * D_MODEL ** -0.5),
        'b_ada': nrm((L, 6 * D_MODEL), 0.01),
        'w_in': nrm((L, D_MODEL, C_TOT), D_MODEL ** -0.5),
        'mu_shift': unif((L, C_A), 0.0, 1.0),
        'w0': unif((L, D_A), -5.0, 1.0),
        'w_w2': nrm((L, R_W, D_A), 0.5 * R_W ** -0.5),
        'a0': nrm((L, D_A), 0.1),
        'w_a2': nrm((L, R_A, D_A), 0.5 * R_A ** -0.5),
        'w_g2': nrm((L, R_G, D_A), R_G ** -0.5),
        'k_k': 0.85 + nrm((L, D_A), 0.1),
        'k_a': 1.0 + nrm((L, D_A), 0.1),
        'r_k': nrm((L, H_A, HEAD_A), 0.1),
        'lnx_w': 1.0 + nrm((L, D_A), 0.1),
        'lnx_b': nrm((L, D_A), 0.01),
        'conv_w': nrm((L, CONV_W, D_B), CONV_W ** -0.5),
        'conv_b': nrm((L, D_B), 0.01),
        'w_gate_a': nrm((L, N_LRU_BLOCKS, LRU_BW, LRU_BW), LRU_BW ** -0.5),
        'b_gate_a': nrm((L, D_B), 0.01),
        'w_gate_i': nrm((L, N_LRU_BLOCKS, LRU_BW, LRU_BW), LRU_BW ** -0.5),
        'b_gate_i': nrm((L, D_B), 0.01),
        'lru_lambda': jnp.log(sig) - jnp.log1p(-sig),
        'w_out': nrm((L, D_MIX, D_MODEL), BETA * D_MIX ** -0.5),
        'ln1_w': 1.0 + nrm((L, D_MODEL), 0.1),
        'ln1_b': nrm((L, D_MODEL), 0.01),
        'w_q': nrm((L, D_MODEL, P_HEADS * D_Q), D_MODEL ** -0.5),
        'sub_keys': nrm((L, 2, N_KEYS, D_QH), D_QH ** -0.5),
        'peer_u': nrm((L, N_EXPERTS, D_MODEL), D_MODEL ** -0.5),
        'peer_v': nrm((L, N_EXPERTS, D_MODEL), BETA * P_HEADS ** -0.5),
        'ln2_w': 1.0 + nrm((L, D_MODEL), 0.1),
        'ln2_b': nrm((L, D_MODEL), 0.01),
    }


def reference(x_prompt, x_sample, c_prompt, c_sample, state_shift, state_wkv, state_lru, state_conv,
              w_ada, b_ada, w_in, mu_shift, w0, w_w2, a0, w_a2, w_g2, k_k, k_a, r_k, lnx_w, lnx_b,
              conv_w, conv_b, w_gate_a, b_gate_a, w_gate_i, b_gate_i, lru_lambda, w_out,
              ln1_w, ln1_b, w_q, sub_keys, peer_u, peer_v, ln2_w, ln2_b):
    yp, ys = x_prompt, x_sample
    bp = x_prompt.shape[0]
    p_shift, p_wkv, p_lru, p_conv = [], [], [], []
    s_shift, s_wkv, s_lru, s_conv = [], [], [], []
    for l in range(DEPTH):
        lp = {
            'w_ada': w_ada[l], 'b_ada': b_ada[l], 'w_in': w_in[l], 'mu_shift': mu_shift[l],
            'w0': w0[l], 'w_w2': w_w2[l], 'a0': a0[l], 'w_a2': w_a2[l], 'w_g2': w_g2[l],
            'k_k': k_k[l], 'k_a': k_a[l], 'r_k': r_k[l], 'lnx_w': lnx_w[l], 'lnx_b': lnx_b[l],
            'conv_w': conv_w[l], 'conv_b': conv_b[l], 'w_gate_a': w_gate_a[l], 'b_gate_a': b_gate_a[l],
            'w_gate_i': w_gate_i[l], 'b_gate_i': b_gate_i[l], 'lru_lambda': lru_lambda[l],
            'w_out': w_out[l], 'ln1_w': ln1_w[l], 'ln1_b': ln1_b[l], 'w_q': w_q[l],
            'sub_keys': sub_keys[l], 'peer_u': peer_u[l], 'peer_v': peer_v[l],
            'ln2_w': ln2_w[l], 'ln2_b': ln2_b[l],
        }
        yp, ps, pw, pl, pc = _layer(
            yp, c_prompt,
            jnp.zeros((bp, D_MODEL), yp.dtype),
            jnp.zeros((bp, H_A, HEAD_A, HEAD_A), state_wkv.dtype),
            jnp.zeros((bp, D_B), state_lru.dtype),
            jnp.zeros((bp, CONV_W - 1, D_B), state_conv.dtype),
            lp)
        p_shift.append(ps)
        p_wkv.append(pw)
        p_lru.append(pl)
        p_conv.append(pc)
        ys, ss, sw, sl, sc = _layer(ys, c_sample, state_shift[l], state_wkv[l], state_lru[l], state_conv[l], lp)
        s_shift.append(ss)
        s_wkv.append(sw)
        s_lru.append(sl)
        s_conv.append(sc)
    return (yp, ys, jnp.stack(p_shift), jnp.stack(p_wkv), jnp.stack(p_lru), jnp.stack(p_conv),
            jnp.stack(s_shift), jnp.stack(s_wkv), jnp.stack(s_lru), jnp.stack(s_conv))
```

```python
import functools

import jax
import jax.numpy as jnp
from jax import lax
from jax.experimental import pallas as pl
from jax.experimental.pallas import tpu as pltpu

F32 = jnp.float32
BF16 = jnp.bfloat16

D_MODEL = 2048
D_A = 1024
D_B = 1024
HEAD = 64
N_HEADS = D_A // HEAD
N_PAIRS = N_HEADS // 2
R_W = 64
R_A = 64
R_G = 128
C_A = 3 * D_A + R_W + R_A + R_G
C_TOT = C_A + 2 * D_B
CONV_W = 4
C_LRU = 8.0
P_HEADS = 8
N_KEYS = 128
N_EXPERTS = N_KEYS * N_KEYS
TOPK = 16
D_QH = 128
LN_EPS = 1e-5
GN_EPS = 64e-5

LANES = 128
SUBLANES = 8
ROWS = 128
SAMPLE_TPAD = 8
VMEM_LIMIT = 56 * 1024 * 1024


def _cparams(sem):
    return pltpu.CompilerParams(dimension_semantics=sem, vmem_limit_bytes=VMEM_LIMIT)


def _split2(x):
    hi = x.astype(BF16)
    lo = (x - hi.astype(F32)).astype(BF16)
    return hi, lo


_NN = (((1,), (0,)), ((), ()))
_NT = (((1,), (1,)), ((), ()))
_TN = (((0,), (0,)), ((), ()))


def _dg(a, b, dims):
    return lax.dot_general(a, b, dims, preferred_element_type=F32)


def _dot1(a, b, dims=_NN):
    return _dg(a.astype(BF16), b.astype(BF16), dims)


def _dot3(a, b, dims=_NN):
    ah, al = _split2(a)
    bh, bl = _split2(b)
    return _dg(ah, bh, dims) + (_dg(ah, bl, dims) + _dg(al, bh, dims))


def _dot_exact_rhs(a, b_exact, dims=_NN):
    a1 = a.astype(BF16)
    r1 = a - a1.astype(F32)
    a2 = r1.astype(BF16)
    a3 = (r1 - a2.astype(F32)).astype(BF16)
    return _dg(a1, b_exact, dims) + (_dg(a2, b_exact, dims) + _dg(a3, b_exact, dims))


def _dot_exact_lhs(a_exact, b, dims=_NN):
    b1 = b.astype(BF16)
    r1 = b - b1.astype(F32)
    b2 = r1.astype(BF16)
    b3 = (r1 - b2.astype(F32)).astype(BF16)
    return _dg(a_exact, b1, dims) + (_dg(a_exact, b2, dims) + _dg(a_exact, b3, dims))


def _sigmoid(x):
    return 1.0 / (1.0 + jnp.exp(-x))


def _softplus(x):
    return jnp.maximum(x, 0.0) + jnp.log(1.0 + jnp.exp(-jnp.abs(x)))


def _gelu(x):
    return 0.5 * x * (1.0 + jnp.tanh(0.7978845608028654 * (x + 0.044715 * (x * x * x))))


def _iota(shape, dim):
    return lax.broadcasted_iota(jnp.int32, shape, dim)


def _ada_kernel(c_ref, w_ref, b_ref, o_ref):
    c = c_ref[...]
    s = c * _sigmoid(c)
    o_ref[...] = _dot3(s, w_ref[...]) + b_ref[...]


def _ada(c, w_ada, b_ada):
    n = c.shape[0]
    tn = 512
    return pl.pallas_call(
        _ada_kernel,
        out_shape=jax.ShapeDtypeStruct((n, 6 * D_MODEL), F32),
        grid=(6 * D_MODEL // tn,),
        in_specs=[pl.BlockSpec((n, D_MODEL), lambda j: (0, 0)),
                  pl.BlockSpec((D_MODEL, tn), lambda j: (0, j)),
                  pl.BlockSpec((1, tn), lambda j: (0, j))],
        out_specs=pl.BlockSpec((n, tn), lambda j: (0, j)),
        compiler_params=_cparams(("arbitrary",)),
        name="ada",
    )(c, w_ada, b_ada.reshape(1, -1))


def _inproj_kernel(x_ref, sc_ref, sh_ref, w_ref, o_ref, h_ref):
    h = x_ref[...] * (1.0 + sc_ref[...]) + sh_ref[...]
    ns, tb, _ = h.shape
    h_ref[...] = h[:, tb - SUBLANES:, :]
    p = _dot1(h.reshape(ns * tb, D_MODEL), w_ref[...])
    o_ref[...] = p.reshape(ns, tb, p.shape[-1])


def _inproj(x, mod, w_bf16, ns, tb):
    S, T, _ = x.shape
    ncol = w_bf16.shape[1]
    tn = ncol // 3
    return pl.pallas_call(
        _inproj_kernel,
        out_shape=(jax.ShapeDtypeStruct((S, T, ncol), F32),
                   jax.ShapeDtypeStruct((S, SUBLANES, D_MODEL), F32)),
        grid=(3, S // ns, T // tb),
        in_specs=[pl.BlockSpec((ns, tb, D_MODEL), lambda j, s, t: (s, t, 0)),
                  pl.BlockSpec((ns, 1, D_MODEL), lambda j, s, t: (s, 0, 1)),
                  pl.BlockSpec((ns, 1, D_MODEL), lambda j, s, t: (s, 0, 0)),
                  pl.BlockSpec((D_MODEL, tn), lambda j, s, t: (0, j))],
        out_specs=(pl.BlockSpec((ns, tb, tn), lambda j, s, t: (s, t, j)),
                   pl.BlockSpec((ns, SUBLANES, D_MODEL), lambda j, s, t: (s, 0, 0))),
        compiler_params=_cparams(("arbitrary", "arbitrary", "arbitrary")),
        name="inproj",
    )(x, mod, mod, w_bf16)


def _mixer_kernel(proj_ref, init_ref, s0_ref, lru0_ref,
                  mu_ref, w0_ref, ww2_ref, a0_ref, wa2_ref, wg2_ref, kk_ref, ka_ref, rk_ref,
                  lnw_ref, lnb_ref, seg_ref, segt_ref,
                  cw_ref, cb_ref, wga_ref, bga_ref, wgi_ref, bgi_ref, lam_ref,
                  y_ref, sout_ref, lruo_ref, convo_ref,
                  prev_ref, hst_ref,
                  bp_ref, rp_ref, x2_ref, y2_ref, v_ref, kb_ref, ab_ref, wt_ref, yacc_ref,
                  *, ns, tb, chunk, t_valid):
    R = ns * tb
    t = pl.program_id(1)

    @pl.when(t == 0)
    def _():
        prev_ref[...] = init_ref[...]
        sout_ref[...] = s0_ref[...]
        hst_ref[...] = lru0_ref[...]

    cur = proj_ref[...]
    ext = jnp.concatenate([prev_ref[...], cur[:, :, :C_A + D_B]], axis=1)
    prev_ref[...] = cur[:, tb - SUBLANES:, :C_A + D_B]

    row = _iota((R, 1), 0)
    rowin = row % tb
    valid = rowin < t_valid

    pa = cur[:, :, :C_A].reshape(R, C_A)
    shifted = ext[:, SUBLANES - 1:SUBLANES - 1 + tb, :C_A].reshape(R, C_A)
    p = pa + (shifted - pa) * mu_ref[...]
    r = p[:, :D_A]
    k = p[:, D_A:2 * D_A]
    v = p[:, 2 * D_A:3 * D_A]
    xwa = p[:, 3 * D_A:3 * D_A + LANES]
    xg = p[:, 3 * D_A + LANES:]
    w_log = -_softplus(-(w0_ref[...] + _dot3(jnp.tanh(xwa), ww2_ref[...]))) - 0.5
    logw = -jnp.exp(w_log)
    a = _sigmoid(a0_ref[...] + _dot3(xwa, wa2_ref[...]))
    g = _dot1(_sigmoid(xg), wg2_ref[...])
    seg = seg_ref[...]
    segt = segt_ref[...]

    def seg_sum(x):
        return _dot_exact_rhs(_dot_exact_rhs(x, seg), segt)

    kk = k * kk_ref[...]
    kk = kk / jnp.maximum(jnp.sqrt(seg_sum(kk * kk)), 1e-12)
    k = k * (1.0 + (a - 1.0) * ka_ref[...])
    kka = kk * a
    bonus = seg_sum(r * k * rk_ref[...]) * v
    logw = jnp.where(valid, logw, 0.0)
    r = jnp.where(valid, r, 0.0)
    k = jnp.where(valid, k, 0.0)
    v = jnp.where(valid, v, 0.0)
    kk = jnp.where(valid, kk, 0.0)
    kka = jnp.where(valid, kka, 0.0)

    ri = _iota((R, R), 0)
    ci = _iota((R, R), 1)
    same = (ri // chunk) == (ci // chunk)
    incl = same & (ci <= ri)
    strict = same & (ci < ri)
    lmat = jnp.concatenate([jnp.where(incl, 1.0, 0.0), jnp.where(same, 1.0, 0.0)], axis=0).astype(BF16)
    cums = _dot_exact_lhs(lmat, logw)
    cum = cums[:R]
    tot = cums[R:]
    e_prev = jnp.exp(cum - logw)
    e_neg = jnp.exp(-cum)
    e_pos = jnp.exp(cum)
    e_rem = jnp.exp(tot - cum)
    beta = kk * e_prev
    alpha = kka * e_neg
    kappa = k * e_neg
    rho = r * e_pos
    v_ref[...] = v
    kb_ref[...] = k * e_rem
    ab_ref[...] = -(kka * e_rem)
    wt_ref[...] = jnp.exp(tot)

    lane = _iota((1, LANES), 1)
    masks = (lane < HEAD, lane >= HEAD)
    eye = jnp.where(ri == ci, 1.0, 0.0)
    nsq = {16: 3, 8: 2}[chunk]
    for pr in range(N_PAIRS):
        sl = slice(pr * LANES, (pr + 1) * LANES)
        b_p, r_p, a_p, k_p, v_p = beta[:, sl], rho[:, sl], alpha[:, sl], kappa[:, sl], v[:, sl]
        rhs = jnp.concatenate([a_p, k_p], axis=0)
        bp_acc = rp_acc = x2_acc = y2_acc = None
        for m in masks:
            bm = jnp.where(m, b_p, 0.0)
            rm = jnp.where(m, r_p, 0.0)
            gm = _dot3(jnp.concatenate([bm, rm], axis=0), rhs, _NT)
            l_ba = jnp.where(strict, gm[:R, :R], 0.0)
            l_bk = jnp.where(strict, gm[:R, R:], 0.0)
            m_ra = jnp.where(incl, gm[R:, :R], 0.0)
            m_rk = jnp.where(incl, gm[R:, R:], 0.0)
            pw = _dot3(l_ba, l_ba)
            minv = eye - l_ba
            for i in range(nsq):
                minv = minv + _dot3(minv, pw)
                if i + 1 < nsq:
                    pw = _dot3(pw, pw)
            xy1 = _dot3(jnp.concatenate([l_bk, m_rk], axis=0), v_p)
            t1 = _dot3(minv, jnp.concatenate([xy1[:R], bm], axis=1))
            x2, bprime = t1[:, :LANES], t1[:, LANES:]
            t2 = _dot3(m_ra, t1)
            y2 = xy1[R:] - t2[:, :LANES]
            rprime = rm - t2[:, LANES:]
            if bp_acc is None:
                bp_acc, rp_acc = bprime, rprime
                x2_acc, y2_acc = x2, y2
            else:
                bp_acc, rp_acc = bp_acc + bprime, rp_acc + rprime
                x2_acc = jnp.where(m, x2, x2_acc)
                y2_acc = jnp.where(m, y2, y2_acc)
        bp_ref[:, sl] = bp_acc
        rp_ref[:, sl] = rp_acc
        x2_ref[:, sl] = x2_acc
        y2_ref[:, sl] = y2_acc

    bi = _iota((LANES, LANES), 0)
    bj = _iota((LANES, LANES), 1)
    bd = (bi < HEAD) == (bj < HEAD)
    for s in range(ns):
        for c in range(tb // chunk):
            rs = slice(s * tb + c * chunk, s * tb + (c + 1) * chunk)
            for pr in range(N_PAIRS):
                sl = slice(pr * LANES, (pr + 1) * LANES)
                sp = sout_ref[s, pr]
                lhs = jnp.concatenate([bp_ref[rs, sl], rp_ref[rs, sl]], axis=0)
                pm = _dot3(lhs, sp, _NT)
                u = pm[:chunk] + x2_ref[rs, sl]
                yacc_ref[rs, sl] = pm[chunk:] + y2_ref[rs, sl]
                vu = jnp.concatenate([v_ref[rs, sl], u], axis=0)
                ka = jnp.concatenate([kb_ref[rs, sl], ab_ref[rs, sl]], axis=0)
                upd = _dot3(vu, ka, _TN)
                wrow = wt_ref[rs.start:rs.start + 1, sl]
                sout_ref[s, pr] = jnp.where(bd, sp * wrow + upd, 0.0)

    y = yacc_ref[...]
    mean = seg_sum(y) * (1.0 / HEAD)
    yc = y - mean
    var = seg_sum(yc * yc) * (1.0 / HEAD)
    yn = yc * lax.rsqrt(var + GN_EPS) * lnw_ref[...] + lnb_ref[...]
    ya = (yn + bonus) * g
    y_ref[:, :, :D_A] = ya.reshape(ns, tb, D_A)

    gb = cur[:, :, C_A + D_B:].reshape(R, D_B)
    xc = cb_ref[...]
    for j in range(CONV_W):
        off = SUBLANES - (CONV_W - 1) + j
        xc = xc + ext[:, off:off + tb, C_A:].reshape(R, D_B) * cw_ref[j:j + 1, :]
    gr = []
    gi = []
    for pr in range(N_PAIRS):
        sl = slice(pr * LANES, (pr + 1) * LANES)
        gr.append(_dot1(xc[:, sl], wga_ref[pr]))
        gi.append(_dot1(xc[:, sl], wgi_ref[pr]))
    rg = _sigmoid(jnp.concatenate(gr, axis=1) + bga_ref[...])
    ig = _sigmoid(jnp.concatenate(gi, axis=1) + bgi_ref[...])
    log_a = -C_LRU * rg * _softplus(-lam_ref[...])
    av = jnp.exp(log_a)
    bv = jnp.sqrt(1.0 - jnp.exp(2.0 * log_a)) * (ig * xc)
    av = jnp.where(valid, av, 1.0)
    bv = jnp.where(valid, bv, 0.0)
    d = 1
    while d < tb:
        take = rowin >= d
        a_sh = jnp.where(take, pltpu.roll(av, d, axis=0), 1.0)
        b_sh = jnp.where(take, pltpu.roll(bv, d, axis=0), 0.0)
        bv = av * b_sh + bv
        av = av * a_sh
        d *= 2
    h0 = jnp.broadcast_to(hst_ref[:, SUBLANES - 1:SUBLANES, :], (ns, tb, D_B)).reshape(R, D_B)
    h = av * h0 + bv
    h3 = h.reshape(ns, tb, D_B)
    hst_ref[...] = h3[:, tb - SUBLANES:, :]
    y_ref[:, :, D_A:] = (h * _gelu(gb)).reshape(ns, tb, D_B)

    @pl.when(t == pl.num_programs(1) - 1)
    def _():
        lruo_ref[...] = h3[:, tb - SUBLANES:, :]
        convo_ref[...] = cur[:, tb - SUBLANES:, C_A:C_A + D_B]


def _mixer(proj, init8, s0, lru0, prm, ns, tb, chunk, t_valid):
    S, T, _ = proj.shape
    R = ns * tb
    assert R == ROWS
    kern = functools.partial(_mixer_kernel, ns=ns, tb=tb, chunk=chunk, t_valid=t_valid)

    def full(a):
        nd = a.ndim
        return pl.BlockSpec(a.shape, lambda s, t, nd=nd: (0,) * nd)

    params = [prm[n] for n in ('mu', 'w0', 'ww2', 'a0', 'wa2', 'wg2', 'k_k', 'k_a', 'r_k', 'lnx_w', 'lnx_b',
                               'seg', 'segt', 'conv_w', 'conv_b', 'wga', 'bga', 'wgi', 'bgi', 'lam')]
    slab = lambda w: pltpu.VMEM((R, w), F32)
    return pl.pallas_call(
        kern,
        out_shape=(jax.ShapeDtypeStruct((S, T, D_A + D_B), F32),
                   jax.ShapeDtypeStruct((S, N_PAIRS, LANES, LANES), F32),
                   jax.ShapeDtypeStruct((S, SUBLANES, D_B), F32),
                   jax.ShapeDtypeStruct((S, SUBLANES, D_B), F32)),
        grid=(S // ns, T // tb),
        in_specs=[pl.BlockSpec((ns, tb, C_TOT), lambda s, t: (s, t, 0)),
                  pl.BlockSpec((ns, SUBLANES, C_A + D_B), lambda s, t: (s, 0, 0)),
                  pl.BlockSpec((ns, N_PAIRS, LANES, LANES), lambda s, t: (s, 0, 0, 0),
                               pipeline_mode=pl.Buffered(1)),
                  pl.BlockSpec((ns, SUBLANES, D_B), lambda s, t: (s, 0, 0))]
                 + [full(a) for a in params],
        out_specs=(pl.BlockSpec((ns, tb, D_A + D_B), lambda s, t: (s, t, 0)),
                   pl.BlockSpec((ns, N_PAIRS, LANES, LANES), lambda s, t: (s, 0, 0, 0),
                                pipeline_mode=pl.Buffered(1)),
                   pl.BlockSpec((ns, SUBLANES, D_B), lambda s, t: (s, 0, 0)),
                   pl.BlockSpec((ns, SUBLANES, D_B), lambda s, t: (s, 0, 0))),
        scratch_shapes=[pltpu.VMEM((ns, SUBLANES, C_A + D_B), F32),
                        pltpu.VMEM((ns, SUBLANES, D_B), F32)]
                       + [slab(D_A) for _ in range(9)],
        compiler_params=_cparams(("arbitrary", "arbitrary")),
        name="mixer",
    )(proj, init8, s0, lru0, *params)


def _layernorm(x, w, b):
    mean = jnp.mean(x, axis=-1, keepdims=True)
    xc = x - mean
    var = jnp.mean(xc * xc, axis=-1, keepdims=True)
    return xc * lax.rsqrt(var + LN_EPS) * w + b


def _outproj_kernel(y_ref, x_ref, g1_ref, sc2_ref, sh2_ref, w_ref, lnw_ref, lnb_ref, x1_ref, h2_ref, *, alpha):
    ns, tb, _ = y_ref.shape
    mix = _dot1(y_ref[...].reshape(ns * tb, D_MODEL), w_ref[...]).reshape(ns, tb, D_MODEL)
    x1 = _layernorm(alpha * x_ref[...] + (1.0 + g1_ref[...]) * mix, lnw_ref[...], lnb_ref[...])
    x1_ref[...] = x1
    h2_ref[...] = x1 * (1.0 + sc2_ref[...]) + sh2_ref[...]


def _outproj(y, x, mod, w_bf16, ln_w, ln_b, ns, tb, alpha):
    S, T, _ = x.shape
    blk = pl.BlockSpec((ns, tb, D_MODEL), lambda s, t: (s, t, 0))
    modspec = lambda i: pl.BlockSpec((ns, 1, D_MODEL), lambda s, t, i=i: (s, 0, i))
    vec = pl.BlockSpec((1, D_MODEL), lambda s, t: (0, 0))
    return pl.pallas_call(
        functools.partial(_outproj_kernel, alpha=alpha),
        out_shape=(jax.ShapeDtypeStruct((S, T, D_MODEL), F32), jax.ShapeDtypeStruct((S, T, D_MODEL), F32)),
        grid=(S // ns, T // tb),
        in_specs=[blk, blk, modspec(2), modspec(4), modspec(3),
                  pl.BlockSpec((D_MODEL, D_MODEL), lambda s, t: (0, 0)), vec, vec],
        out_specs=(blk, blk),
        compiler_params=_cparams(("arbitrary", "arbitrary")),
        name="outproj",
    )(y, x, mod, mod, mod, w_bf16, ln_w.reshape(1, -1), ln_b.reshape(1, -1))


def _topk_rows(s, n_rows):
    ridx = _iota(s.shape, 0).astype(F32)
    out_i = _iota((TOPK, s.shape[1]), 0)
    vals = jnp.zeros((TOPK, s.shape[1]), F32)
    idxs = jnp.zeros((TOPK, s.shape[1]), F32)
    for it in range(TOPK):
        m = jnp.max(s, axis=0, keepdims=True)
        idx = jnp.min(jnp.where(s == m, ridx, float(n_rows)), axis=0, keepdims=True)
        vals = jnp.where(out_i == it, m, vals)
        idxs = jnp.where(out_i == it, idx, idxs)
        s = jnp.where(ridx == idx, -jnp.inf, s)
    return vals, idxs


def _route_kernel(h_ref, wq_ref, sk_ref, g_ref, tile_ref, gate_ref, e1_ref, e2_ref):
    R = ROWS
    q = _dot1(h_ref[...], wq_ref[...])
    sk0 = sk_ref[0]
    sk1 = sk_ref[1]
    gates, e1s, e2s = [], [], []
    for hd in range(P_HEADS):
        base = hd * 2 * D_QH
        s1 = _dot1(sk0, q[:, base:base + D_QH], _NT)
        s2 = _dot1(sk1, q[:, base + D_QH:base + 2 * D_QH], _NT)
        v1, i1 = _topk_rows(s1, N_KEYS)
        v2, i2 = _topk_rows(s2, N_KEYS)
        cand = jnp.concatenate([v1[a:a + 1, :] + v2 for a in range(TOPK)], axis=0)
        sv, ci = _topk_rows(cand, TOPK * TOPK)
        ca = jnp.floor(ci * (1.0 / TOPK))
        cb = ci - TOPK * ca
        e1 = jnp.zeros((TOPK, R), F32)
        e2 = jnp.zeros((TOPK, R), F32)
        for a in range(TOPK):
            e1 = jnp.where(ca == float(a), i1[a:a + 1, :], e1)
            e2 = jnp.where(cb == float(a), i2[a:a + 1, :], e2)
        ex = jnp.exp(sv - jnp.max(sv, axis=0, keepdims=True))
        gates.append(ex / jnp.sum(ex, axis=0, keepdims=True))
        e1s.append(e1)
        e2s.append(e2)
    gate_ref[...] = jnp.concatenate(gates, axis=0).T
    e1_ref[...] = jnp.concatenate(e1s, axis=0).T
    e2_ref[...] = jnp.concatenate(e2s, axis=0).T
    key_i = _iota((N_KEYS, P_HEADS * TOPK), 0).astype(F32)

    def per_token(tk, carry):
        sel1 = key_i == e1_ref[pl.ds(tk, 1), :]
        sel2 = key_i == e2_ref[pl.ds(tk, 1), :]
        m1 = jnp.where(sel1, gate_ref[pl.ds(tk, 1), :], 0.0)
        m2 = jnp.where(sel2, 1.0, 0.0)
        tile_ref[tk] = _dot1(m1, m2, _NT)
        return carry

    lax.fori_loop(0, R, per_token, 0)
    for i1v in range(N_KEYS):
        g_ref[:, i1v * N_KEYS:(i1v + 1) * N_KEYS] = tile_ref[:, i1v, :].astype(BF16)


def _route(h2, wq_bf16, sub_keys):
    n = h2.shape[0]
    return pl.pallas_call(
        _route_kernel,
        out_shape=jax.ShapeDtypeStruct((n, N_EXPERTS), BF16),
        grid=(n // ROWS,),
        in_specs=[pl.BlockSpec((ROWS, D_MODEL), lambda i: (i, 0)),
                  pl.BlockSpec((D_MODEL, D_MODEL), lambda i: (0, 0)),
                  pl.BlockSpec((2, N_KEYS, D_QH), lambda i: (0, 0, 0))],
        out_specs=pl.BlockSpec((ROWS, N_EXPERTS), lambda i: (i, 0)),
        scratch_shapes=[pltpu.VMEM((ROWS, N_KEYS, N_KEYS), F32)]
                       + [pltpu.VMEM((ROWS, P_HEADS * TOPK), F32) for _ in range(3)],
        compiler_params=_cparams(("arbitrary",)),
        name="route",
    )(h2, wq_bf16, sub_keys)


def _experts_kernel(h_ref, g_ref, u_ref, v_ref, x1_ref, g2_ref, lnw_ref, lnb_ref, o_ref, acc_ref, *, alpha):
    e = pl.program_id(2)

    @pl.when(e == 0)
    def _():
        acc_ref[...] = jnp.zeros_like(acc_ref)

    ns, tb, _ = h_ref.shape
    h = h_ref[...].reshape(ns * tb, D_MODEL)
    act = _dot1(h, u_ref[...], _NT)
    w = g_ref[...].astype(F32) * _gelu(act)
    acc_ref[...] += _dot1(w, v_ref[...])

    @pl.when(e == pl.num_programs(2) - 1)
    def _():
        ff = acc_ref[...].reshape(ns, tb, D_MODEL)
        o_ref[...] = _layernorm(alpha * x1_ref[...] + (1.0 + g2_ref[...]) * ff, lnw_ref[...], lnb_ref[...])


def _experts(h2, gmat, u_bf16, v_bf16, x1, mod, ln_w, ln_b, ns, tb, alpha):
    S, T, _ = x1.shape
    ec = 1024
    blk = pl.BlockSpec((ns, tb, D_MODEL), lambda s, t, e: (s, t, 0))
    vec = pl.BlockSpec((1, D_MODEL), lambda s, t, e: (0, 0))
    rows = ns * tb
    nt = T // tb
    return pl.pallas_call(
        functools.partial(_experts_kernel, alpha=alpha),
        out_shape=jax.ShapeDtypeStruct((S, T, D_MODEL), F32),
        grid=(S // ns, nt, N_EXPERTS // ec),
        in_specs=[blk,
                  pl.BlockSpec((rows, ec), lambda s, t, e: (s * nt + t, e)),
                  pl.BlockSpec((ec, D_MODEL), lambda s, t, e: (e, 0)),
                  pl.BlockSpec((ec, D_MODEL), lambda s, t, e: (e, 0)),
                  blk,
                  pl.BlockSpec((ns, 1, D_MODEL), lambda s, t, e: (s, 0, 5)),
                  vec, vec],
        out_specs=blk,
        scratch_shapes=[pltpu.VMEM((rows, D_MODEL), F32)],
        compiler_params=_cparams(("arbitrary", "arbitrary", "arbitrary")),
        name="experts",
    )(h2, gmat, u_bf16, v_bf16, x1, mod, ln_w.reshape(1, -1), ln_b.reshape(1, -1))


def _pair_blockdiag(w):
    n = w.shape[0] // 2
    w = w.reshape(n, 2, HEAD, HEAD)
    z = jnp.zeros((n, HEAD, HEAD), w.dtype)
    top = jnp.concatenate([w[:, 0], z], axis=2)
    bot = jnp.concatenate([z, w[:, 1]], axis=2)
    return jnp.concatenate([top, bot], axis=1)


def _pair_diag_extract(sp):
    S = sp.shape[0]
    a = sp[:, :, :HEAD, :HEAD]
    b = sp[:, :, HEAD:, HEAD:]
    return jnp.stack([a, b], axis=2).reshape(S, N_HEADS, HEAD, HEAD)


def _group(x, mod, p_prev, wkv0, lru0, conv0, prm, *, ns_mix, tb_mix, chunk, t_valid, ns_big, tb_big, alpha):
    S, T, _ = x.shape
    proj, h_last = _inproj(x, mod, prm['w_in'], ns_big, tb_big)
    init8 = jnp.zeros((S, SUBLANES, C_A + D_B), F32)
    init8 = init8.at[:, SUBLANES - 1, :C_A].set(p_prev)
    init8 = init8.at[:, SUBLANES - (CONV_W - 1):, C_A:].set(conv0)
    s0 = _pair_blockdiag(wkv0.reshape(S * N_HEADS, HEAD, HEAD)).reshape(S, N_PAIRS, LANES, LANES)
    lru8 = jnp.broadcast_to(lru0[:, None, :], (S, SUBLANES, D_B))
    y, sp, lru_o, conv_o = _mixer(proj, init8, s0, lru8, prm, ns_mix, tb_mix, chunk, t_valid)
    x1, h2 = _outproj(y, x, mod, prm['w_out'], prm['ln1_w'], prm['ln1_b'], ns_big, tb_big, alpha)
    gmat = _route(h2.reshape(S * T, D_MODEL), prm['w_q'], prm['sub_keys'])
    out = _experts(h2, gmat, prm['peer_u'], prm['peer_v'], x1, mod, prm['ln2_w'], prm['ln2_b'],
                   ns_big, tb_big, alpha)
    tv = (t_valid - 1) % SUBLANES
    shift = h_last[:, tv]
    wkv = _pair_diag_extract(sp)
    lru = lru_o[:, SUBLANES - 1]
    conv = conv_o[:, tv - (CONV_W - 2):tv + 1]
    return out, shift, wkv, lru, conv


def kernel(x_prompt, x_sample, c_prompt, c_sample, state_shift, state_wkv, state_lru, state_conv, w_ada, b_ada, w_in, mu_shift, w0, w_w2, a0, w_a2, w_g2, k_k, k_a, r_k, lnx_w, lnx_b, conv_w, conv_b, w_gate_a, b_gate_a, w_gate_i, b_gate_i, lru_lambda, w_out, ln1_w, ln1_b, w_q, sub_keys, peer_u, peer_v, ln2_w, ln2_b):
    depth = w_ada.shape[0]
    alpha = (2 * depth) ** 0.25
    bp, tp, _ = x_prompt.shape
    bs, ts, _ = x_sample.shape
    yp = x_prompt
    ys = jnp.pad(x_sample, ((0, 0), (0, SAMPLE_TPAD - ts), (0, 0)))
    head_of = jnp.arange(D_A) // HEAD
    seg = (head_of[:, None] == jnp.arange(LANES)[None, :]).astype(BF16)
    row = lambda a: a.reshape(1, -1)
    outs = [[] for _ in range(8)]
    for l in range(depth):
        zpad = jnp.zeros((LANES - R_W, D_A), F32)
        prm = {
            'w_in': w_in[l].astype(BF16), 'w_out': w_out[l].astype(BF16), 'w_q': w_q[l].astype(BF16),
            'peer_u': peer_u[l].astype(BF16), 'peer_v': peer_v[l].astype(BF16), 'sub_keys': sub_keys[l],
            'mu': row(mu_shift[l]), 'w0': row(w0[l]),
            'ww2': jnp.concatenate([w_w2[l], zpad], axis=0), 'a0': row(a0[l]),
            'wa2': jnp.concatenate([zpad, w_a2[l]], axis=0), 'wg2': w_g2[l].astype(BF16),
            'k_k': row(k_k[l]), 'k_a': row(k_a[l]), 'r_k': row(r_k[l]),
            'lnx_w': row(lnx_w[l]), 'lnx_b': row(lnx_b[l]), 'seg': seg, 'segt': seg.T,
            'conv_w': conv_w[l], 'conv_b': row(conv_b[l]),
            'wga': _pair_blockdiag(w_gate_a[l]).astype(BF16), 'bga': row(b_gate_a[l]),
            'wgi': _pair_blockdiag(w_gate_i[l]).astype(BF16), 'bgi': row(b_gate_i[l]),
            'lam': row(lru_lambda[l]),
            'ln1_w': ln1_w[l], 'ln1_b': ln1_b[l], 'ln2_w': ln2_w[l], 'ln2_b': ln2_b[l],
        }
        c_all = jnp.concatenate([c_prompt, c_sample], axis=0)
        mod = _ada(c_all, w_ada[l], b_ada[l])
        mod_p = mod[:bp, None, :]
        mod_s = mod[bp:, None, :]
        zmod = jnp.zeros((bs // SUBLANES, 1, 6 * D_MODEL), F32)
        pprev, _ = _inproj(state_shift[l].reshape(bs // SUBLANES, SUBLANES, D_MODEL), zmod, prm['w_in'], 2, SUBLANES)
        pprev = pprev.reshape(bs, C_TOT)[:, :C_A]
        yp, *st_p = _group(
            yp, mod_p, jnp.zeros((bp, C_A), F32), jnp.zeros((bp, N_HEADS, HEAD, HEAD), F32),
            jnp.zeros((bp, D_B), F32), jnp.zeros((bp, CONV_W - 1, D_B), F32), prm,
            ns_mix=1, tb_mix=ROWS, chunk=16, t_valid=ROWS, ns_big=1, tb_big=min(512, tp), alpha=alpha)
        ys, *st_s = _group(
            ys, mod_s, pprev, state_wkv[l], state_lru[l], state_conv[l], prm,
            ns_mix=ROWS // SAMPLE_TPAD, tb_mix=SAMPLE_TPAD, chunk=SAMPLE_TPAD, t_valid=ts,
            ns_big=min(64, bs), tb_big=SAMPLE_TPAD, alpha=alpha)
        for i, a in enumerate(st_p + st_s):
            outs[i].append(a)
    return (yp, ys[:, :ts]) + tuple(jnp.stack(o) for o in outs)
```

```python
import functools

import jax
import jax.numpy as jnp
from jax import lax
from jax.experimental import pallas as pl
from jax.experimental.pallas import tpu as pltpu

F32 = jnp.float32
BF16 = jnp.bfloat16

D_MODEL = 2048
D_A = 1024
D_B = 1024
HEAD = 64
N_HEADS = D_A // HEAD
N_PAIRS = N_HEADS // 2
R_W = 64
R_A = 64
R_G = 128
C_A = 3 * D_A + R_W + R_A + R_G
C_TOT = C_A + 2 * D_B
CONV_W = 4
C_LRU = 8.0
P_HEADS = 8
N_KEYS = 128
N_EXPERTS = N_KEYS * N_KEYS
TOPK = 16
D_QH = 128
LN_EPS = 1e-5
GN_EPS = 64e-5

LANES = 128
SUBLANES = 8
ROWS = 128
SAMPLE_TPAD = 8
VMEM_LIMIT = 56 * 1024 * 1024


def _cparams(sem):
    return pltpu.CompilerParams(dimension_semantics=sem, vmem_limit_bytes=VMEM_LIMIT)


def _split2(x):
    hi = x.astype(BF16)
    lo = (x - hi.astype(F32)).astype(BF16)
    return hi, lo


_NN = (((1,), (0,)), ((), ()))
_NT = (((1,), (1,)), ((), ()))
_TN = (((0,), (0,)), ((), ()))


def _dg(a, b, dims):
    return lax.dot_general(a, b, dims, preferred_element_type=F32)


def _dot1(a, b, dims=_NN):
    return _dg(a.astype(BF16), b.astype(BF16), dims)


def _dot3(a, b, dims=_NN):
    ah, al = _split2(a)
    bh, bl = _split2(b)
    return _dg(ah, bh, dims) + (_dg(ah, bl, dims) + _dg(al, bh, dims))


def _dot_exact_rhs(a, b_exact, dims=_NN):
    a1 = a.astype(BF16)
    r1 = a - a1.astype(F32)
    a2 = r1.astype(BF16)
    a3 = (r1 - a2.astype(F32)).astype(BF16)
    return _dg(a1, b_exact, dims) + (_dg(a2, b_exact, dims) + _dg(a3, b_exact, dims))


def _dot_exact_lhs(a_exact, b, dims=_NN):
    b1 = b.astype(BF16)
    r1 = b - b1.astype(F32)
    b2 = r1.astype(BF16)
    b3 = (r1 - b2.astype(F32)).astype(BF16)
    return _dg(a_exact, b1, dims) + (_dg(a_exact, b2, dims) + _dg(a_exact, b3, dims))


def _sigmoid(x):
    return 1.0 / (1.0 + jnp.exp(-x))


def _softplus(x):
    return jnp.maximum(x, 0.0) + jnp.log(1.0 + jnp.exp(-jnp.abs(x)))


def _gelu(x):
    return 0.5 * x * (1.0 + jnp.tanh(0.7978845608028654 * (x + 0.044715 * (x * x * x))))


def _iota(shape, dim):
    return lax.broadcasted_iota(jnp.int32, shape, dim)


def _ada_kernel(c_ref, w_ref, b_ref, o_ref):
    c = c_ref[...]
    s = c * _sigmoid(c)
    o_ref[...] = _dot3(s, w_ref[...]) + b_ref[...]


def _ada(c, w_ada, b_ada):
    n = c.shape[0]
    tn = 512
    return pl.pallas_call(
        _ada_kernel,
        out_shape=jax.ShapeDtypeStruct((n, 6 * D_MODEL), F32),
        grid=(6 * D_MODEL // tn,),
        in_specs=[pl.BlockSpec((n, D_MODEL), lambda j: (0, 0)),
                  pl.BlockSpec((D_MODEL, tn), lambda j: (0, j)),
                  pl.BlockSpec((1, tn), lambda j: (0, j))],
        out_specs=pl.BlockSpec((n, tn), lambda j: (0, j)),
        compiler_params=_cparams(("arbitrary",)),
        name="ada",
    )(c, w_ada, b_ada.reshape(1, -1))


def _inproj_kernel(x_ref, sc_ref, sh_ref, w_ref, o_ref, h_ref):
    h = x_ref[...] * (1.0 + sc_ref[...]) + sh_ref[...]
    ns, tb, _ = h.shape
    h_ref[...] = h[:, tb - SUBLANES:, :]
    p = _dot1(h.reshape(ns * tb, D_MODEL), w_ref[...])
    o_ref[...] = p.reshape(ns, tb, p.shape[-1])


def _inproj(x, mod, w_bf16, ns, tb):
    S, T, _ = x.shape
    ncol = w_bf16.shape[1]
    tn = ncol // 3
    proj, h_last = pl.pallas_call(
        _inproj_kernel,
        out_shape=(jax.ShapeDtypeStruct((S, T, ncol), F32),
                   jax.ShapeDtypeStruct((3, S, SUBLANES, D_MODEL), F32)),
        grid=(3, S // ns, T // tb),
        in_specs=[pl.BlockSpec((ns, tb, D_MODEL), lambda j, s, t: (s, t, 0)),
                  pl.BlockSpec((ns, 1, D_MODEL), lambda j, s, t: (s, 0, 1)),
                  pl.BlockSpec((ns, 1, D_MODEL), lambda j, s, t: (s, 0, 0)),
                  pl.BlockSpec((D_MODEL, tn), lambda j, s, t: (0, j))],
        out_specs=(pl.BlockSpec((ns, tb, tn), lambda j, s, t: (s, t, j)),
                   pl.BlockSpec((None, ns, SUBLANES, D_MODEL), lambda j, s, t: (j, s, 0, 0))),
        compiler_params=_cparams(("arbitrary", "arbitrary", "arbitrary")),
        name="inproj",
    )(x, mod, mod, w_bf16)
    return proj, h_last[0]


def _mixer_kernel(proj_ref, init_ref, s0_ref, lru0_ref,
                  mu_ref, w0_ref, ww2_ref, a0_ref, wa2_ref, wg2_ref, kk_ref, ka_ref, rk_ref,
                  lnw_ref, lnb_ref, seg_ref, segt_ref,
                  cw_ref, cb_ref, wga_ref, bga_ref, wgi_ref, bgi_ref, lam_ref,
                  y_ref, sout_ref, lruo_ref, convo_ref,
                  prev_ref, hst_ref,
                  bp_ref, rp_ref, x2_ref, y2_ref, v_ref, kb_ref, ab_ref, wt_ref, yacc_ref,
                  *, ns, tb, chunk, t_valid):
    R = ns * tb
    t = pl.program_id(1)

    @pl.when(t == 0)
    def _():
        prev_ref[...] = init_ref[...]
        sout_ref[...] = s0_ref[...]
        hst_ref[...] = lru0_ref[...]

    cur = proj_ref[...]
    ext = jnp.concatenate([prev_ref[...], cur[:, :, :C_A + D_B]], axis=1)
    prev_ref[...] = cur[:, tb - SUBLANES:, :C_A + D_B]

    row = _iota((R, 1), 0)
    rowin = row % tb
    valid = rowin < t_valid

    pa = cur[:, :, :C_A].reshape(R, C_A)
    shifted = ext[:, SUBLANES - 1:SUBLANES - 1 + tb, :C_A].reshape(R, C_A)
    p = pa + (shifted - pa) * mu_ref[...]
    r = p[:, :D_A]
    k = p[:, D_A:2 * D_A]
    v = p[:, 2 * D_A:3 * D_A]
    xwa = p[:, 3 * D_A:3 * D_A + LANES]
    xg = p[:, 3 * D_A + LANES:]
    w_log = -_softplus(-(w0_ref[...] + _dot3(jnp.tanh(xwa), ww2_ref[...]))) - 0.5
    logw = -jnp.exp(w_log)
    a = _sigmoid(a0_ref[...] + _dot3(xwa, wa2_ref[...]))
    g = _dot1(_sigmoid(xg), wg2_ref[...])
    seg = seg_ref[...]
    segt = segt_ref[...]

    def seg_sum(x):
        return _dot_exact_rhs(_dot_exact_rhs(x, seg), segt)

    kk = k * kk_ref[...]
    kk = kk / jnp.maximum(jnp.sqrt(seg_sum(kk * kk)), 1e-12)
    k = k * (1.0 + (a - 1.0) * ka_ref[...])
    kka = kk * a
    bonus = seg_sum(r * k * rk_ref[...]) * v
    logw = jnp.where(valid, logw, 0.0)
    r = jnp.where(valid, r, 0.0)
    k = jnp.where(valid, k, 0.0)
    v = jnp.where(valid, v, 0.0)
    kk = jnp.where(valid, kk, 0.0)
    kka = jnp.where(valid, kka, 0.0)

    ri = _iota((R, R), 0)
    ci = _iota((R, R), 1)
    same = (ri // chunk) == (ci // chunk)
    incl = same & (ci <= ri)
    strict = same & (ci < ri)
    lmat = jnp.concatenate([jnp.where(incl, 1.0, 0.0), jnp.where(same, 1.0, 0.0)], axis=0).astype(BF16)
    cums = _dot_exact_lhs(lmat, logw)
    cum = cums[:R]
    tot = cums[R:]
    e_prev = jnp.exp(cum - logw)
    e_neg = jnp.exp(-cum)
    e_pos = jnp.exp(cum)
    e_rem = jnp.exp(tot - cum)
    beta = kk * e_prev
    alpha = kka * e_neg
    kappa = k * e_neg
    rho = r * e_pos
    v_ref[...] = v
    kb_ref[...] = k * e_rem
    ab_ref[...] = -(kka * e_rem)
    wt_ref[...] = jnp.exp(tot)

    lane = _iota((1, LANES), 1)
    masks = (lane < HEAD, lane >= HEAD)
    eye = jnp.where(ri == ci, 1.0, 0.0)
    nsq = {16: 3, 8: 2}[chunk]
    heads = [(pr, m) for pr in range(N_PAIRS) for m in masks]
    psl = lambda pr: slice(pr * LANES, (pr + 1) * LANES)
    bms = [jnp.where(m, beta[:, psl(pr)], 0.0) for pr, m in heads]
    rms = [jnp.where(m, rho[:, psl(pr)], 0.0) for pr, m in heads]
    gms = [_dot3(jnp.concatenate([bm, rm], axis=0),
                 jnp.concatenate([alpha[:, psl(pr)], kappa[:, psl(pr)]], axis=0), _NT)
           for (pr, m), bm, rm in zip(heads, bms, rms)]
    l_bas = [jnp.where(strict, gm[:R, :R], 0.0) for gm in gms]
    m_ras = [jnp.where(incl, gm[R:, :R], 0.0) for gm in gms]
    xy1s = [_dot3(jnp.concatenate([jnp.where(strict, gm[:R, R:], 0.0), jnp.where(incl, gm[R:, R:], 0.0)], axis=0),
                  v[:, psl(pr)])
            for (pr, m), gm in zip(heads, gms)]
    pws = [_dot3(l, l) for l in l_bas]
    minvs = [eye - l for l in l_bas]
    for i in range(nsq):
        minvs = [mi + _dot3(mi, pw) for mi, pw in zip(minvs, pws)]
        if i + 1 < nsq:
            pws = [_dot3(pw, pw) for pw in pws]
    t1s = [_dot3(mi, jnp.concatenate([xy1[:R], bm], axis=1))
           for mi, xy1, bm in zip(minvs, xy1s, bms)]
    t2s = [_dot3(m_ra, t1) for m_ra, t1 in zip(m_ras, t1s)]
    for pr in range(N_PAIRS):
        h0, h1 = 2 * pr, 2 * pr + 1
        sl = psl(pr)
        m1 = masks[1]
        bp_ref[:, sl] = t1s[h0][:, LANES:] + t1s[h1][:, LANES:]
        rp_ref[:, sl] = (rms[h0] - t2s[h0][:, LANES:]) + (rms[h1] - t2s[h1][:, LANES:])
        x2_ref[:, sl] = jnp.where(m1, t1s[h1][:, :LANES], t1s[h0][:, :LANES])
        y2_ref[:, sl] = jnp.where(m1, xy1s[h1][R:] - t2s[h1][:, :LANES], xy1s[h0][R:] - t2s[h0][:, :LANES])

    bi = _iota((LANES, LANES), 0)
    bj = _iota((LANES, LANES), 1)
    bd = (bi < HEAD) == (bj < HEAD)
    for s in range(ns):
        sps = [sout_ref[s, pr] for pr in range(N_PAIRS)]
        for c in range(tb // chunk):
            rs = slice(s * tb + c * chunk, s * tb + (c + 1) * chunk)
            pms = [_dot3(jnp.concatenate([bp_ref[rs, psl(pr)], rp_ref[rs, psl(pr)]], axis=0), sps[pr], _NT)
                   for pr in range(N_PAIRS)]
            upds = []
            for pr in range(N_PAIRS):
                sl = psl(pr)
                u = pms[pr][:chunk] + x2_ref[rs, sl]
                yacc_ref[rs, sl] = pms[pr][chunk:] + y2_ref[rs, sl]
                vu = jnp.concatenate([v_ref[rs, sl], u], axis=0)
                ka = jnp.concatenate([kb_ref[rs, sl], ab_ref[rs, sl]], axis=0)
                upds.append(_dot3(vu, ka, _TN))
            sps = [jnp.where(bd, sps[pr] * wt_ref[rs.start:rs.start + 1, psl(pr)] + upds[pr], 0.0)
                   for pr in range(N_PAIRS)]
        for pr in range(N_PAIRS):
            sout_ref[s, pr] = sps[pr]

    y = yacc_ref[...]
    mean = seg_sum(y) * (1.0 / HEAD)
    yc = y - mean
    var = seg_sum(yc * yc) * (1.0 / HEAD)
    yn = yc * lax.rsqrt(var + GN_EPS) * lnw_ref[...] + lnb_ref[...]
    ya = (yn + bonus) * g
    y_ref[:, :, :D_A] = ya.reshape(ns, tb, D_A)

    gb = cur[:, :, C_A + D_B:].reshape(R, D_B)
    xc = cb_ref[...]
    for j in range(CONV_W):
        off = SUBLANES - (CONV_W - 1) + j
        xc = xc + ext[:, off:off + tb, C_A:].reshape(R, D_B) * cw_ref[j:j + 1, :]
    gr = []
    gi = []
    for pr in range(N_PAIRS):
        sl = slice(pr * LANES, (pr + 1) * LANES)
        gr.append(_dot1(xc[:, sl], wga_ref[pr]))
        gi.append(_dot1(xc[:, sl], wgi_ref[pr]))
    rg = _sigmoid(jnp.concatenate(gr, axis=1) + bga_ref[...])
    ig = _sigmoid(jnp.concatenate(gi, axis=1) + bgi_ref[...])
    log_a = -C_LRU * rg * _softplus(-lam_ref[...])
    av = jnp.exp(log_a)
    bv = jnp.sqrt(1.0 - jnp.exp(2.0 * log_a)) * (ig * xc)
    av = jnp.where(valid, av, 1.0)
    bv = jnp.where(valid, bv, 0.0)
    d = 1
    while d < tb:
        take = rowin >= d
        a_sh = jnp.where(take, pltpu.roll(av, d, axis=0), 1.0)
        b_sh = jnp.where(take, pltpu.roll(bv, d, axis=0), 0.0)
        bv = av * b_sh + bv
        av = av * a_sh
        d *= 2
    h0 = jnp.broadcast_to(hst_ref[:, SUBLANES - 1:SUBLANES, :], (ns, tb, D_B)).reshape(R, D_B)
    h = av * h0 + bv
    h3 = h.reshape(ns, tb, D_B)
    hst_ref[...] = h3[:, tb - SUBLANES:, :]
    y_ref[:, :, D_A:] = (h * _gelu(gb)).reshape(ns, tb, D_B)

    @pl.when(t == pl.num_programs(1) - 1)
    def _():
        lruo_ref[...] = h3[:, tb - SUBLANES:, :]
        convo_ref[...] = cur[:, tb - SUBLANES:, C_A:C_A + D_B]


def _mixer(proj, init8, s0, lru0, prm, ns, tb, chunk, t_valid):
    S, T, _ = proj.shape
    R = ns * tb
    assert R == ROWS
    kern = functools.partial(_mixer_kernel, ns=ns, tb=tb, chunk=chunk, t_valid=t_valid)

    def full(a):
        nd = a.ndim
        return pl.BlockSpec(a.shape, lambda s, t, nd=nd: (0,) * nd)

    params = [prm[n] for n in ('mu', 'w0', 'ww2', 'a0', 'wa2', 'wg2', 'k_k', 'k_a', 'r_k', 'lnx_w', 'lnx_b',
                               'seg', 'segt', 'conv_w', 'conv_b', 'wga', 'bga', 'wgi', 'bgi', 'lam')]
    slab = lambda w: pltpu.VMEM((R, w), F32)
    return pl.pallas_call(
        kern,
        out_shape=(jax.ShapeDtypeStruct((S, T, D_A + D_B), F32),
                   jax.ShapeDtypeStruct((S, N_PAIRS, LANES, LANES), F32),
                   jax.ShapeDtypeStruct((S, SUBLANES, D_B), F32),
                   jax.ShapeDtypeStruct((S, SUBLANES, D_B), F32)),
        grid=(S // ns, T // tb),
        in_specs=[pl.BlockSpec((ns, tb, C_TOT), lambda s, t: (s, t, 0)),
                  pl.BlockSpec((ns, SUBLANES, C_A + D_B), lambda s, t: (s, 0, 0)),
                  pl.BlockSpec((ns, N_PAIRS, LANES, LANES), lambda s, t: (s, 0, 0, 0),
                               pipeline_mode=pl.Buffered(1)),
                  pl.BlockSpec((ns, SUBLANES, D_B), lambda s, t: (s, 0, 0))]
                 + [full(a) for a in params],
        out_specs=(pl.BlockSpec((ns, tb, D_A + D_B), lambda s, t: (s, t, 0)),
                   pl.BlockSpec((ns, N_PAIRS, LANES, LANES), lambda s, t: (s, 0, 0, 0),
                                pipeline_mode=pl.Buffered(1)),
                   pl.BlockSpec((ns, SUBLANES, D_B), lambda s, t: (s, 0, 0)),
                   pl.BlockSpec((ns, SUBLANES, D_B), lambda s, t: (s, 0, 0))),
        scratch_shapes=[pltpu.VMEM((ns, SUBLANES, C_A + D_B), F32),
                        pltpu.VMEM((ns, SUBLANES, D_B), F32)]
                       + [slab(D_A) for _ in range(9)],
        compiler_params=_cparams(("arbitrary", "arbitrary")),
        name="mixer",
    )(proj, init8, s0, lru0, *params)


def _layernorm(x, w, b):
    mean = jnp.mean(x, axis=-1, keepdims=True)
    xc = x - mean
    var = jnp.mean(xc * xc, axis=-1, keepdims=True)
    return xc * lax.rsqrt(var + LN_EPS) * w + b


def _outproj_kernel(y_ref, x_ref, g1_ref, sc2_ref, sh2_ref, w_ref, lnw_ref, lnb_ref, x1_ref, h2_ref, *, alpha):
    ns, tb, _ = y_ref.shape
    mix = _dot1(y_ref[...].reshape(ns * tb, D_MODEL), w_ref[...]).reshape(ns, tb, D_MODEL)
    x1 = _layernorm(alpha * x_ref[...] + (1.0 + g1_ref[...]) * mix, lnw_ref[...], lnb_ref[...])
    x1_ref[...] = x1
    h2_ref[...] = x1 * (1.0 + sc2_ref[...]) + sh2_ref[...]


def _outproj(y, x, mod, w_bf16, ln_w, ln_b, ns, tb, alpha):
    S, T, _ = x.shape
    blk = pl.BlockSpec((ns, tb, D_MODEL), lambda s, t: (s, t, 0))
    modspec = lambda i: pl.BlockSpec((ns, 1, D_MODEL), lambda s, t, i=i: (s, 0, i))
    vec = pl.BlockSpec((1, D_MODEL), lambda s, t: (0, 0))
    return pl.pallas_call(
        functools.partial(_outproj_kernel, alpha=alpha),
        out_shape=(jax.ShapeDtypeStruct((S, T, D_MODEL), F32), jax.ShapeDtypeStruct((S, T, D_MODEL), F32)),
        grid=(S // ns, T // tb),
        in_specs=[blk, blk, modspec(2), modspec(4), modspec(3),
                  pl.BlockSpec((D_MODEL, D_MODEL), lambda s, t: (0, 0)), vec, vec],
        out_specs=(blk, blk),
        compiler_params=_cparams(("arbitrary", "arbitrary")),
        name="outproj",
    )(y, x, mod, mod, mod, w_bf16, ln_w.reshape(1, -1), ln_b.reshape(1, -1))


def _topk_rows(s, ridx, n_rows):
    out_i = _iota((TOPK, s.shape[1]), 0)
    vals = jnp.zeros((TOPK, s.shape[1]), F32)
    idxs = jnp.zeros((TOPK, s.shape[1]), F32)
    for it in range(TOPK):
        m = jnp.max(s, axis=0, keepdims=True)
        idx = jnp.min(jnp.where(s == m, ridx, float(n_rows)), axis=0, keepdims=True)
        vals = jnp.where(out_i == it, m, vals)
        idxs = jnp.where(out_i == it, idx, idxs)
        s = jnp.where(ridx == idx, -jnp.inf, s)
    return vals, idxs


def _route_kernel(h_ref, wq_ref, sk_ref, g_ref, gate_ref, e1_ref, e2_ref):
    R = ROWS
    q = _dot1(h_ref[...], wq_ref[...])
    sk0 = sk_ref[0]
    sk1 = sk_ref[1]
    gates, e1s, e2s = [], [], []
    key_rows = _iota((N_KEYS, R), 0).astype(F32)
    r16 = _iota((TOPK, R), 0).astype(F32)
    r8 = _iota((SUBLANES, R), 0).astype(F32)
    ea_rank = jnp.where(r8 < 3, 2.0, jnp.where(r8 < 5, 3.0, 4.0))
    eb_rank = jnp.where((r8 == 0) | (r8 == 3) | (r8 == 5), 2.0, jnp.where((r8 == 1) | (r8 == 4), 3.0, 4.0))
    flat = jnp.concatenate([r16, TOPK + r8, TOPK * r16, TOPK * r8 + 1.0, TOPK * ea_rank + eb_rank], axis=0)
    neg = -jnp.inf
    for hd in range(P_HEADS):
        base = hd * 2 * D_QH
        s1 = _dot1(sk0, q[:, base:base + D_QH], _NT)
        s2 = _dot1(sk1, q[:, base + D_QH:base + 2 * D_QH], _NT)
        v1, i1 = _topk_rows(s1, key_rows, N_KEYS)
        v2, i2 = _topk_rows(s2, key_rows, N_KEYS)
        ea = jnp.where(r8 < 3, v1[2:3], jnp.where(r8 < 5, v1[3:4], v1[4:5]))
        eb = jnp.where(eb_rank == 2.0, v2[2:3], jnp.where(eb_rank == 3.0, v2[3:4], v2[4:5]))
        cand = jnp.concatenate([
            v1[0:1] + v2,
            v1[1:2] + v2[:SUBLANES],
            jnp.where(r16 >= 2, v1 + v2[0:1], neg),
            jnp.where(r8 >= 2, v1[:SUBLANES] + v2[1:2], neg),
            jnp.where(r8 < 6, ea + eb, neg),
        ], axis=0)
        sv, ci = _topk_rows(cand, flat, TOPK * TOPK)
        ca = jnp.floor(ci * (1.0 / TOPK))
        cb = ci - TOPK * ca
        e1 = jnp.zeros((TOPK, R), F32)
        e2 = jnp.zeros((TOPK, R), F32)
        for a in range(TOPK):
            e1 = jnp.where(ca == float(a), i1[a:a + 1, :], e1)
            e2 = jnp.where(cb == float(a), i2[a:a + 1, :], e2)
        ex = jnp.exp(sv - jnp.max(sv, axis=0, keepdims=True))
        gates.append(ex / jnp.sum(ex, axis=0, keepdims=True))
        e1s.append(e1)
        e2s.append(e2)
    gate_ref[...] = jnp.concatenate(gates, axis=0).T
    e1_ref[...] = jnp.concatenate(e1s, axis=0).T
    e2_ref[...] = jnp.concatenate(e2s, axis=0).T
    key_i = _iota((N_KEYS, P_HEADS * TOPK), 0).astype(F32)

    def per_octet(o, carry):
        base = pl.multiple_of(o * SUBLANES, SUBLANES)
        tiles = []
        for j in range(SUBLANES):
            sel1 = key_i == e1_ref[pl.ds(base + j, 1), :]
            sel2 = key_i == e2_ref[pl.ds(base + j, 1), :]
            m1 = jnp.where(sel1, gate_ref[pl.ds(base + j, 1), :], 0.0)
            m2 = jnp.where(sel2, 1.0, 0.0)
            tiles.append(_dot1(m1, m2, _NT))
        by_key = jnp.swapaxes(jnp.stack(tiles, axis=0), 0, 1)
        for i1v in range(N_KEYS):
            g_ref[pl.ds(base, SUBLANES), i1v * N_KEYS:(i1v + 1) * N_KEYS] = by_key[i1v]
        return carry

    lax.fori_loop(0, R // SUBLANES, per_octet, 0)


def _route(h2, wq_bf16, sub_keys):
    n = h2.shape[0]
    return pl.pallas_call(
        _route_kernel,
        out_shape=jax.ShapeDtypeStruct((n, N_EXPERTS), F32),
        grid=(n // ROWS,),
        in_specs=[pl.BlockSpec((ROWS, D_MODEL), lambda i: (i, 0)),
                  pl.BlockSpec((D_MODEL, D_MODEL), lambda i: (0, 0)),
                  pl.BlockSpec((2, N_KEYS, D_QH), lambda i: (0, 0, 0))],
        out_specs=pl.BlockSpec((ROWS, N_EXPERTS), lambda i: (i, 0)),
        scratch_shapes=[pltpu.VMEM((ROWS, P_HEADS * TOPK), F32) for _ in range(3)],
        compiler_params=_cparams(("arbitrary",)),
        name="route",
    )(h2, wq_bf16, sub_keys)


def _experts_kernel(h_ref, g_ref, u_ref, v_ref, x1_ref, g2_ref, lnw_ref, lnb_ref, o_ref, acc_ref, *, alpha):
    e = pl.program_id(2)

    @pl.when(e == 0)
    def _():
        acc_ref[...] = jnp.zeros_like(acc_ref)

    ns, tb, _ = h_ref.shape
    h = h_ref[...].reshape(ns * tb, D_MODEL)
    act = _dot1(h, u_ref[...], _NT)
    w = g_ref[...] * _gelu(act)
    acc_ref[...] += _dot1(w, v_ref[...])

    @pl.when(e == pl.num_programs(2) - 1)
    def _():
        ff = acc_ref[...].reshape(ns, tb, D_MODEL)
        o_ref[...] = _layernorm(alpha * x1_ref[...] + (1.0 + g2_ref[...]) * ff, lnw_ref[...], lnb_ref[...])


def _experts(h2, gmat, u_bf16, v_bf16, x1, mod, ln_w, ln_b, ns, tb, alpha):
    S, T, _ = x1.shape
    ec = 1024
    blk = pl.BlockSpec((ns, tb, D_MODEL), lambda s, t, e: (s, t, 0))
    vec = pl.BlockSpec((1, D_MODEL), lambda s, t, e: (0, 0))
    rows = ns * tb
    nt = T // tb
    return pl.pallas_call(
        functools.partial(_experts_kernel, alpha=alpha),
        out_shape=jax.ShapeDtypeStruct((S, T, D_MODEL), F32),
        grid=(S // ns, nt, N_EXPERTS // ec),
        in_specs=[blk,
                  pl.BlockSpec((rows, ec), lambda s, t, e: (s * nt + t, e)),
                  pl.BlockSpec((ec, D_MODEL), lambda s, t, e: (e, 0)),
                  pl.BlockSpec((ec, D_MODEL), lambda s, t, e: (e, 0)),
                  blk,
                  pl.BlockSpec((ns, 1, D_MODEL), lambda s, t, e: (s, 0, 5)),
                  vec, vec],
        out_specs=blk,
        scratch_shapes=[pltpu.VMEM((rows, D_MODEL), F32)],
        compiler_params=_cparams(("arbitrary", "arbitrary", "arbitrary")),
        name="experts",
    )(h2, gmat, u_bf16, v_bf16, x1, mod, ln_w.reshape(1, -1), ln_b.reshape(1, -1))


def _pair_blockdiag(w):
    n = w.shape[0] // 2
    w = w.reshape(n, 2, HEAD, HEAD)
    z = jnp.zeros((n, HEAD, HEAD), w.dtype)
    top = jnp.concatenate([w[:, 0], z], axis=2)
    bot = jnp.concatenate([z, w[:, 1]], axis=2)
    return jnp.concatenate([top, bot], axis=1)


def _pair_diag_extract(sp):
    S = sp.shape[0]
    a = sp[:, :, :HEAD, :HEAD]
    b = sp[:, :, HEAD:, HEAD:]
    return jnp.stack([a, b], axis=2).reshape(S, N_HEADS, HEAD, HEAD)


def _group(x, mod, p_prev, wkv0, lru0, conv0, prm, *, ns_mix, tb_mix, chunk, t_valid, ns_big, tb_big, alpha):
    S, T, _ = x.shape
    proj, h_last = _inproj(x, mod, prm['w_in'], ns_big, tb_big)
    init8 = jnp.zeros((S, SUBLANES, C_A + D_B), F32)
    init8 = init8.at[:, SUBLANES - 1, :C_A].set(p_prev)
    init8 = init8.at[:, SUBLANES - (CONV_W - 1):, C_A:].set(conv0)
    s0 = _pair_blockdiag(wkv0.reshape(S * N_HEADS, HEAD, HEAD)).reshape(S, N_PAIRS, LANES, LANES)
    lru8 = jnp.broadcast_to(lru0[:, None, :], (S, SUBLANES, D_B))
    y, sp, lru_o, conv_o = _mixer(proj, init8, s0, lru8, prm, ns_mix, tb_mix, chunk, t_valid)
    x1, h2 = _outproj(y, x, mod, prm['w_out'], prm['ln1_w'], prm['ln1_b'], ns_big, tb_big, alpha)
    gmat = _route(h2.reshape(S * T, D_MODEL), prm['w_q'], prm['sub_keys'])
    out = _experts(h2, gmat, prm['peer_u'], prm['peer_v'], x1, mod, prm['ln2_w'], prm['ln2_b'],
                   ns_big, tb_big, alpha)
    tv = (t_valid - 1) % SUBLANES
    shift = h_last[:, tv]
    wkv = _pair_diag_extract(sp)
    lru = lru_o[:, SUBLANES - 1]
    conv = conv_o[:, tv - (CONV_W - 2):tv + 1]
    return out, shift, wkv, lru, conv


def kernel(x_prompt, x_sample, c_prompt, c_sample, state_shift, state_wkv, state_lru, state_conv, w_ada, b_ada, w_in, mu_shift, w0, w_w2, a0, w_a2, w_g2, k_k, k_a, r_k, lnx_w, lnx_b, conv_w, conv_b, w_gate_a, b_gate_a, w_gate_i, b_gate_i, lru_lambda, w_out, ln1_w, ln1_b, w_q, sub_keys, peer_u, peer_v, ln2_w, ln2_b):
    depth = w_ada.shape[0]
    alpha = (2 * depth) ** 0.25
    bp, tp, _ = x_prompt.shape
    bs, ts, _ = x_sample.shape
    yp = x_prompt
    ys = jnp.pad(x_sample, ((0, 0), (0, SAMPLE_TPAD - ts), (0, 0)))
    head_of = jnp.arange(D_A) // HEAD
    seg = (head_of[:, None] == jnp.arange(LANES)[None, :]).astype(BF16)
    row = lambda a: a.reshape(1, -1)
    outs = [[] for _ in range(8)]
    for l in range(depth):
        zpad = jnp.zeros((LANES - R_W, D_A), F32)
        prm = {
            'w_in': w_in[l].astype(BF16), 'w_out': w_out[l].astype(BF16), 'w_q': w_q[l].astype(BF16),
            'peer_u': peer_u[l].astype(BF16), 'peer_v': peer_v[l].astype(BF16), 'sub_keys': sub_keys[l],
            'mu': row(mu_shift[l]), 'w0': row(w0[l]),
            'ww2': jnp.concatenate([w_w2[l], zpad], axis=0), 'a0': row(a0[l]),
            'wa2': jnp.concatenate([zpad, w_a2[l]], axis=0), 'wg2': w_g2[l].astype(BF16),
            'k_k': row(k_k[l]), 'k_a': row(k_a[l]), 'r_k': row(r_k[l]),
            'lnx_w': row(lnx_w[l]), 'lnx_b': row(lnx_b[l]), 'seg': seg, 'segt': seg.T,
            'conv_w': conv_w[l], 'conv_b': row(conv_b[l]),
            'wga': _pair_blockdiag(w_gate_a[l]).astype(BF16), 'bga': row(b_gate_a[l]),
            'wgi': _pair_blockdiag(w_gate_i[l]).astype(BF16), 'bgi': row(b_gate_i[l]),
            'lam': row(lru_lambda[l]),
            'ln1_w': ln1_w[l], 'ln1_b': ln1_b[l], 'ln2_w': ln2_w[l], 'ln2_b': ln2_b[l],
        }
        c_all = jnp.concatenate([c_prompt, c_sample], axis=0)
        mod = _ada(c_all, w_ada[l], b_ada[l])
        mod_p = mod[:bp, None, :]
        mod_s = mod[bp:, None, :]
        zmod = jnp.zeros((bs // SUBLANES, 1, 6 * D_MODEL), F32)
        pprev, _ = _inproj(state_shift[l].reshape(bs // SUBLANES, SUBLANES, D_MODEL), zmod, prm['w_in'], 2, SUBLANES)
        pprev = pprev.reshape(bs, C_TOT)[:, :C_A]
        yp, *st_p = _group(
            yp, mod_p, jnp.zeros((bp, C_A), F32), jnp.zeros((bp, N_HEADS, HEAD, HEAD), F32),
            jnp.zeros((bp, D_B), F32), jnp.zeros((bp, CONV_W - 1, D_B), F32), prm,
            ns_mix=1, tb_mix=ROWS, chunk=16, t_valid=ROWS, ns_big=1, tb_big=min(512, tp), alpha=alpha)
        ys, *st_s = _group(
            ys, mod_s, pprev, state_wkv[l], state_lru[l], state_conv[l], prm,
            ns_mix=ROWS // SAMPLE_TPAD, tb_mix=SAMPLE_TPAD, chunk=SAMPLE_TPAD, t_valid=ts,
            ns_big=min(64, bs), tb_big=SAMPLE_TPAD, alpha=alpha)
        for i, a in enumerate(st_p + st_s):
            outs[i].append(a)
    return (yp, ys[:, :ts]) + tuple(jnp.stack(o) for o in outs)
```

```python
import functools

import jax
import jax.numpy as jnp
from jax import lax
from jax.experimental import pallas as pl
from jax.experimental.pallas import tpu as pltpu

F32 = jnp.float32
BF16 = jnp.bfloat16

D_MODEL = 2048
D_A = 1024
D_B = 1024
HEAD = 64
N_HEADS = D_A // HEAD
N_PAIRS = N_HEADS // 2
R_W = 64
R_A = 64
R_G = 128
C_A = 3 * D_A + R_W + R_A + R_G
C_TOT = C_A + 2 * D_B
CONV_W = 4
C_LRU = 8.0
P_HEADS = 8
N_KEYS = 128
N_EXPERTS = N_KEYS * N_KEYS
TOPK = 16
D_QH = 128
LN_EPS = 1e-5
GN_EPS = 64e-5

LANES = 128
SUBLANES = 8
ROWS = 128
SAMPLE_TPAD = 8
VMEM_LIMIT = 56 * 1024 * 1024


def _cparams(sem):
    return pltpu.CompilerParams(dimension_semantics=sem, vmem_limit_bytes=VMEM_LIMIT)


def _split2(x):
    hi = x.astype(BF16)
    lo = (x - hi.astype(F32)).astype(BF16)
    return hi, lo


_NN = (((1,), (0,)), ((), ()))
_NT = (((1,), (1,)), ((), ()))
_TN = (((0,), (0,)), ((), ()))


def _dg(a, b, dims):
    return lax.dot_general(a, b, dims, preferred_element_type=F32)


def _dot1(a, b, dims=_NN):
    return _dg(a.astype(BF16), b.astype(BF16), dims)


def _dot3(a, b, dims=_NN):
    ah, al = _split2(a)
    bh, bl = _split2(b)
    return _dg(ah, bh, dims) + (_dg(ah, bl, dims) + _dg(al, bh, dims))


_state_dot = _dot1


def _dot_exact_rhs(a, b_exact, dims=_NN):
    a1 = a.astype(BF16)
    r1 = a - a1.astype(F32)
    a2 = r1.astype(BF16)
    a3 = (r1 - a2.astype(F32)).astype(BF16)
    return _dg(a1, b_exact, dims) + (_dg(a2, b_exact, dims) + _dg(a3, b_exact, dims))


def _dot_exact_lhs(a_exact, b, dims=_NN):
    b1 = b.astype(BF16)
    r1 = b - b1.astype(F32)
    b2 = r1.astype(BF16)
    b3 = (r1 - b2.astype(F32)).astype(BF16)
    return _dg(a_exact, b1, dims) + (_dg(a_exact, b2, dims) + _dg(a_exact, b3, dims))


def _sigmoid(x):
    return 1.0 / (1.0 + jnp.exp(-x))


def _softplus(x):
    return jnp.maximum(x, 0.0) + jnp.log(1.0 + jnp.exp(-jnp.abs(x)))


def _gelu(x):
    return 0.5 * x * (1.0 + jnp.tanh(0.7978845608028654 * (x + 0.044715 * (x * x * x))))


def _iota(shape, dim):
    return lax.broadcasted_iota(jnp.int32, shape, dim)


def _ada_kernel(c_ref, w_ref, b_ref, o_ref):
    c = c_ref[...]
    s = c * _sigmoid(c)
    o_ref[...] = _dot3(s, w_ref[...]) + b_ref[...]


def _ada(c, w_ada, b_ada):
    n = c.shape[0]
    tn = 512
    return pl.pallas_call(
        _ada_kernel,
        out_shape=jax.ShapeDtypeStruct((n, 6 * D_MODEL), F32),
        grid=(6 * D_MODEL // tn,),
        in_specs=[pl.BlockSpec((n, D_MODEL), lambda j: (0, 0)),
                  pl.BlockSpec((D_MODEL, tn), lambda j: (0, j)),
                  pl.BlockSpec((1, tn), lambda j: (0, j))],
        out_specs=pl.BlockSpec((n, tn), lambda j: (0, j)),
        compiler_params=_cparams(("arbitrary",)),
        name="ada",
    )(c, w_ada, b_ada.reshape(1, -1))


def _inproj_kernel(x_ref, sc_ref, sh_ref, w_ref, o_ref, h_ref):
    h = x_ref[...] * (1.0 + sc_ref[...]) + sh_ref[...]
    ns, tb, _ = h.shape
    h_ref[...] = h[:, tb - SUBLANES:, :]
    p = _dot1(h.reshape(ns * tb, D_MODEL), w_ref[...])
    o_ref[...] = p.reshape(ns, tb, p.shape[-1])


def _inproj(x, mod, w_bf16, ns, tb):
    S, T, _ = x.shape
    ncol = w_bf16.shape[1]
    tn = ncol // 3
    proj, h_last = pl.pallas_call(
        _inproj_kernel,
        out_shape=(jax.ShapeDtypeStruct((S, T, ncol), F32),
                   jax.ShapeDtypeStruct((3, S, SUBLANES, D_MODEL), F32)),
        grid=(3, S // ns, T // tb),
        in_specs=[pl.BlockSpec((ns, tb, D_MODEL), lambda j, s, t: (s, t, 0)),
                  pl.BlockSpec((ns, 1, D_MODEL), lambda j, s, t: (s, 0, 1)),
                  pl.BlockSpec((ns, 1, D_MODEL), lambda j, s, t: (s, 0, 0)),
                  pl.BlockSpec((D_MODEL, tn), lambda j, s, t: (0, j))],
        out_specs=(pl.BlockSpec((ns, tb, tn), lambda j, s, t: (s, t, j)),
                   pl.BlockSpec((None, ns, SUBLANES, D_MODEL), lambda j, s, t: (j, s, 0, 0))),
        compiler_params=_cparams(("arbitrary", "arbitrary", "arbitrary")),
        name="inproj",
    )(x, mod, mod, w_bf16)
    return proj, h_last[0]


def _mixer_kernel(proj_ref, pprev_ref, conv0_ref, s0_ref, lru0_ref,
                  mu_ref, w0_ref, ww2_ref, a0_ref, wa2_ref, wg2_ref, kk_ref, ka_ref, rk_ref,
                  lnw_ref, lnb_ref, seg_ref, segt_ref,
                  cw_ref, cb_ref, wga_ref, bga_ref, wgi_ref, bgi_ref, lam_ref,
                  y_ref, sout_ref, lruo_ref, convo_ref,
                  prev_ref, hst_ref, st_ref,
                  bp_ref, rp_ref, x2_ref, y2_ref, v_ref, kb_ref, ab_ref, wt_ref, yacc_ref,
                  *, ns, tb, chunk, t_valid, carry_state):
    R = ns * tb
    t = pl.program_id(1)
    zero_blk = jnp.zeros((HEAD, HEAD), F32)

    def pair_tile(ref, s, pr):
        top = jnp.concatenate([ref[s, 2 * pr], zero_blk], axis=1)
        bot = jnp.concatenate([zero_blk, ref[s, 2 * pr + 1]], axis=1)
        return jnp.concatenate([top, bot], axis=0)

    @pl.when(t == 0)
    def _():
        prev_ref[...] = jnp.zeros_like(prev_ref)
        prev_ref[:, SUBLANES - 1:, :C_A] = pprev_ref[...]
        prev_ref[:, SUBLANES - (CONV_W - 1):, C_A:] = conv0_ref[...]
        hst_ref[...] = jnp.broadcast_to(lru0_ref[...], hst_ref.shape)

        if carry_state:
            for pr in range(N_PAIRS):
                st_ref[0, pr] = pair_tile(s0_ref, 0, pr)

    cur = proj_ref[...]
    ext = jnp.concatenate([prev_ref[...], cur[:, :, :C_A + D_B]], axis=1)
    prev_ref[...] = cur[:, tb - SUBLANES:, :C_A + D_B]

    row = _iota((R, 1), 0)
    rowin = row % tb
    valid = rowin < t_valid

    pa = cur[:, :, :C_A].reshape(R, C_A)
    shifted = ext[:, SUBLANES - 1:SUBLANES - 1 + tb, :C_A].reshape(R, C_A)
    p = pa + (shifted - pa) * mu_ref[...]
    r = p[:, :D_A]
    k = p[:, D_A:2 * D_A]
    v = p[:, 2 * D_A:3 * D_A]
    xwa = p[:, 3 * D_A:3 * D_A + LANES]
    xg = p[:, 3 * D_A + LANES:]
    w_log = -_softplus(-(w0_ref[...] + _dot3(jnp.tanh(xwa), ww2_ref[...]))) - 0.5
    logw = -jnp.exp(w_log)
    a = _sigmoid(a0_ref[...] + _dot3(xwa, wa2_ref[...]))
    g = _dot1(_sigmoid(xg), wg2_ref[...])
    seg = seg_ref[...]
    segt = segt_ref[...]

    def seg_sum(x):
        return _dot_exact_rhs(_dot_exact_rhs(x, seg), segt)

    kk = k * kk_ref[...]
    kk = kk / jnp.maximum(jnp.sqrt(seg_sum(kk * kk)), 1e-12)
    k = k * (1.0 + (a - 1.0) * ka_ref[...])
    kka = kk * a
    bonus = seg_sum(r * k * rk_ref[...]) * v
    logw = jnp.where(valid, logw, 0.0)
    r = jnp.where(valid, r, 0.0)
    k = jnp.where(valid, k, 0.0)
    v = jnp.where(valid, v, 0.0)
    kk = jnp.where(valid, kk, 0.0)
    kka = jnp.where(valid, kka, 0.0)

    ri = _iota((R, R), 0)
    ci = _iota((R, R), 1)
    same = (ri // chunk) == (ci // chunk)
    incl = same & (ci <= ri)
    strict = same & (ci < ri)
    lmat = jnp.concatenate([jnp.where(incl, 1.0, 0.0), jnp.where(same, 1.0, 0.0)], axis=0).astype(BF16)
    cums = _dot_exact_lhs(lmat, logw)
    cum = cums[:R]
    tot = cums[R:]
    e_prev = jnp.exp(cum - logw)
    e_neg = jnp.exp(-cum)
    e_pos = jnp.exp(cum)
    e_rem = jnp.exp(tot - cum)
    beta = kk * e_prev
    alpha = kka * e_neg
    kappa = k * e_neg
    rho = r * e_pos
    v_ref[...] = v
    kb_ref[...] = k * e_rem
    ab_ref[...] = -(kka * e_rem)
    wt_ref[...] = jnp.exp(tot)

    lane = _iota((1, LANES), 1)
    masks = (lane < HEAD, lane >= HEAD)
    eye = jnp.where(ri == ci, 1.0, 0.0)
    nsq = {16: 3, 8: 2}[chunk]
    heads = [(pr, m) for pr in range(N_PAIRS) for m in masks]
    psl = lambda pr: slice(pr * LANES, (pr + 1) * LANES)
    bms = [jnp.where(m, beta[:, psl(pr)], 0.0) for pr, m in heads]
    rms = [jnp.where(m, rho[:, psl(pr)], 0.0) for pr, m in heads]
    gms = [_dot1(jnp.concatenate([bm, rm], axis=0),
                 jnp.concatenate([alpha[:, psl(pr)], kappa[:, psl(pr)]], axis=0), _NT)
           for (pr, m), bm, rm in zip(heads, bms, rms)]
    l_bas = [jnp.where(strict, gm[:R, :R], 0.0) for gm in gms]
    m_ras = [jnp.where(incl, gm[R:, :R], 0.0) for gm in gms]
    xy1s = [_dot1(jnp.concatenate([jnp.where(strict, gm[:R, R:], 0.0), jnp.where(incl, gm[R:, R:], 0.0)], axis=0),
                  v[:, psl(pr)])
            for (pr, m), gm in zip(heads, gms)]
    pws = [_dot1(l, l) for l in l_bas]
    minvs = [eye - l for l in l_bas]
    for i in range(nsq):
        minvs = [mi + _dot1(mi, pw) for mi, pw in zip(minvs, pws)]
        if i + 1 < nsq:
            pws = [_dot1(pw, pw) for pw in pws]
    t1s = [_dot1(mi, jnp.concatenate([xy1[:R], bm], axis=1))
           for mi, xy1, bm in zip(minvs, xy1s, bms)]
    t2s = [_dot1(m_ra, t1) for m_ra, t1 in zip(m_ras, t1s)]
    for pr in range(N_PAIRS):
        h0, h1 = 2 * pr, 2 * pr + 1
        sl = psl(pr)
        m1 = masks[1]
        bp_ref[:, sl] = t1s[h0][:, LANES:] + t1s[h1][:, LANES:]
        rp_ref[:, sl] = (rms[h0] - t2s[h0][:, LANES:]) + (rms[h1] - t2s[h1][:, LANES:])
        x2_ref[:, sl] = jnp.where(m1, t1s[h1][:, :LANES], t1s[h0][:, :LANES])
        y2_ref[:, sl] = jnp.where(m1, xy1s[h1][R:] - t2s[h1][:, :LANES], xy1s[h0][R:] - t2s[h0][:, :LANES])

    bi = _iota((LANES, LANES), 0)
    bj = _iota((LANES, LANES), 1)
    bd = (bi < HEAD) == (bj < HEAD)
    nck = tb // chunk
    sites = [(s, c, pr) for s in range(ns) for c in range(nck) for pr in range(N_PAIRS)]
    rows_of = lambda s, c: slice(s * tb + c * chunk, s * tb + (c + 1) * chunk)
    thetas = {}
    psis = {}
    for s, c, pr in sites:
        rs, sl = rows_of(s, c), psl(pr)
        thetas[s, c, pr] = jnp.where(bd, _state_dot(bp_ref[rs, sl], ab_ref[rs, sl], _TN), 0.0)
        vx = jnp.concatenate([v_ref[rs, sl], x2_ref[rs, sl]], axis=0)
        ka = jnp.concatenate([kb_ref[rs, sl], ab_ref[rs, sl]], axis=0)
        psis[s, c, pr] = jnp.where(bd, _state_dot(vx, ka, _TN), 0.0)
    for s in range(ns):
        sps = [st_ref[s, pr] if carry_state else pair_tile(s0_ref, s, pr) for pr in range(N_PAIRS)]
        for c in range(nck):
            rs = rows_of(s, c)
            for pr in range(N_PAIRS):
                yacc_ref[rs, psl(pr)] = _state_dot(rp_ref[rs, psl(pr)], sps[pr], _NT) + y2_ref[rs, psl(pr)]
            sps = [sps[pr] * wt_ref[rs.start:rs.start + 1, psl(pr)]
                   + (_state_dot(sps[pr], thetas[s, c, pr]) + psis[s, c, pr]) for pr in range(N_PAIRS)]
        for pr in range(N_PAIRS):
            if carry_state:
                st_ref[s, pr] = sps[pr]
            else:
                sout_ref[s, 2 * pr] = sps[pr][:HEAD, :HEAD]
                sout_ref[s, 2 * pr + 1] = sps[pr][HEAD:, HEAD:]

    if carry_state:
        @pl.when(t == pl.num_programs(1) - 1)
        def _():
            for pr in range(N_PAIRS):
                tile = st_ref[0, pr]
                sout_ref[0, 2 * pr] = tile[:HEAD, :HEAD]
                sout_ref[0, 2 * pr + 1] = tile[HEAD:, HEAD:]

    y = yacc_ref[...]
    mean = seg_sum(y) * (1.0 / HEAD)
    yc = y - mean
    var = seg_sum(yc * yc) * (1.0 / HEAD)
    yn = yc * lax.rsqrt(var + GN_EPS) * lnw_ref[...] + lnb_ref[...]
    ya = (yn + bonus) * g
    y_ref[:, :, :D_A] = ya.reshape(ns, tb, D_A)

    gb = cur[:, :, C_A + D_B:].reshape(R, D_B)
    xc = cb_ref[...]
    for j in range(CONV_W):
        off = SUBLANES - (CONV_W - 1) + j
        xc = xc + ext[:, off:off + tb, C_A:].reshape(R, D_B) * cw_ref[j:j + 1, :]
    gr = []
    gi = []
    for pr in range(N_PAIRS):
        sl = slice(pr * LANES, (pr + 1) * LANES)
        gr.append(_dot1(xc[:, sl], wga_ref[pr]))
        gi.append(_dot1(xc[:, sl], wgi_ref[pr]))
    rg = _sigmoid(jnp.concatenate(gr, axis=1) + bga_ref[...])
    ig = _sigmoid(jnp.concatenate(gi, axis=1) + bgi_ref[...])
    log_a = -C_LRU * rg * _softplus(-lam_ref[...])
    av = jnp.exp(log_a)
    bv = jnp.sqrt(1.0 - jnp.exp(2.0 * log_a)) * (ig * xc)
    av = jnp.where(valid, av, 1.0)
    bv = jnp.where(valid, bv, 0.0)
    d = 1
    while d < tb:
        take = rowin >= d
        a_sh = jnp.where(take, pltpu.roll(av, d, axis=0), 1.0)
        b_sh = jnp.where(take, pltpu.roll(bv, d, axis=0), 0.0)
        bv = av * b_sh + bv
        av = av * a_sh
        d *= 2
    h0 = jnp.broadcast_to(hst_ref[:, SUBLANES - 1:SUBLANES, :], (ns, tb, D_B)).reshape(R, D_B)
    h = av * h0 + bv
    h3 = h.reshape(ns, tb, D_B)
    hst_ref[...] = h3[:, tb - SUBLANES:, :]
    y_ref[:, :, D_A:] = (h * _gelu(gb)).reshape(ns, tb, D_B)

    @pl.when(t == pl.num_programs(1) - 1)
    def _():
        lruo_ref[...] = h3[:, tb - SUBLANES:, :]
        convo_ref[...] = cur[:, tb - SUBLANES:, C_A:C_A + D_B]


def _mixer(proj, p_prev, conv0, s0, lru0, prm, ns, tb, chunk, t_valid):
    S, T, _ = proj.shape
    R = ns * tb
    assert R == ROWS
    carry_state = T // tb > 1
    assert ns == 1 or not carry_state
    kern = functools.partial(_mixer_kernel, ns=ns, tb=tb, chunk=chunk, t_valid=t_valid, carry_state=carry_state)

    def full(a):
        nd = a.ndim
        return pl.BlockSpec(a.shape, lambda s, t, nd=nd: (0,) * nd)

    params = [prm[n] for n in ('mu', 'w0', 'ww2', 'a0', 'wa2', 'wg2', 'k_k', 'k_a', 'r_k', 'lnx_w', 'lnx_b',
                               'seg', 'segt', 'conv_w', 'conv_b', 'wga', 'bga', 'wgi', 'bgi', 'lam')]
    slab = lambda w: pltpu.VMEM((R, w), F32)
    return pl.pallas_call(
        kern,
        out_shape=(jax.ShapeDtypeStruct((S, T, D_A + D_B), F32),
                   jax.ShapeDtypeStruct((S, N_HEADS, HEAD, HEAD), F32),
                   jax.ShapeDtypeStruct((S, SUBLANES, D_B), F32),
                   jax.ShapeDtypeStruct((S, SUBLANES, D_B), F32)),
        grid=(S // ns, T // tb),
        in_specs=[pl.BlockSpec((ns, tb, C_TOT), lambda s, t: (s, t, 0)),
                  pl.BlockSpec((ns, 1, C_A), lambda s, t: (s, 0, 0)),
                  pl.BlockSpec((ns, CONV_W - 1, D_B), lambda s, t: (s, 0, 0)),
                  pl.BlockSpec((ns, N_HEADS, HEAD, HEAD), lambda s, t: (s, 0, 0, 0),
                               pipeline_mode=pl.Buffered(1)),
                  pl.BlockSpec((ns, 1, D_B), lambda s, t: (s, 0, 0))]
                 + [full(a) for a in params],
        out_specs=(pl.BlockSpec((ns, tb, D_A + D_B), lambda s, t: (s, t, 0)),
                   pl.BlockSpec((ns, N_HEADS, HEAD, HEAD), lambda s, t: (s, 0, 0, 0),
                                pipeline_mode=pl.Buffered(1)),
                   pl.BlockSpec((ns, SUBLANES, D_B), lambda s, t: (s, 0, 0)),
                   pl.BlockSpec((ns, SUBLANES, D_B), lambda s, t: (s, 0, 0))),
        scratch_shapes=[pltpu.VMEM((ns, SUBLANES, C_A + D_B), F32),
                        pltpu.VMEM((ns, SUBLANES, D_B), F32),
                        pltpu.VMEM((1, N_PAIRS, LANES, LANES), F32)]
                       + [slab(D_A) for _ in range(9)],
        compiler_params=_cparams(("arbitrary", "arbitrary")),
        name="mixer",
    )(proj, p_prev, conv0, s0, lru0, *params)


def _layernorm(x, w, b):
    mean = jnp.mean(x, axis=-1, keepdims=True)
    xc = x - mean
    var = jnp.mean(xc * xc, axis=-1, keepdims=True)
    return xc * lax.rsqrt(var + LN_EPS) * w + b


def _outproj_kernel(y_ref, x_ref, g1_ref, sc2_ref, sh2_ref, w_ref, lnw_ref, lnb_ref, x1_ref, h2_ref, *, alpha):
    ns, tb, _ = y_ref.shape
    mix = _dot1(y_ref[...].reshape(ns * tb, D_MODEL), w_ref[...]).reshape(ns, tb, D_MODEL)
    x1 = _layernorm(alpha * x_ref[...] + (1.0 + g1_ref[...]) * mix, lnw_ref[...], lnb_ref[...])
    x1_ref[...] = x1
    h2 = x1 * (1.0 + sc2_ref[...]) + sh2_ref[...]
    h2_ref[...] = h2.reshape(ns * tb, D_MODEL).astype(BF16)


def _outproj(y, x, mod, w_bf16, ln_w, ln_b, ns, tb, alpha):
    S, T, _ = x.shape
    nt = T // tb
    blk = pl.BlockSpec((ns, tb, D_MODEL), lambda s, t: (s, t, 0))
    modspec = lambda i: pl.BlockSpec((ns, 1, D_MODEL), lambda s, t, i=i: (s, 0, i))
    vec = pl.BlockSpec((1, D_MODEL), lambda s, t: (0, 0))
    return pl.pallas_call(
        functools.partial(_outproj_kernel, alpha=alpha),
        out_shape=(jax.ShapeDtypeStruct((S, T, D_MODEL), F32), jax.ShapeDtypeStruct((S * T, D_MODEL), BF16)),
        grid=(S // ns, nt),
        in_specs=[blk, blk, modspec(2), modspec(4), modspec(3),
                  pl.BlockSpec((D_MODEL, D_MODEL), lambda s, t: (0, 0)), vec, vec],
        out_specs=(blk, pl.BlockSpec((ns * tb, D_MODEL), lambda s, t: (s * nt + t, 0))),
        compiler_params=_cparams(("arbitrary", "arbitrary")),
        name="outproj",
    )(y, x, mod, mod, mod, w_bf16, ln_w.reshape(1, -1), ln_b.reshape(1, -1))


def _topk_rows(s, ridx, n_rows):
    out_i = _iota((TOPK, s.shape[1]), 0)
    vals = jnp.zeros((TOPK, s.shape[1]), F32)
    idxs = jnp.zeros((TOPK, s.shape[1]), F32)
    for it in range(TOPK):
        m = jnp.max(s, axis=0, keepdims=True)
        idx = jnp.min(jnp.where(s == m, ridx, float(n_rows)), axis=0, keepdims=True)
        vals = jnp.where(out_i == it, m, vals)
        idxs = jnp.where(out_i == it, idx, idxs)
        s = jnp.where(ridx == idx, -jnp.inf, s)
    return vals, idxs


def _route_kernel(h_ref, wq_ref, sk_ref, g_ref, gate_ref, e1_ref, e2_ref):
    R = ROWS
    q = _dot1(h_ref[...], wq_ref[...])
    sk0 = sk_ref[0]
    sk1 = sk_ref[1]
    gates, e1s, e2s = [], [], []
    key_rows = _iota((N_KEYS, R), 0).astype(F32)
    r16 = _iota((TOPK, R), 0).astype(F32)
    r8 = _iota((SUBLANES, R), 0).astype(F32)
    ea_rank = jnp.where(r8 < 3, 2.0, jnp.where(r8 < 5, 3.0, 4.0))
    eb_rank = jnp.where((r8 == 0) | (r8 == 3) | (r8 == 5), 2.0, jnp.where((r8 == 1) | (r8 == 4), 3.0, 4.0))
    flat = jnp.concatenate([r16, TOPK + r8, TOPK * r16, TOPK * r8 + 1.0, TOPK * ea_rank + eb_rank], axis=0)
    neg = -jnp.inf
    for hd in range(P_HEADS):
        base = hd * 2 * D_QH
        s1 = _dot1(sk0, q[:, base:base + D_QH], _NT)
        s2 = _dot1(sk1, q[:, base + D_QH:base + 2 * D_QH], _NT)
        v1, i1 = _topk_rows(s1, key_rows, N_KEYS)
        v2, i2 = _topk_rows(s2, key_rows, N_KEYS)
        ea = jnp.where(r8 < 3, v1[2:3], jnp.where(r8 < 5, v1[3:4], v1[4:5]))
        eb = jnp.where(eb_rank == 2.0, v2[2:3], jnp.where(eb_rank == 3.0, v2[3:4], v2[4:5]))
        cand = jnp.concatenate([
            v1[0:1] + v2,
            v1[1:2] + v2[:SUBLANES],
            jnp.where(r16 >= 2, v1 + v2[0:1], neg),
            jnp.where(r8 >= 2, v1[:SUBLANES] + v2[1:2], neg),
            jnp.where(r8 < 6, ea + eb, neg),
        ], axis=0)
        sv, ci = _topk_rows(cand, flat, TOPK * TOPK)
        ca = jnp.floor(ci * (1.0 / TOPK))
        cb = ci - TOPK * ca
        e1 = jnp.zeros((TOPK, R), F32)
        e2 = jnp.zeros((TOPK, R), F32)
        for a in range(TOPK):
            e1 = jnp.where(ca == float(a), i1[a:a + 1, :], e1)
            e2 = jnp.where(cb == float(a), i2[a:a + 1, :], e2)
        ex = jnp.exp(sv - jnp.max(sv, axis=0, keepdims=True))
        gates.append(ex / jnp.sum(ex, axis=0, keepdims=True))
        e1s.append(e1)
        e2s.append(e2)
    gate_ref[...] = jnp.concatenate(gates, axis=0).T
    e1_ref[...] = jnp.concatenate(e1s, axis=0).T
    e2_ref[...] = jnp.concatenate(e2s, axis=0).T
    key_i = _iota((N_KEYS, P_HEADS * TOPK), 0).astype(F32)

    def per_octet(o, carry):
        base = pl.multiple_of(o * SUBLANES, SUBLANES)
        tiles = []
        for j in range(SUBLANES):
            sel1 = key_i == e1_ref[pl.ds(base + j, 1), :]
            sel2 = key_i == e2_ref[pl.ds(base + j, 1), :]
            m1 = jnp.where(sel1, gate_ref[pl.ds(base + j, 1), :], 0.0)
            m2 = jnp.where(sel2, 1.0, 0.0)
            tiles.append(_dot1(m1, m2, _NT))
        by_key = jnp.swapaxes(jnp.stack(tiles, axis=0), 0, 1)
        for i1v in range(N_KEYS):
            g_ref[pl.ds(base, SUBLANES), i1v * N_KEYS:(i1v + 1) * N_KEYS] = by_key[i1v]
        return carry

    lax.fori_loop(0, R // SUBLANES, per_octet, 0)


def _route(h2, wq_bf16, sub_keys):
    n = h2.shape[0]
    assert n % ROWS == 0
    return pl.pallas_call(
        _route_kernel,
        out_shape=jax.ShapeDtypeStruct((n, N_EXPERTS), F32),
        grid=(n // ROWS,),
        in_specs=[pl.BlockSpec((ROWS, D_MODEL), lambda i: (i, 0)),
                  pl.BlockSpec((D_MODEL, D_MODEL), lambda i: (0, 0)),
                  pl.BlockSpec((2, N_KEYS, D_QH), lambda i: (0, 0, 0))],
        out_specs=pl.BlockSpec((ROWS, N_EXPERTS), lambda i: (i, 0)),
        scratch_shapes=[pltpu.VMEM((ROWS, P_HEADS * TOPK), F32) for _ in range(3)],
        compiler_params=_cparams(("arbitrary",)),
        name="route",
    )(h2, wq_bf16, sub_keys)


EXPERT_CHUNK = 512
EXPERT_ROWS_MAX = 1152


def _experts_kernel(h_ref, g_ref, u_ref, v_ref, o_ref):
    e = pl.program_id(1)
    act = _dot1(h_ref[...], u_ref[...], _NT)
    w = g_ref[...] * _gelu(act)
    upd = _dot1(w, v_ref[...])

    @pl.when(e == 0)
    def _():
        o_ref[...] = upd

    @pl.when(e > 0)
    def _():
        o_ref[...] += upd


def _token_block(n, cap):
    for tb in range(min(n, cap), 0, -1):
        if n % tb == 0 and tb % (2 * SUBLANES) == 0:
            return tb
    raise ValueError(n)


def _experts(h2, gmat, peer_u, peer_v):
    n = h2.shape[0]
    tb = _token_block(n, EXPERT_ROWS_MAX)
    ec = EXPERT_CHUNK
    return pl.pallas_call(
        _experts_kernel,
        out_shape=jax.ShapeDtypeStruct((n, D_MODEL), F32),
        grid=(n // tb, N_EXPERTS // ec),
        in_specs=[pl.BlockSpec((tb, D_MODEL), lambda i, e: (i, 0)),
                  pl.BlockSpec((tb, ec), lambda i, e: (i, e)),
                  pl.BlockSpec((ec, D_MODEL), lambda i, e: (e, 0)),
                  pl.BlockSpec((ec, D_MODEL), lambda i, e: (e, 0))],
        out_specs=pl.BlockSpec((tb, D_MODEL), lambda i, e: (i, 0), pipeline_mode=pl.Buffered(1)),
        compiler_params=_cparams(("arbitrary", "arbitrary")),
        name="experts",
    )(h2, gmat, peer_u, peer_v)


def _ln2_kernel(x1_ref, ff_ref, g2_ref, lnw_ref, lnb_ref, o_ref, *, alpha):
    o_ref[...] = _layernorm(alpha * x1_ref[...] + (1.0 + g2_ref[...]) * ff_ref[...], lnw_ref[...], lnb_ref[...])


def _ln2(x1, ff, mod, ln_w, ln_b, ns, tb, alpha):
    S, T, _ = x1.shape
    blk = pl.BlockSpec((ns, tb, D_MODEL), lambda s, t: (s, t, 0))
    vec = pl.BlockSpec((1, D_MODEL), lambda s, t: (0, 0))
    return pl.pallas_call(
        functools.partial(_ln2_kernel, alpha=alpha),
        out_shape=jax.ShapeDtypeStruct((S, T, D_MODEL), F32),
        grid=(S // ns, T // tb),
        in_specs=[blk, blk, pl.BlockSpec((ns, 1, D_MODEL), lambda s, t: (s, 0, 5)), vec, vec],
        out_specs=blk,
        compiler_params=_cparams(("arbitrary", "arbitrary")),
        name="ln2",
    )(x1, ff, mod, ln_w.reshape(1, -1), ln_b.reshape(1, -1))


def _pair_blockdiag(w):
    n = w.shape[0] // 2
    w = w.reshape(n, 2, HEAD, HEAD)
    z = jnp.zeros((n, HEAD, HEAD), w.dtype)
    top = jnp.concatenate([w[:, 0], z], axis=2)
    bot = jnp.concatenate([z, w[:, 1]], axis=2)
    return jnp.concatenate([top, bot], axis=1)


def _mix_group(x, mod, p_prev, wkv0, lru0, conv0, prm, *, ns_mix, tb_mix, chunk, t_valid, ns_big, tb_big, alpha):
    S, T, _ = x.shape
    proj, h_last = _inproj(x, mod, prm['w_in'], ns_big, tb_big)
    y, wkv, lru_o, conv_o = _mixer(proj, p_prev[:, None, :], conv0, wkv0, lru0[:, None, :], prm,
                                   ns_mix, tb_mix, chunk, t_valid)
    x1, h2 = _outproj(y, x, mod, prm['w_out'], prm['ln1_w'], prm['ln1_b'], ns_big, tb_big, alpha)
    tv = (t_valid - 1) % SUBLANES
    shift = h_last[:, tv]
    lru = lru_o[:, SUBLANES - 1]
    conv = conv_o[:, tv - (CONV_W - 2):tv + 1]
    return x1, h2, (shift, wkv, lru, conv)


def kernel(x_prompt, x_sample, c_prompt, c_sample, state_shift, state_wkv, state_lru, state_conv, w_ada, b_ada, w_in, mu_shift, w0, w_w2, a0, w_a2, w_g2, k_k, k_a, r_k, lnx_w, lnx_b, conv_w, conv_b, w_gate_a, b_gate_a, w_gate_i, b_gate_i, lru_lambda, w_out, ln1_w, ln1_b, w_q, sub_keys, peer_u, peer_v, ln2_w, ln2_b):
    depth = w_ada.shape[0]
    alpha = (2 * depth) ** 0.25
    bp, tp, _ = x_prompt.shape
    bs, ts, _ = x_sample.shape
    yp = x_prompt
    ys = jnp.pad(x_sample, ((0, 0), (0, SAMPLE_TPAD - ts), (0, 0)))
    head_of = jnp.arange(D_A) // HEAD
    seg = (head_of[:, None] == jnp.arange(LANES)[None, :]).astype(BF16)
    row = lambda a: a.reshape(1, -1)
    outs = [[] for _ in range(8)]
    for l in range(depth):
        zpad = jnp.zeros((LANES - R_W, D_A), F32)
        prm = {
            'w_in': w_in[l].astype(BF16), 'w_out': w_out[l].astype(BF16), 'w_q': w_q[l].astype(BF16),
            'sub_keys': sub_keys[l],
            'mu': row(mu_shift[l]), 'w0': row(w0[l]),
            'ww2': jnp.concatenate([w_w2[l], zpad], axis=0), 'a0': row(a0[l]),
            'wa2': jnp.concatenate([zpad, w_a2[l]], axis=0), 'wg2': w_g2[l].astype(BF16),
            'k_k': row(k_k[l]), 'k_a': row(k_a[l]), 'r_k': row(r_k[l]),
            'lnx_w': row(lnx_w[l]), 'lnx_b': row(lnx_b[l]), 'seg': seg, 'segt': seg.T,
            'conv_w': conv_w[l], 'conv_b': row(conv_b[l]),
            'wga': _pair_blockdiag(w_gate_a[l]).astype(BF16), 'bga': row(b_gate_a[l]),
            'wgi': _pair_blockdiag(w_gate_i[l]).astype(BF16), 'bgi': row(b_gate_i[l]),
            'lam': row(lru_lambda[l]),
            'ln1_w': ln1_w[l], 'ln1_b': ln1_b[l], 'ln2_w': ln2_w[l], 'ln2_b': ln2_b[l],
        }
        c_all = jnp.concatenate([c_prompt, c_sample], axis=0)
        mod = _ada(c_all, w_ada[l], b_ada[l])
        mod_p = mod[:bp, None, :]
        mod_s = mod[bp:, None, :]
        zmod = jnp.zeros((bs // SUBLANES, 1, 6 * D_MODEL), F32)
        pprev, _ = _inproj(state_shift[l].reshape(bs // SUBLANES, SUBLANES, D_MODEL), zmod, prm['w_in'], 2, SUBLANES)
        pprev = pprev.reshape(bs, C_TOT)[:, :C_A]
        x1p, h2p, st_p = _mix_group(
            yp, mod_p, jnp.zeros((bp, C_A), F32), jnp.zeros((bp, N_HEADS, HEAD, HEAD), F32),
            jnp.zeros((bp, D_B), F32), jnp.zeros((bp, CONV_W - 1, D_B), F32), prm,
            ns_mix=1, tb_mix=ROWS, chunk=16, t_valid=ROWS, ns_big=1, tb_big=min(512, tp), alpha=alpha)
        x1s, h2s, st_s = _mix_group(
            ys, mod_s, pprev, state_wkv[l], state_lru[l], state_conv[l], prm,
            ns_mix=ROWS // SAMPLE_TPAD, tb_mix=SAMPLE_TPAD, chunk=SAMPLE_TPAD, t_valid=ts,
            ns_big=min(64, bs), tb_big=SAMPLE_TPAD, alpha=alpha)
        h2s = h2s.reshape(bs, SAMPLE_TPAD, D_MODEL)[:, :ts].reshape(bs * ts, D_MODEL)
        h2 = jnp.concatenate([h2p, h2s], axis=0)
        gmat = _route(h2, prm['w_q'], prm['sub_keys'])
        ff = _experts(h2, gmat, peer_u[l], peer_v[l])
        yp = _ln2(x1p, ff[:bp * tp].reshape(bp, tp, D_MODEL), mod_p, ln2_w[l], ln2_b[l], 1, min(512, tp), alpha)
        ys_real = _ln2(x1s[:, :ts], ff[bp * tp:].reshape(bs, ts, D_MODEL), mod_s, ln2_w[l], ln2_b[l],
                       min(64, bs), ts, alpha)
        ys = jnp.pad(ys_real, ((0, 0), (0, SAMPLE_TPAD - ts), (0, 0)))
        for i, a in enumerate(st_p + st_s):
            outs[i].append(a)
    return (yp, ys[:, :ts]) + tuple(jnp.stack(o) for o in outs)
```

```python
import functools

import jax
import jax.numpy as jnp
from jax import lax
from jax.experimental import pallas as pl
from jax.experimental.pallas import tpu as pltpu

F32 = jnp.float32
BF16 = jnp.bfloat16

D_MODEL = 2048
D_A = 1024
D_B = 1024
HEAD = 64
N_HEADS = D_A // HEAD
N_PAIRS = N_HEADS // 2
R_W = 64
R_A = 64
R_G = 128
C_A = 3 * D_A + R_W + R_A + R_G
C_TOT = C_A + 2 * D_B
CONV_W = 4
C_LRU = 8.0
P_HEADS = 8
N_KEYS = 128
N_EXPERTS = N_KEYS * N_KEYS
TOPK = 16
D_QH = 128
LN_EPS = 1e-5
GN_EPS = 64e-5

LANES = 128
SUBLANES = 8
ROWS = 128
SAMPLE_TPAD = 8
VMEM_LIMIT = 56 * 1024 * 1024


def _cparams(sem):
    return pltpu.CompilerParams(dimension_semantics=sem, vmem_limit_bytes=VMEM_LIMIT)


def _split2(x):
    hi = x.astype(BF16)
    lo = (x - hi.astype(F32)).astype(BF16)
    return hi, lo


_NN = (((1,), (0,)), ((), ()))
_NT = (((1,), (1,)), ((), ()))
_TN = (((0,), (0,)), ((), ()))


def _dg(a, b, dims):
    return lax.dot_general(a, b, dims, preferred_element_type=F32)


def _dot1(a, b, dims=_NN):
    return _dg(a.astype(BF16), b.astype(BF16), dims)


def _dot3(a, b, dims=_NN):
    ah, al = _split2(a)
    bh, bl = _split2(b)
    return _dg(ah, bh, dims) + (_dg(ah, bl, dims) + _dg(al, bh, dims))


_state_dot = _dot1


def _dot_exact_rhs(a, b_exact, dims=_NN):
    a1 = a.astype(BF16)
    r1 = a - a1.astype(F32)
    a2 = r1.astype(BF16)
    a3 = (r1 - a2.astype(F32)).astype(BF16)
    return _dg(a1, b_exact, dims) + (_dg(a2, b_exact, dims) + _dg(a3, b_exact, dims))


def _dot_exact_lhs(a_exact, b, dims=_NN):
    b1 = b.astype(BF16)
    r1 = b - b1.astype(F32)
    b2 = r1.astype(BF16)
    b3 = (r1 - b2.astype(F32)).astype(BF16)
    return _dg(a_exact, b1, dims) + (_dg(a_exact, b2, dims) + _dg(a_exact, b3, dims))


def _sigmoid(x):
    return 1.0 / (1.0 + jnp.exp(-x))


def _softplus(x):
    return jnp.maximum(x, 0.0) + jnp.log(1.0 + jnp.exp(-jnp.abs(x)))


def _gelu(x):
    return 0.5 * x * (1.0 + jnp.tanh(0.7978845608028654 * (x + 0.044715 * (x * x * x))))


def _iota(shape, dim):
    return lax.broadcasted_iota(jnp.int32, shape, dim)


def _ada_kernel(c_ref, w_ref, b_ref, o_ref):
    c = c_ref[...]
    s = c * _sigmoid(c)
    o_ref[...] = _dot3(s, w_ref[...]) + b_ref[...]


def _ada(c, w_ada, b_ada):
    n = c.shape[0]
    tn = 512
    return pl.pallas_call(
        _ada_kernel,
        out_shape=jax.ShapeDtypeStruct((n, 6 * D_MODEL), F32),
        grid=(6 * D_MODEL // tn,),
        in_specs=[pl.BlockSpec((n, D_MODEL), lambda j: (0, 0)),
                  pl.BlockSpec((D_MODEL, tn), lambda j: (0, j)),
                  pl.BlockSpec((1, tn), lambda j: (0, j))],
        out_specs=pl.BlockSpec((n, tn), lambda j: (0, j)),
        compiler_params=_cparams(("arbitrary",)),
        name="ada",
    )(c, w_ada, b_ada.reshape(1, -1))


def _inproj_kernel(x_ref, sc_ref, sh_ref, w_ref, o_ref, h_ref):
    h = x_ref[...] * (1.0 + sc_ref[...]) + sh_ref[...]
    ns, tb, _ = h.shape
    h_ref[...] = h[:, tb - SUBLANES:, :]
    p = _dot1(h.reshape(ns * tb, D_MODEL), w_ref[...])
    o_ref[...] = p.reshape(ns, tb, p.shape[-1])


def _inproj(x, mod, w_bf16, ns, tb):
    S, T, _ = x.shape
    ncol = w_bf16.shape[1]
    tn = ncol // 3
    proj, h_last = pl.pallas_call(
        _inproj_kernel,
        out_shape=(jax.ShapeDtypeStruct((S, T, ncol), F32),
                   jax.ShapeDtypeStruct((3, S, SUBLANES, D_MODEL), F32)),
        grid=(3, S // ns, T // tb),
        in_specs=[pl.BlockSpec((ns, tb, D_MODEL), lambda j, s, t: (s, t, 0)),
                  pl.BlockSpec((ns, 1, D_MODEL), lambda j, s, t: (s, 0, 1)),
                  pl.BlockSpec((ns, 1, D_MODEL), lambda j, s, t: (s, 0, 0)),
                  pl.BlockSpec((D_MODEL, tn), lambda j, s, t: (0, j))],
        out_specs=(pl.BlockSpec((ns, tb, tn), lambda j, s, t: (s, t, j)),
                   pl.BlockSpec((None, ns, SUBLANES, D_MODEL), lambda j, s, t: (j, s, 0, 0))),
        compiler_params=_cparams(("arbitrary", "arbitrary", "arbitrary")),
        name="inproj",
    )(x, mod, mod, w_bf16)
    return proj, h_last[0]


def _mixer_kernel(proj_ref, pprev_ref, conv0_ref, s0_ref, lru0_ref,
                  mu_ref, w0_ref, ww2_ref, a0_ref, wa2_ref, wg2_ref, kk_ref, ka_ref, rk_ref,
                  lnw_ref, lnb_ref, seg_ref, segt_ref,
                  cw_ref, cb_ref, wga_ref, bga_ref, wgi_ref, bgi_ref, lam_ref,
                  y_ref, sout_ref, lruo_ref, convo_ref,
                  prev_ref, hst_ref, st_ref,
                  bp_ref, rp_ref, x2_ref, y2_ref, v_ref, kb_ref, ab_ref, wt_ref, yacc_ref,
                  *, ns, tb, chunk, t_valid, carry_state):
    R = ns * tb
    t = pl.program_id(1)
    zero_blk = jnp.zeros((HEAD, HEAD), F32)

    def pair_tile(ref, s, pr):
        top = jnp.concatenate([ref[s, 2 * pr], zero_blk], axis=1)
        bot = jnp.concatenate([zero_blk, ref[s, 2 * pr + 1]], axis=1)
        return jnp.concatenate([top, bot], axis=0)

    @pl.when(t == 0)
    def _():
        prev_ref[...] = jnp.zeros_like(prev_ref)
        prev_ref[:, SUBLANES - 1:, :C_A] = pprev_ref[...]
        prev_ref[:, SUBLANES - (CONV_W - 1):, C_A:] = conv0_ref[...]
        hst_ref[...] = jnp.broadcast_to(lru0_ref[...], hst_ref.shape)

        if carry_state:
            for pr in range(N_PAIRS):
                st_ref[0, pr] = pair_tile(s0_ref, 0, pr)

    cur = proj_ref[...]
    ext = jnp.concatenate([prev_ref[...], cur[:, :, :C_A + D_B]], axis=1)
    prev_ref[...] = cur[:, tb - SUBLANES:, :C_A + D_B]

    row = _iota((R, 1), 0)
    rowin = row % tb
    valid = rowin < t_valid

    pa = cur[:, :, :C_A].reshape(R, C_A)
    shifted = ext[:, SUBLANES - 1:SUBLANES - 1 + tb, :C_A].reshape(R, C_A)
    p = pa + (shifted - pa) * mu_ref[...]
    r = p[:, :D_A]
    k = p[:, D_A:2 * D_A]
    v = p[:, 2 * D_A:3 * D_A]
    xwa = p[:, 3 * D_A:3 * D_A + LANES]
    xg = p[:, 3 * D_A + LANES:]
    w_log = -_softplus(-(w0_ref[...] + _dot3(jnp.tanh(xwa), ww2_ref[...]))) - 0.5
    logw = -jnp.exp(w_log)
    a = _sigmoid(a0_ref[...] + _dot3(xwa, wa2_ref[...]))
    g = _dot1(_sigmoid(xg), wg2_ref[...])
    seg = seg_ref[...]
    segt = segt_ref[...]

    def seg_sum(x):
        return _dot_exact_rhs(_dot_exact_rhs(x, seg), segt)

    kk = k * kk_ref[...]
    kk = kk / jnp.maximum(jnp.sqrt(seg_sum(kk * kk)), 1e-12)
    k = k * (1.0 + (a - 1.0) * ka_ref[...])
    kka = kk * a
    bonus = seg_sum(r * k * rk_ref[...]) * v
    logw = jnp.where(valid, logw, 0.0)
    r = jnp.where(valid, r, 0.0)
    k = jnp.where(valid, k, 0.0)
    v = jnp.where(valid, v, 0.0)
    kk = jnp.where(valid, kk, 0.0)
    kka = jnp.where(valid, kka, 0.0)

    ri = _iota((R, R), 0)
    ci = _iota((R, R), 1)
    same = (ri // chunk) == (ci // chunk)
    incl = same & (ci <= ri)
    strict = same & (ci < ri)
    lmat = jnp.concatenate([jnp.where(incl, 1.0, 0.0), jnp.where(same, 1.0, 0.0)], axis=0).astype(BF16)
    cums = _dot_exact_lhs(lmat, logw)
    cum = cums[:R]
    tot = cums[R:]
    e_prev = jnp.exp(cum - logw)
    e_neg = jnp.exp(-cum)
    e_pos = jnp.exp(cum)
    e_rem = jnp.exp(tot - cum)
    beta = kk * e_prev
    alpha = kka * e_neg
    kappa = k * e_neg
    rho = r * e_pos
    v_ref[...] = v
    kb_ref[...] = k * e_rem
    ab_ref[...] = -(kka * e_rem)
    wt_ref[...] = jnp.exp(tot)

    lane = _iota((1, LANES), 1)
    masks = (lane < HEAD, lane >= HEAD)
    eye = jnp.where(ri == ci, 1.0, 0.0)
    nsq = {16: 3, 8: 2}[chunk]
    heads = [(pr, m) for pr in range(N_PAIRS) for m in masks]
    psl = lambda pr: slice(pr * LANES, (pr + 1) * LANES)
    bms = [jnp.where(m, beta[:, psl(pr)], 0.0) for pr, m in heads]
    rms = [jnp.where(m, rho[:, psl(pr)], 0.0) for pr, m in heads]
    gms = [_dot1(jnp.concatenate([bm, rm], axis=0),
                 jnp.concatenate([alpha[:, psl(pr)], kappa[:, psl(pr)]], axis=0), _NT)
           for (pr, m), bm, rm in zip(heads, bms, rms)]
    l_bas = [jnp.where(strict, gm[:R, :R], 0.0) for gm in gms]
    m_ras = [jnp.where(incl, gm[R:, :R], 0.0) for gm in gms]
    xy1s = [_dot1(jnp.concatenate([jnp.where(strict, gm[:R, R:], 0.0), jnp.where(incl, gm[R:, R:], 0.0)], axis=0),
                  v[:, psl(pr)])
            for (pr, m), gm in zip(heads, gms)]
    pws = [_dot1(l, l) for l in l_bas]
    minvs = [eye - l for l in l_bas]
    for i in range(nsq):
        minvs = [mi + _dot1(mi, pw) for mi, pw in zip(minvs, pws)]
        if i + 1 < nsq:
            pws = [_dot1(pw, pw) for pw in pws]
    t1s = [_dot1(mi, jnp.concatenate([xy1[:R], bm], axis=1))
           for mi, xy1, bm in zip(minvs, xy1s, bms)]
    t2s = [_dot1(m_ra, t1) for m_ra, t1 in zip(m_ras, t1s)]
    for pr in range(N_PAIRS):
        h0, h1 = 2 * pr, 2 * pr + 1
        sl = psl(pr)
        m1 = masks[1]
        bp_ref[:, sl] = t1s[h0][:, LANES:] + t1s[h1][:, LANES:]
        rp_ref[:, sl] = (rms[h0] - t2s[h0][:, LANES:]) + (rms[h1] - t2s[h1][:, LANES:])
        x2_ref[:, sl] = jnp.where(m1, t1s[h1][:, :LANES], t1s[h0][:, :LANES])
        y2_ref[:, sl] = jnp.where(m1, xy1s[h1][R:] - t2s[h1][:, :LANES], xy1s[h0][R:] - t2s[h0][:, :LANES])

    bi = _iota((LANES, LANES), 0)
    bj = _iota((LANES, LANES), 1)
    bd = (bi < HEAD) == (bj < HEAD)
    nck = tb // chunk
    sites = [(s, c, pr) for s in range(ns) for c in range(nck) for pr in range(N_PAIRS)]
    rows_of = lambda s, c: slice(s * tb + c * chunk, s * tb + (c + 1) * chunk)
    thetas = {}
    psis = {}
    for s, c, pr in sites:
        rs, sl = rows_of(s, c), psl(pr)
        thetas[s, c, pr] = jnp.where(bd, _state_dot(bp_ref[rs, sl], ab_ref[rs, sl], _TN), 0.0)
        vx = jnp.concatenate([v_ref[rs, sl], x2_ref[rs, sl]], axis=0)
        ka = jnp.concatenate([kb_ref[rs, sl], ab_ref[rs, sl]], axis=0)
        psis[s, c, pr] = jnp.where(bd, _state_dot(vx, ka, _TN), 0.0)
    for s in range(ns):
        sps = [st_ref[s, pr] if carry_state else pair_tile(s0_ref, s, pr) for pr in range(N_PAIRS)]
        for c in range(nck):
            rs = rows_of(s, c)
            for pr in range(N_PAIRS):
                yacc_ref[rs, psl(pr)] = _state_dot(rp_ref[rs, psl(pr)], sps[pr], _NT) + y2_ref[rs, psl(pr)]
            sps = [sps[pr] * wt_ref[rs.start:rs.start + 1, psl(pr)]
                   + (_state_dot(sps[pr], thetas[s, c, pr]) + psis[s, c, pr]) for pr in range(N_PAIRS)]
        for pr in range(N_PAIRS):
            if carry_state:
                st_ref[s, pr] = sps[pr]
            else:
                sout_ref[s, 2 * pr] = sps[pr][:HEAD, :HEAD]
                sout_ref[s, 2 * pr + 1] = sps[pr][HEAD:, HEAD:]

    if carry_state:
        @pl.when(t == pl.num_programs(1) - 1)
        def _():
            for pr in range(N_PAIRS):
                tile = st_ref[0, pr]
                sout_ref[0, 2 * pr] = tile[:HEAD, :HEAD]
                sout_ref[0, 2 * pr + 1] = tile[HEAD:, HEAD:]

    y = yacc_ref[...]
    mean = seg_sum(y) * (1.0 / HEAD)
    yc = y - mean
    var = seg_sum(yc * yc) * (1.0 / HEAD)
    yn = yc * lax.rsqrt(var + GN_EPS) * lnw_ref[...] + lnb_ref[...]
    ya = (yn + bonus) * g
    y_ref[:, :, :D_A] = ya.reshape(ns, tb, D_A)

    gb = cur[:, :, C_A + D_B:].reshape(R, D_B)
    xc = cb_ref[...]
    for j in range(CONV_W):
        off = SUBLANES - (CONV_W - 1) + j
        xc = xc + ext[:, off:off + tb, C_A:].reshape(R, D_B) * cw_ref[j:j + 1, :]
    gr = []
    gi = []
    for pr in range(N_PAIRS):
        sl = slice(pr * LANES, (pr + 1) * LANES)
        gr.append(_dot1(xc[:, sl], wga_ref[pr]))
        gi.append(_dot1(xc[:, sl], wgi_ref[pr]))
    rg = _sigmoid(jnp.concatenate(gr, axis=1) + bga_ref[...])
    ig = _sigmoid(jnp.concatenate(gi, axis=1) + bgi_ref[...])
    log_a = -C_LRU * rg * _softplus(-lam_ref[...])
    av = jnp.exp(log_a)
    bv = jnp.sqrt(1.0 - jnp.exp(2.0 * log_a)) * (ig * xc)
    av = jnp.where(valid, av, 1.0)
    bv = jnp.where(valid, bv, 0.0)
    d = 1
    while d < tb:
        take = rowin >= d
        a_sh = jnp.where(take, pltpu.roll(av, d, axis=0), 1.0)
        b_sh = jnp.where(take, pltpu.roll(bv, d, axis=0), 0.0)
        bv = av * b_sh + bv
        av = av * a_sh
        d *= 2
    h0 = jnp.broadcast_to(hst_ref[:, SUBLANES - 1:SUBLANES, :], (ns, tb, D_B)).reshape(R, D_B)
    h = av * h0 + bv
    h3 = h.reshape(ns, tb, D_B)
    hst_ref[...] = h3[:, tb - SUBLANES:, :]
    y_ref[:, :, D_A:] = (h * _gelu(gb)).reshape(ns, tb, D_B)

    @pl.when(t == pl.num_programs(1) - 1)
    def _():
        lruo_ref[...] = h3[:, tb - SUBLANES:, :]
        convo_ref[...] = cur[:, tb - SUBLANES:, C_A:C_A + D_B]


def _mixer(proj, p_prev, conv0, s0, lru0, prm, ns, tb, chunk, t_valid):
    S, T, _ = proj.shape
    R = ns * tb
    assert R == ROWS
    carry_state = T // tb > 1
    assert ns == 1 or not carry_state
    kern = functools.partial(_mixer_kernel, ns=ns, tb=tb, chunk=chunk, t_valid=t_valid, carry_state=carry_state)

    def full(a):
        nd = a.ndim
        return pl.BlockSpec(a.shape, lambda s, t, nd=nd: (0,) * nd)

    params = [prm[n] for n in ('mu', 'w0', 'ww2', 'a0', 'wa2', 'wg2', 'k_k', 'k_a', 'r_k', 'lnx_w', 'lnx_b',
                               'seg', 'segt', 'conv_w', 'conv_b', 'wga', 'bga', 'wgi', 'bgi', 'lam')]
    slab = lambda w: pltpu.VMEM((R, w), F32)
    return pl.pallas_call(
        kern,
        out_shape=(jax.ShapeDtypeStruct((S, T, D_A + D_B), F32),
                   jax.ShapeDtypeStruct((S, N_HEADS, HEAD, HEAD), F32),
                   jax.ShapeDtypeStruct((S, SUBLANES, D_B), F32),
                   jax.ShapeDtypeStruct((S, SUBLANES, D_B), F32)),
        grid=(S // ns, T // tb),
        in_specs=[pl.BlockSpec((ns, tb, C_TOT), lambda s, t: (s, t, 0)),
                  pl.BlockSpec((ns, 1, C_A), lambda s, t: (s, 0, 0)),
                  pl.BlockSpec((ns, CONV_W - 1, D_B), lambda s, t: (s, 0, 0)),
                  pl.BlockSpec((ns, N_HEADS, HEAD, HEAD), lambda s, t: (s, 0, 0, 0),
                               pipeline_mode=pl.Buffered(1)),
                  pl.BlockSpec((ns, 1, D_B), lambda s, t: (s, 0, 0))]
                 + [full(a) for a in params],
        out_specs=(pl.BlockSpec((ns, tb, D_A + D_B), lambda s, t: (s, t, 0)),
                   pl.BlockSpec((ns, N_HEADS, HEAD, HEAD), lambda s, t: (s, 0, 0, 0),
                                pipeline_mode=pl.Buffered(1)),
                   pl.BlockSpec((ns, SUBLANES, D_B), lambda s, t: (s, 0, 0)),
                   pl.BlockSpec((ns, SUBLANES, D_B), lambda s, t: (s, 0, 0))),
        scratch_shapes=[pltpu.VMEM((ns, SUBLANES, C_A + D_B), F32),
                        pltpu.VMEM((ns, SUBLANES, D_B), F32),
                        pltpu.VMEM((1, N_PAIRS, LANES, LANES), F32)]
                       + [slab(D_A) for _ in range(9)],
        compiler_params=_cparams(("arbitrary", "arbitrary")),
        name="mixer",
    )(proj, p_prev, conv0, s0, lru0, *params)


def _layernorm(x, w, b):
    mean = jnp.mean(x, axis=-1, keepdims=True)
    xc = x - mean
    var = jnp.mean(xc * xc, axis=-1, keepdims=True)
    return xc * lax.rsqrt(var + LN_EPS) * w + b


def _outproj_kernel(y_ref, x_ref, g1_ref, sc2_ref, sh2_ref, w_ref, lnw_ref, lnb_ref, x1_ref, h2_ref, *, alpha):
    ns, tb, _ = y_ref.shape
    mix = _dot1(y_ref[...].reshape(ns * tb, D_MODEL), w_ref[...]).reshape(ns, tb, D_MODEL)
    x1 = _layernorm(alpha * x_ref[...] + (1.0 + g1_ref[...]) * mix, lnw_ref[...], lnb_ref[...])
    x1_ref[...] = x1
    h2 = x1 * (1.0 + sc2_ref[...]) + sh2_ref[...]
    h2_ref[...] = h2.reshape(ns * tb, D_MODEL).astype(BF16)


def _outproj(y, x, mod, w_bf16, ln_w, ln_b, ns, tb, alpha):
    S, T, _ = x.shape
    nt = T // tb
    blk = pl.BlockSpec((ns, tb, D_MODEL), lambda s, t: (s, t, 0))
    modspec = lambda i: pl.BlockSpec((ns, 1, D_MODEL), lambda s, t, i=i: (s, 0, i))
    vec = pl.BlockSpec((1, D_MODEL), lambda s, t: (0, 0))
    return pl.pallas_call(
        functools.partial(_outproj_kernel, alpha=alpha),
        out_shape=(jax.ShapeDtypeStruct((S, T, D_MODEL), F32), jax.ShapeDtypeStruct((S * T, D_MODEL), BF16)),
        grid=(S // ns, nt),
        in_specs=[blk, blk, modspec(2), modspec(4), modspec(3),
                  pl.BlockSpec((D_MODEL, D_MODEL), lambda s, t: (0, 0)), vec, vec],
        out_specs=(blk, pl.BlockSpec((ns * tb, D_MODEL), lambda s, t: (s * nt + t, 0))),
        compiler_params=_cparams(("arbitrary", "arbitrary")),
        name="outproj",
    )(y, x, mod, mod, mod, w_bf16, ln_w.reshape(1, -1), ln_b.reshape(1, -1))


def _topk_rows(s, ridx, n_rows):
    out_i = _iota((TOPK, s.shape[1]), 0)
    vals = jnp.zeros((TOPK, s.shape[1]), F32)
    idxs = jnp.zeros((TOPK, s.shape[1]), F32)
    for it in range(TOPK):
        m = jnp.max(s, axis=0, keepdims=True)
        idx = jnp.min(jnp.where(s == m, ridx, float(n_rows)), axis=0, keepdims=True)
        vals = jnp.where(out_i == it, m, vals)
        idxs = jnp.where(out_i == it, idx, idxs)
        s = jnp.where(ridx == idx, -jnp.inf, s)
    return vals, idxs


def _route_kernel(h_ref, wq_ref, sk_ref, g_ref, gate_ref, e1_ref, e2_ref):
    R = ROWS
    q = _dot1(h_ref[...], wq_ref[...])
    sk0 = sk_ref[0]
    sk1 = sk_ref[1]
    gates, e1s, e2s = [], [], []
    key_rows = _iota((N_KEYS, R), 0).astype(F32)
    r16 = _iota((TOPK, R), 0).astype(F32)
    r8 = _iota((SUBLANES, R), 0).astype(F32)
    ea_rank = jnp.where(r8 < 3, 2.0, jnp.where(r8 < 5, 3.0, 4.0))
    eb_rank = jnp.where((r8 == 0) | (r8 == 3) | (r8 == 5), 2.0, jnp.where((r8 == 1) | (r8 == 4), 3.0, 4.0))
    flat = jnp.concatenate([r16, TOPK + r8, TOPK * r16, TOPK * r8 + 1.0, TOPK * ea_rank + eb_rank], axis=0)
    neg = -jnp.inf
    for hd in range(P_HEADS):
        base = hd * 2 * D_QH
        s1 = _dot1(sk0, q[:, base:base + D_QH], _NT)
        s2 = _dot1(sk1, q[:, base + D_QH:base + 2 * D_QH], _NT)
        v1, i1 = _topk_rows(s1, key_rows, N_KEYS)
        v2, i2 = _topk_rows(s2, key_rows, N_KEYS)
        ea = jnp.where(r8 < 3, v1[2:3], jnp.where(r8 < 5, v1[3:4], v1[4:5]))
        eb = jnp.where(eb_rank == 2.0, v2[2:3], jnp.where(eb_rank == 3.0, v2[3:4], v2[4:5]))
        cand = jnp.concatenate([
            v1[0:1] + v2,
            v1[1:2] + v2[:SUBLANES],
            jnp.where(r16 >= 2, v1 + v2[0:1], neg),
            jnp.where(r8 >= 2, v1[:SUBLANES] + v2[1:2], neg),
            jnp.where(r8 < 6, ea + eb, neg),
        ], axis=0)
        sv, ci = _topk_rows(cand, flat, TOPK * TOPK)
        ca = jnp.floor(ci * (1.0 / TOPK))
        cb = ci - TOPK * ca
        e1 = jnp.zeros((TOPK, R), F32)
        e2 = jnp.zeros((TOPK, R), F32)
        for a in range(TOPK):
            e1 = jnp.where(ca == float(a), i1[a:a + 1, :], e1)
            e2 = jnp.where(cb == float(a), i2[a:a + 1, :], e2)
        ex = jnp.exp(sv - jnp.max(sv, axis=0, keepdims=True))
        gates.append(ex / jnp.sum(ex, axis=0, keepdims=True))
        e1s.append(e1)
        e2s.append(e2)
    gate_ref[...] = jnp.concatenate(gates, axis=0).T
    e1_ref[...] = jnp.concatenate(e1s, axis=0).T
    e2_ref[...] = jnp.concatenate(e2s, axis=0).T
    key_i = _iota((N_KEYS, P_HEADS * TOPK), 0).astype(F32)

    def per_octet(o, carry):
        base = pl.multiple_of(o * SUBLANES, SUBLANES)
        tiles = []
        for j in range(SUBLANES):
            sel1 = key_i == e1_ref[pl.ds(base + j, 1), :]
            sel2 = key_i == e2_ref[pl.ds(base + j, 1), :]
            m1 = jnp.where(sel1, gate_ref[pl.ds(base + j, 1), :], 0.0)
            m2 = jnp.where(sel2, 1.0, 0.0)
            tiles.append(_dot1(m1, m2, _NT))
        by_key = jnp.swapaxes(jnp.stack(tiles, axis=0), 0, 1)
        for i1v in range(N_KEYS):
            g_ref[pl.ds(base, SUBLANES), i1v * N_KEYS:(i1v + 1) * N_KEYS] = by_key[i1v]
        return carry

    lax.fori_loop(0, R // SUBLANES, per_octet, 0, unroll=8)


def _route(h2, wq_bf16, sub_keys):
    n = h2.shape[0]
    assert n % ROWS == 0
    return pl.pallas_call(
        _route_kernel,
        out_shape=jax.ShapeDtypeStruct((n, N_EXPERTS), F32),
        grid=(n // ROWS,),
        in_specs=[pl.BlockSpec((ROWS, D_MODEL), lambda i: (i, 0)),
                  pl.BlockSpec((D_MODEL, D_MODEL), lambda i: (0, 0)),
                  pl.BlockSpec((2, N_KEYS, D_QH), lambda i: (0, 0, 0))],
        out_specs=pl.BlockSpec((ROWS, N_EXPERTS), lambda i: (i, 0)),
        scratch_shapes=[pltpu.VMEM((ROWS, P_HEADS * TOPK), F32) for _ in range(3)],
        compiler_params=_cparams(("arbitrary",)),
        name="route",
    )(h2, wq_bf16, sub_keys)


EXPERT_CHUNK = 512
EXPERT_ROWS_MAX = 1152


EXPERT_SPLIT = 4


def _experts_kernel(h_ref, g_ref, u_ref, v_ref, o_ref):
    @pl.when(pl.program_id(1) == 0)
    def _():
        o_ref[...] = jnp.zeros_like(o_ref)

    u = u_ref[...].astype(BF16)
    v = v_ref[...].astype(BF16)
    rb = h_ref.shape[0] // EXPERT_SPLIT
    rows = [pl.ds(b * rb, rb) for b in range(EXPERT_SPLIT)]
    act = _dg(h_ref[rows[0], :], u, _NT)
    for b in range(EXPERT_SPLIT):
        nxt = _dg(h_ref[rows[b + 1], :], u, _NT) if b + 1 < EXPERT_SPLIT else None
        w = (g_ref[rows[b], :] * _gelu(act)).astype(BF16)
        o_ref[rows[b], :] += _dg(w, v, _NN)
        act = nxt


def _token_block(n, cap):
    for tb in range(min(n, cap), 0, -1):
        if n % tb == 0 and tb % (2 * SUBLANES * EXPERT_SPLIT) == 0:
            return tb
    raise ValueError(n)


def _experts(h2, gmat, peer_u, peer_v):
    n = h2.shape[0]
    tb = _token_block(n, EXPERT_ROWS_MAX)
    ec = EXPERT_CHUNK
    return pl.pallas_call(
        _experts_kernel,
        out_shape=jax.ShapeDtypeStruct((n, D_MODEL), F32),
        grid=(n // tb, N_EXPERTS // ec),
        in_specs=[pl.BlockSpec((tb, D_MODEL), lambda i, e: (i, 0)),
                  pl.BlockSpec((tb, ec), lambda i, e: (i, e)),
                  pl.BlockSpec((ec, D_MODEL), lambda i, e: (e, 0)),
                  pl.BlockSpec((ec, D_MODEL), lambda i, e: (e, 0))],
        out_specs=pl.BlockSpec((tb, D_MODEL), lambda i, e: (i, 0), pipeline_mode=pl.Buffered(1)),
        compiler_params=_cparams(("arbitrary", "arbitrary")),
        name="experts",
    )(h2, gmat, peer_u, peer_v)


def _ln2_kernel(x1_ref, ff_ref, g2_ref, lnw_ref, lnb_ref, o_ref, *, alpha):
    o_ref[...] = _layernorm(alpha * x1_ref[...] + (1.0 + g2_ref[...]) * ff_ref[...], lnw_ref[...], lnb_ref[...])


def _ln2(x1, ff, mod, ln_w, ln_b, ns, tb, alpha):
    S, T, _ = x1.shape
    blk = pl.BlockSpec((ns, tb, D_MODEL), lambda s, t: (s, t, 0))
    vec = pl.BlockSpec((1, D_MODEL), lambda s, t: (0, 0))
    return pl.pallas_call(
        functools.partial(_ln2_kernel, alpha=alpha),
        out_shape=jax.ShapeDtypeStruct((S, T, D_MODEL), F32),
        grid=(S // ns, T // tb),
        in_specs=[blk, blk, pl.BlockSpec((ns, 1, D_MODEL), lambda s, t: (s, 0, 5)), vec, vec],
        out_specs=blk,
        compiler_params=_cparams(("arbitrary", "arbitrary")),
        name="ln2",
    )(x1, ff, mod, ln_w.reshape(1, -1), ln_b.reshape(1, -1))


def _pair_blockdiag(w):
    n = w.shape[0] // 2
    w = w.reshape(n, 2, HEAD, HEAD)
    z = jnp.zeros((n, HEAD, HEAD), w.dtype)
    top = jnp.concatenate([w[:, 0], z], axis=2)
    bot = jnp.concatenate([z, w[:, 1]], axis=2)
    return jnp.concatenate([top, bot], axis=1)


def _mix_group(x, mod, p_prev, wkv0, lru0, conv0, prm, *, ns_mix, tb_mix, chunk, t_valid, ns_big, tb_big, alpha):
    S, T, _ = x.shape
    proj, h_last = _inproj(x, mod, prm['w_in'], ns_big, tb_big)
    y, wkv, lru_o, conv_o = _mixer(proj, p_prev[:, None, :], conv0, wkv0, lru0[:, None, :], prm,
                                   ns_mix, tb_mix, chunk, t_valid)
    x1, h2 = _outproj(y, x, mod, prm['w_out'], prm['ln1_w'], prm['ln1_b'], ns_big, tb_big, alpha)
    tv = (t_valid - 1) % SUBLANES
    shift = h_last[:, tv]
    lru = lru_o[:, SUBLANES - 1]
    conv = conv_o[:, tv - (CONV_W - 2):tv + 1]
    return x1, h2, (shift, wkv, lru, conv)


def kernel(x_prompt, x_sample, c_prompt, c_sample, state_shift, state_wkv, state_lru, state_conv, w_ada, b_ada, w_in, mu_shift, w0, w_w2, a0, w_a2, w_g2, k_k, k_a, r_k, lnx_w, lnx_b, conv_w, conv_b, w_gate_a, b_gate_a, w_gate_i, b_gate_i, lru_lambda, w_out, ln1_w, ln1_b, w_q, sub_keys, peer_u, peer_v, ln2_w, ln2_b):
    depth = w_ada.shape[0]
    alpha = (2 * depth) ** 0.25
    bp, tp, _ = x_prompt.shape
    bs, ts, _ = x_sample.shape
    yp = x_prompt
    ys = jnp.pad(x_sample, ((0, 0), (0, SAMPLE_TPAD - ts), (0, 0)))
    head_of = jnp.arange(D_A) // HEAD
    seg = (head_of[:, None] == jnp.arange(LANES)[None, :]).astype(BF16)
    row = lambda a: a.reshape(1, -1)
    outs = [[] for _ in range(8)]
    for l in range(depth):
        zpad = jnp.zeros((LANES - R_W, D_A), F32)
        prm = {
            'w_in': w_in[l].astype(BF16), 'w_out': w_out[l].astype(BF16), 'w_q': w_q[l].astype(BF16),
            'sub_keys': sub_keys[l],
            'mu': row(mu_shift[l]), 'w0': row(w0[l]),
            'ww2': jnp.concatenate([w_w2[l], zpad], axis=0), 'a0': row(a0[l]),
            'wa2': jnp.concatenate([zpad, w_a2[l]], axis=0), 'wg2': w_g2[l].astype(BF16),
            'k_k': row(k_k[l]), 'k_a': row(k_a[l]), 'r_k': row(r_k[l]),
            'lnx_w': row(lnx_w[l]), 'lnx_b': row(lnx_b[l]), 'seg': seg, 'segt': seg.T,
            'conv_w': conv_w[l], 'conv_b': row(conv_b[l]),
            'wga': _pair_blockdiag(w_gate_a[l]).astype(BF16), 'bga': row(b_gate_a[l]),
            'wgi': _pair_blockdiag(w_gate_i[l]).astype(BF16), 'bgi': row(b_gate_i[l]),
            'lam': row(lru_lambda[l]),
            'ln1_w': ln1_w[l], 'ln1_b': ln1_b[l], 'ln2_w': ln2_w[l], 'ln2_b': ln2_b[l],
        }
        c_all = jnp.concatenate([c_prompt, c_sample], axis=0)
        mod = _ada(c_all, w_ada[l], b_ada[l])
        mod_p = mod[:bp, None, :]
        mod_s = mod[bp:, None, :]
        zmod = jnp.zeros((bs // SUBLANES, 1, 6 * D_MODEL), F32)
        pprev, _ = _inproj(state_shift[l].reshape(bs // SUBLANES, SUBLANES, D_MODEL), zmod, prm['w_in'], 2, SUBLANES)
        pprev = pprev.reshape(bs, C_TOT)[:, :C_A]
        x1p, h2p, st_p = _mix_group(
            yp, mod_p, jnp.zeros((bp, C_A), F32), jnp.zeros((bp, N_HEADS, HEAD, HEAD), F32),
            jnp.zeros((bp, D_B), F32), jnp.zeros((bp, CONV_W - 1, D_B), F32), prm,
            ns_mix=1, tb_mix=ROWS, chunk=16, t_valid=ROWS, ns_big=1, tb_big=min(512, tp), alpha=alpha)
        x1s, h2s, st_s = _mix_group(
            ys, mod_s, pprev, state_wkv[l], state_lru[l], state_conv[l], prm,
            ns_mix=ROWS // SAMPLE_TPAD, tb_mix=SAMPLE_TPAD, chunk=SAMPLE_TPAD, t_valid=ts,
            ns_big=min(64, bs), tb_big=SAMPLE_TPAD, alpha=alpha)
        h2s = h2s.reshape(bs, SAMPLE_TPAD, D_MODEL)[:, :ts].reshape(bs * ts, D_MODEL)
        h2 = jnp.concatenate([h2p, h2s], axis=0)
        gmat = _route(h2, prm['w_q'], prm['sub_keys'])
        ff = _experts(h2, gmat, peer_u[l], peer_v[l])
        yp = _ln2(x1p, ff[:bp * tp].reshape(bp, tp, D_MODEL), mod_p, ln2_w[l], ln2_b[l], 1, min(512, tp), alpha)
        ys_real = _ln2(x1s[:, :ts], ff[bp * tp:].reshape(bs, ts, D_MODEL), mod_s, ln2_w[l], ln2_b[l],
                       min(64, bs), ts, alpha)
        ys = jnp.pad(ys_real, ((0, 0), (0, SAMPLE_TPAD - ts), (0, 0)))
        for i, a in enumerate(st_p + st_s):
            outs[i].append(a)
    return (yp, ys[:, :ts]) + tuple(jnp.stack(o) for o in outs)
```

```python
import functools

import jax
import jax.numpy as jnp
from jax import lax
from jax.experimental import pallas as pl
from jax.experimental.pallas import tpu as pltpu

F32 = jnp.float32
BF16 = jnp.bfloat16

D_MODEL = 2048
D_A = 1024
D_B = 1024
HEAD = 64
N_HEADS = D_A // HEAD
N_PAIRS = N_HEADS // 2
R_W = 64
R_A = 64
R_G = 128
C_A = 3 * D_A + R_W + R_A + R_G
C_TOT = C_A + 2 * D_B
CONV_W = 4
C_LRU = 8.0
P_HEADS = 8
N_KEYS = 128
N_EXPERTS = N_KEYS * N_KEYS
TOPK = 16
D_QH = 128
LN_EPS = 1e-5
GN_EPS = 64e-5

LANES = 128
SUBLANES = 8
ROWS = 128
SAMPLE_TPAD = 8
VMEM_LIMIT = 56 * 1024 * 1024


def _cparams(sem):
    return pltpu.CompilerParams(dimension_semantics=sem, vmem_limit_bytes=VMEM_LIMIT)


def _split2(x):
    hi = x.astype(BF16)
    lo = (x - hi.astype(F32)).astype(BF16)
    return hi, lo


_NN = (((1,), (0,)), ((), ()))
_NT = (((1,), (1,)), ((), ()))
_TN = (((0,), (0,)), ((), ()))


def _dg(a, b, dims):
    return lax.dot_general(a, b, dims, preferred_element_type=F32)


def _dot1(a, b, dims=_NN):
    return _dg(a.astype(BF16), b.astype(BF16), dims)


def _dot3(a, b, dims=_NN):
    ah, al = _split2(a)
    bh, bl = _split2(b)
    return _dg(ah, bh, dims) + (_dg(ah, bl, dims) + _dg(al, bh, dims))


_state_dot = _dot1


def _dot_exact_rhs(a, b_exact, dims=_NN):
    a1 = a.astype(BF16)
    r1 = a - a1.astype(F32)
    a2 = r1.astype(BF16)
    a3 = (r1 - a2.astype(F32)).astype(BF16)
    return _dg(a1, b_exact, dims) + (_dg(a2, b_exact, dims) + _dg(a3, b_exact, dims))


def _dot_exact_lhs(a_exact, b, dims=_NN):
    b1 = b.astype(BF16)
    r1 = b - b1.astype(F32)
    b2 = r1.astype(BF16)
    b3 = (r1 - b2.astype(F32)).astype(BF16)
    return _dg(a_exact, b1, dims) + (_dg(a_exact, b2, dims) + _dg(a_exact, b3, dims))


def _sigmoid(x):
    return 1.0 / (1.0 + jnp.exp(-x))


def _softplus(x):
    return jnp.maximum(x, 0.0) + jnp.log(1.0 + jnp.exp(-jnp.abs(x)))


def _gelu(x):
    return 0.5 * x * (1.0 + jnp.tanh(0.7978845608028654 * (x + 0.044715 * (x * x * x))))


def _iota(shape, dim):
    return lax.broadcasted_iota(jnp.int32, shape, dim)


def _ada_kernel(c_ref, w_ref, b_ref, o_ref):
    c = c_ref[...]
    s = c * _sigmoid(c)
    o_ref[...] = _dot3(s, w_ref[...]) + b_ref[...]


def _ada(c, w_ada, b_ada):
    n = c.shape[0]
    tn = 512
    return pl.pallas_call(
        _ada_kernel,
        out_shape=jax.ShapeDtypeStruct((n, 6 * D_MODEL), F32),
        grid=(6 * D_MODEL // tn,),
        in_specs=[pl.BlockSpec((n, D_MODEL), lambda j: (0, 0)),
                  pl.BlockSpec((D_MODEL, tn), lambda j: (0, j)),
                  pl.BlockSpec((1, tn), lambda j: (0, j))],
        out_specs=pl.BlockSpec((n, tn), lambda j: (0, j)),
        compiler_params=_cparams(("arbitrary",)),
        name="ada",
    )(c, w_ada, b_ada.reshape(1, -1))


def _inproj_kernel(x_ref, sc_ref, sh_ref, w_ref, o_ref, h_ref):
    h = x_ref[...] * (1.0 + sc_ref[...]) + sh_ref[...]
    ns, tb, _ = h.shape
    h_ref[...] = h[:, tb - SUBLANES:, :]
    p = _dot1(h.reshape(ns * tb, D_MODEL), w_ref[...])
    o_ref[...] = p.reshape(ns, tb, p.shape[-1])


def _inproj(x, mod, w_bf16, ns, tb):
    S, T, _ = x.shape
    ncol = w_bf16.shape[1]
    tn = ncol // 3
    proj, h_last = pl.pallas_call(
        _inproj_kernel,
        out_shape=(jax.ShapeDtypeStruct((S, T, ncol), F32),
                   jax.ShapeDtypeStruct((3, S, SUBLANES, D_MODEL), F32)),
        grid=(3, S // ns, T // tb),
        in_specs=[pl.BlockSpec((ns, tb, D_MODEL), lambda j, s, t: (s, t, 0)),
                  pl.BlockSpec((ns, 1, D_MODEL), lambda j, s, t: (s, 0, 1)),
                  pl.BlockSpec((ns, 1, D_MODEL), lambda j, s, t: (s, 0, 0)),
                  pl.BlockSpec((D_MODEL, tn), lambda j, s, t: (0, j))],
        out_specs=(pl.BlockSpec((ns, tb, tn), lambda j, s, t: (s, t, j)),
                   pl.BlockSpec((None, ns, SUBLANES, D_MODEL), lambda j, s, t: (j, s, 0, 0))),
        compiler_params=_cparams(("arbitrary", "arbitrary", "arbitrary")),
        name="inproj",
    )(x, mod, mod, w_bf16)
    return proj, h_last[0]


def _mixer_kernel(proj_ref, pprev_ref, conv0_ref, s0_ref, lru0_ref,
                  mu_ref, w0_ref, ww2_ref, a0_ref, wa2_ref, wg2_ref, kk_ref, ka_ref, rk_ref,
                  lnw_ref, lnb_ref, seg_ref, segt_ref,
                  cw_ref, cb_ref, wga_ref, bga_ref, wgi_ref, bgi_ref, lam_ref,
                  y_ref, sout_ref, lruo_ref, convo_ref,
                  prev_ref, hst_ref, st_ref,
                  bp_ref, rp_ref, x2_ref, y2_ref, v_ref, kb_ref, ab_ref, wt_ref, yacc_ref,
                  *, ns, tb, chunk, t_valid, carry_state):
    R = ns * tb
    t = pl.program_id(1)
    zero_blk = jnp.zeros((HEAD, HEAD), F32)

    def pair_tile(ref, s, pr):
        top = jnp.concatenate([ref[s, 2 * pr], zero_blk], axis=1)
        bot = jnp.concatenate([zero_blk, ref[s, 2 * pr + 1]], axis=1)
        return jnp.concatenate([top, bot], axis=0)

    @pl.when(t == 0)
    def _():
        prev_ref[...] = jnp.zeros_like(prev_ref)
        prev_ref[:, SUBLANES - 1:, :C_A] = pprev_ref[...]
        prev_ref[:, SUBLANES - (CONV_W - 1):, C_A:] = conv0_ref[...]
        hst_ref[...] = jnp.broadcast_to(lru0_ref[...], hst_ref.shape)

        if carry_state:
            for pr in range(N_PAIRS):
                st_ref[0, pr] = pair_tile(s0_ref, 0, pr)

    cur = proj_ref[...]
    ext = jnp.concatenate([prev_ref[...], cur[:, :, :C_A + D_B]], axis=1)
    prev_ref[...] = cur[:, tb - SUBLANES:, :C_A + D_B]

    row = _iota((R, 1), 0)
    rowin = row % tb
    valid = rowin < t_valid

    pa = cur[:, :, :C_A].reshape(R, C_A)
    shifted = ext[:, SUBLANES - 1:SUBLANES - 1 + tb, :C_A].reshape(R, C_A)
    p = pa + (shifted - pa) * mu_ref[...]
    r = p[:, :D_A]
    k = p[:, D_A:2 * D_A]
    v = p[:, 2 * D_A:3 * D_A]
    xwa = p[:, 3 * D_A:3 * D_A + LANES]
    xg = p[:, 3 * D_A + LANES:]
    w_log = -_softplus(-(w0_ref[...] + _dot3(jnp.tanh(xwa), ww2_ref[...]))) - 0.5
    logw = -jnp.exp(w_log)
    a = _sigmoid(a0_ref[...] + _dot3(xwa, wa2_ref[...]))
    g = _dot1(_sigmoid(xg), wg2_ref[...])
    seg = seg_ref[...]
    segt = segt_ref[...]

    def seg_sum(x):
        return _dot_exact_rhs(_dot_exact_rhs(x, seg), segt)

    kk = k * kk_ref[...]
    kk = kk / jnp.maximum(jnp.sqrt(seg_sum(kk * kk)), 1e-12)
    k = k * (1.0 + (a - 1.0) * ka_ref[...])
    kka = kk * a
    bonus = seg_sum(r * k * rk_ref[...]) * v
    logw = jnp.where(valid, logw, 0.0)
    r = jnp.where(valid, r, 0.0)
    k = jnp.where(valid, k, 0.0)
    v = jnp.where(valid, v, 0.0)
    kk = jnp.where(valid, kk, 0.0)
    kka = jnp.where(valid, kka, 0.0)

    ri = _iota((R, R), 0)
    ci = _iota((R, R), 1)
    same = (ri // chunk) == (ci // chunk)
    incl = same & (ci <= ri)
    strict = same & (ci < ri)
    lmat = jnp.concatenate([jnp.where(incl, 1.0, 0.0), jnp.where(same, 1.0, 0.0)], axis=0).astype(BF16)
    cums = _dot_exact_lhs(lmat, logw)
    cum = cums[:R]
    tot = cums[R:]
    e_prev = jnp.exp(cum - logw)
    e_neg = jnp.exp(-cum)
    e_pos = jnp.exp(cum)
    e_rem = jnp.exp(tot - cum)
    beta = kk * e_prev
    alpha = kka * e_neg
    kappa = k * e_neg
    rho = r * e_pos
    v_ref[...] = v
    kb_ref[...] = k * e_rem
    ab_ref[...] = -(kka * e_rem)
    wt_ref[...] = jnp.exp(tot)

    lane = _iota((1, LANES), 1)
    masks = (lane < HEAD, lane >= HEAD)
    eye = jnp.where(ri == ci, 1.0, 0.0)
    nsq = {16: 3, 8: 2}[chunk]
    heads = [(pr, m) for pr in range(N_PAIRS) for m in masks]
    psl = lambda pr: slice(pr * LANES, (pr + 1) * LANES)
    bms = [jnp.where(m, beta[:, psl(pr)], 0.0) for pr, m in heads]
    rms = [jnp.where(m, rho[:, psl(pr)], 0.0) for pr, m in heads]
    gms = [_dot1(jnp.concatenate([bm, rm], axis=0),
                 jnp.concatenate([alpha[:, psl(pr)], kappa[:, psl(pr)]], axis=0), _NT)
           for (pr, m), bm, rm in zip(heads, bms, rms)]
    l_bas = [jnp.where(strict, gm[:R, :R], 0.0) for gm in gms]
    m_ras = [jnp.where(incl, gm[R:, :R], 0.0) for gm in gms]
    xy1s = [_dot1(jnp.concatenate([jnp.where(strict, gm[:R, R:], 0.0), jnp.where(incl, gm[R:, R:], 0.0)], axis=0),
                  v[:, psl(pr)])
            for (pr, m), gm in zip(heads, gms)]
    pws = [_dot1(l, l) for l in l_bas]

    def lru_columns(pr):
        sl = psl(pr)
        bsl = slice(C_A + pr * LANES, C_A + (pr + 1) * LANES)
        gsl = slice(C_A + D_B + pr * LANES, C_A + D_B + (pr + 1) * LANES)
        gb = cur[:, :, gsl].reshape(R, LANES)
        xc = cb_ref[:, sl]
        for j in range(CONV_W):
            off = SUBLANES - (CONV_W - 1) + j
            xc = xc + ext[:, off:off + tb, bsl].reshape(R, LANES) * cw_ref[j:j + 1, sl]
        rg = _sigmoid(_dot1(xc, wga_ref[pr]) + bga_ref[:, sl])
        ig = _sigmoid(_dot1(xc, wgi_ref[pr]) + bgi_ref[:, sl])
        log_a = -C_LRU * rg * _softplus(-lam_ref[:, sl])
        av = jnp.where(valid, jnp.exp(log_a), 1.0)
        bv = jnp.where(valid, jnp.sqrt(1.0 - jnp.exp(2.0 * log_a)) * (ig * xc), 0.0)
        d = 1
        while d < tb:
            take = rowin >= d
            a_sh = jnp.where(take, pltpu.roll(av, d, axis=0), 1.0)
            b_sh = jnp.where(take, pltpu.roll(bv, d, axis=0), 0.0)
            bv = av * b_sh + bv
            av = av * a_sh
            d *= 2
        h0 = jnp.broadcast_to(hst_ref[:, SUBLANES - 1:SUBLANES, sl], (ns, tb, LANES)).reshape(R, LANES)
        h3 = (av * h0 + bv).reshape(ns, tb, LANES)
        hst_ref[:, :, sl] = h3[:, tb - SUBLANES:, :]
        y_ref[:, :, D_A + pr * LANES:D_A + (pr + 1) * LANES] = h3 * _gelu(gb).reshape(ns, tb, LANES)

    minvs = [eye - l for l in l_bas]
    lru_todo = list(range(N_PAIRS))
    for i in range(nsq):
        nxt = []
        for hd, (mi, pw) in enumerate(zip(minvs, pws)):
            nxt.append(mi + _dot1(mi, pw))
            if hd % 2 == 1 and lru_todo:
                lru_columns(lru_todo.pop(0))
        minvs = nxt
        if i + 1 < nsq:
            pws = [_dot1(pw, pw) for pw in pws]
    t1s = [_dot1(mi, jnp.concatenate([xy1[:R], bm], axis=1))
           for mi, xy1, bm in zip(minvs, xy1s, bms)]
    t2s = [_dot1(m_ra, t1) for m_ra, t1 in zip(m_ras, t1s)]
    for pr in range(N_PAIRS):
        h0, h1 = 2 * pr, 2 * pr + 1
        sl = psl(pr)
        m1 = masks[1]
        bp_ref[:, sl] = t1s[h0][:, LANES:] + t1s[h1][:, LANES:]
        rp_ref[:, sl] = (rms[h0] - t2s[h0][:, LANES:]) + (rms[h1] - t2s[h1][:, LANES:])
        x2_ref[:, sl] = jnp.where(m1, t1s[h1][:, :LANES], t1s[h0][:, :LANES])
        y2_ref[:, sl] = jnp.where(m1, xy1s[h1][R:] - t2s[h1][:, :LANES], xy1s[h0][R:] - t2s[h0][:, :LANES])

    bi = _iota((LANES, LANES), 0)
    bj = _iota((LANES, LANES), 1)
    bd = (bi < HEAD) == (bj < HEAD)
    nck = tb // chunk
    sites = [(s, c, pr) for s in range(ns) for c in range(nck) for pr in range(N_PAIRS)]
    rows_of = lambda s, c: slice(s * tb + c * chunk, s * tb + (c + 1) * chunk)
    thetas = {}
    psis = {}
    for s, c, pr in sites:
        rs, sl = rows_of(s, c), psl(pr)
        thetas[s, c, pr] = jnp.where(bd, _state_dot(bp_ref[rs, sl], ab_ref[rs, sl], _TN), 0.0)
        vx = jnp.concatenate([v_ref[rs, sl], x2_ref[rs, sl]], axis=0)
        ka = jnp.concatenate([kb_ref[rs, sl], ab_ref[rs, sl]], axis=0)
        psis[s, c, pr] = jnp.where(bd, _state_dot(vx, ka, _TN), 0.0)
    for s in range(ns):
        sps = [st_ref[s, pr] if carry_state else pair_tile(s0_ref, s, pr) for pr in range(N_PAIRS)]
        for c in range(nck):
            rs = rows_of(s, c)
            for pr in range(N_PAIRS):
                yacc_ref[rs, psl(pr)] = _state_dot(rp_ref[rs, psl(pr)], sps[pr], _NT) + y2_ref[rs, psl(pr)]
            sps = [sps[pr] * wt_ref[rs.start:rs.start + 1, psl(pr)]
                   + (_state_dot(sps[pr], thetas[s, c, pr]) + psis[s, c, pr]) for pr in range(N_PAIRS)]
        for pr in range(N_PAIRS):
            if carry_state:
                st_ref[s, pr] = sps[pr]
            else:
                sout_ref[s, 2 * pr] = sps[pr][:HEAD, :HEAD]
                sout_ref[s, 2 * pr + 1] = sps[pr][HEAD:, HEAD:]

    if carry_state:
        @pl.when(t == pl.num_programs(1) - 1)
        def _():
            for pr in range(N_PAIRS):
                tile = st_ref[0, pr]
                sout_ref[0, 2 * pr] = tile[:HEAD, :HEAD]
                sout_ref[0, 2 * pr + 1] = tile[HEAD:, HEAD:]

    y = yacc_ref[...]
    mean = seg_sum(y) * (1.0 / HEAD)
    yc = y - mean
    var = seg_sum(yc * yc) * (1.0 / HEAD)
    yn = yc * lax.rsqrt(var + GN_EPS) * lnw_ref[...] + lnb_ref[...]
    ya = (yn + bonus) * g
    y_ref[:, :, :D_A] = ya.reshape(ns, tb, D_A)

    @pl.when(t == pl.num_programs(1) - 1)
    def _():
        lruo_ref[...] = hst_ref[...]
        convo_ref[...] = cur[:, tb - SUBLANES:, C_A:C_A + D_B]


def _mixer(proj, p_prev, conv0, s0, lru0, prm, ns, tb, chunk, t_valid):
    S, T, _ = proj.shape
    R = ns * tb
    assert R == ROWS
    carry_state = T // tb > 1
    assert ns == 1 or not carry_state
    kern = functools.partial(_mixer_kernel, ns=ns, tb=tb, chunk=chunk, t_valid=t_valid, carry_state=carry_state)

    def full(a):
        nd = a.ndim
        return pl.BlockSpec(a.shape, lambda s, t, nd=nd: (0,) * nd)

    params = [prm[n] for n in ('mu', 'w0', 'ww2', 'a0', 'wa2', 'wg2', 'k_k', 'k_a', 'r_k', 'lnx_w', 'lnx_b',
                               'seg', 'segt', 'conv_w', 'conv_b', 'wga', 'bga', 'wgi', 'bgi', 'lam')]
    slab = lambda w: pltpu.VMEM((R, w), F32)
    return pl.pallas_call(
        kern,
        out_shape=(jax.ShapeDtypeStruct((S, T, D_A + D_B), F32),
                   jax.ShapeDtypeStruct((S, N_HEADS, HEAD, HEAD), F32),
                   jax.ShapeDtypeStruct((S, SUBLANES, D_B), F32),
                   jax.ShapeDtypeStruct((S, SUBLANES, D_B), F32)),
        grid=(S // ns, T // tb),
        in_specs=[pl.BlockSpec((ns, tb, C_TOT), lambda s, t: (s, t, 0)),
                  pl.BlockSpec((ns, 1, C_A), lambda s, t: (s, 0, 0)),
                  pl.BlockSpec((ns, CONV_W - 1, D_B), lambda s, t: (s, 0, 0)),
                  pl.BlockSpec((ns, N_HEADS, HEAD, HEAD), lambda s, t: (s, 0, 0, 0),
                               pipeline_mode=pl.Buffered(1)),
                  pl.BlockSpec((ns, 1, D_B), lambda s, t: (s, 0, 0))]
                 + [full(a) for a in params],
        out_specs=(pl.BlockSpec((ns, tb, D_A + D_B), lambda s, t: (s, t, 0)),
                   pl.BlockSpec((ns, N_HEADS, HEAD, HEAD), lambda s, t: (s, 0, 0, 0),
                                pipeline_mode=pl.Buffered(1)),
                   pl.BlockSpec((ns, SUBLANES, D_B), lambda s, t: (s, 0, 0)),
                   pl.BlockSpec((ns, SUBLANES, D_B), lambda s, t: (s, 0, 0))),
        scratch_shapes=[pltpu.VMEM((ns, SUBLANES, C_A + D_B), F32),
                        pltpu.VMEM((ns, SUBLANES, D_B), F32),
                        pltpu.VMEM((1, N_PAIRS, LANES, LANES), F32)]
                       + [slab(D_A) for _ in range(9)],
        compiler_params=_cparams(("arbitrary", "arbitrary")),
        name="mixer",
    )(proj, p_prev, conv0, s0, lru0, *params)


def _layernorm(x, w, b):
    mean = jnp.mean(x, axis=-1, keepdims=True)
    xc = x - mean
    var = jnp.mean(xc * xc, axis=-1, keepdims=True)
    return xc * lax.rsqrt(var + LN_EPS) * w + b


def _outproj_kernel(y_ref, x_ref, g1_ref, sc2_ref, sh2_ref, w_ref, lnw_ref, lnb_ref, x1_ref, h2_ref, *, alpha):
    ns, tb, _ = y_ref.shape
    mix = _dot1(y_ref[...].reshape(ns * tb, D_MODEL), w_ref[...]).reshape(ns, tb, D_MODEL)
    x1 = _layernorm(alpha * x_ref[...] + (1.0 + g1_ref[...]) * mix, lnw_ref[...], lnb_ref[...])
    x1_ref[...] = x1
    h2 = x1 * (1.0 + sc2_ref[...]) + sh2_ref[...]
    h2_ref[...] = h2.reshape(ns * tb, D_MODEL).astype(BF16)


def _outproj(y, x, mod, w_bf16, ln_w, ln_b, ns, tb, alpha):
    S, T, _ = x.shape
    nt = T // tb
    blk = pl.BlockSpec((ns, tb, D_MODEL), lambda s, t: (s, t, 0))
    modspec = lambda i: pl.BlockSpec((ns, 1, D_MODEL), lambda s, t, i=i: (s, 0, i))
    vec = pl.BlockSpec((1, D_MODEL), lambda s, t: (0, 0))
    return pl.pallas_call(
        functools.partial(_outproj_kernel, alpha=alpha),
        out_shape=(jax.ShapeDtypeStruct((S, T, D_MODEL), F32), jax.ShapeDtypeStruct((S * T, D_MODEL), BF16)),
        grid=(S // ns, nt),
        in_specs=[blk, blk, modspec(2), modspec(4), modspec(3),
                  pl.BlockSpec((D_MODEL, D_MODEL), lambda s, t: (0, 0)), vec, vec],
        out_specs=(blk, pl.BlockSpec((ns * tb, D_MODEL), lambda s, t: (s * nt + t, 0))),
        compiler_params=_cparams(("arbitrary", "arbitrary")),
        name="outproj",
    )(y, x, mod, mod, mod, w_bf16, ln_w.reshape(1, -1), ln_b.reshape(1, -1))


def _topk_rows(s, ridx, n_rows):
    out_i = _iota((TOPK, s.shape[1]), 0)
    vals = jnp.zeros((TOPK, s.shape[1]), F32)
    idxs = jnp.zeros((TOPK, s.shape[1]), F32)
    for it in range(TOPK):
        m = jnp.max(s, axis=0, keepdims=True)
        idx = jnp.min(jnp.where(s == m, ridx, float(n_rows)), axis=0, keepdims=True)
        vals = jnp.where(out_i == it, m, vals)
        idxs = jnp.where(out_i == it, idx, idxs)
        s = jnp.where(ridx == idx, -jnp.inf, s)
    return vals, idxs


def _route_kernel(h_ref, wq_ref, sk_ref, g_ref, gate_ref, e1_ref, e2_ref):
    R = ROWS
    q = _dot1(h_ref[...], wq_ref[...])
    sk0 = sk_ref[0]
    sk1 = sk_ref[1]
    gates, e1s, e2s = [], [], []
    key_rows = _iota((N_KEYS, R), 0).astype(F32)
    r16 = _iota((TOPK, R), 0).astype(F32)
    r8 = _iota((SUBLANES, R), 0).astype(F32)
    ea_rank = jnp.where(r8 < 3, 2.0, jnp.where(r8 < 5, 3.0, 4.0))
    eb_rank = jnp.where((r8 == 0) | (r8 == 3) | (r8 == 5), 2.0, jnp.where((r8 == 1) | (r8 == 4), 3.0, 4.0))
    flat = jnp.concatenate([r16, TOPK + r8, TOPK * r16, TOPK * r8 + 1.0, TOPK * ea_rank + eb_rank], axis=0)
    neg = -jnp.inf
    for hd in range(P_HEADS):
        base = hd * 2 * D_QH
        s1 = _dot1(sk0, q[:, base:base + D_QH], _NT)
        s2 = _dot1(sk1, q[:, base + D_QH:base + 2 * D_QH], _NT)
        v1, i1 = _topk_rows(s1, key_rows, N_KEYS)
        v2, i2 = _topk_rows(s2, key_rows, N_KEYS)
        ea = jnp.where(r8 < 3, v1[2:3], jnp.where(r8 < 5, v1[3:4], v1[4:5]))
        eb = jnp.where(eb_rank == 2.0, v2[2:3], jnp.where(eb_rank == 3.0, v2[3:4], v2[4:5]))
        cand = jnp.concatenate([
            v1[0:1] + v2,
            v1[1:2] + v2[:SUBLANES],
            jnp.where(r16 >= 2, v1 + v2[0:1], neg),
            jnp.where(r8 >= 2, v1[:SUBLANES] + v2[1:2], neg),
            jnp.where(r8 < 6, ea + eb, neg),
        ], axis=0)
        sv, ci = _topk_rows(cand, flat, TOPK * TOPK)
        ca = jnp.floor(ci * (1.0 / TOPK))
        cb = ci - TOPK * ca
        e1 = jnp.zeros((TOPK, R), F32)
        e2 = jnp.zeros((TOPK, R), F32)
        for a in range(TOPK):
            e1 = jnp.where(ca == float(a), i1[a:a + 1, :], e1)
            e2 = jnp.where(cb == float(a), i2[a:a + 1, :], e2)
        ex = jnp.exp(sv - jnp.max(sv, axis=0, keepdims=True))
        gates.append(ex / jnp.sum(ex, axis=0, keepdims=True))
        e1s.append(e1)
        e2s.append(e2)
    gate_ref[...] = jnp.concatenate(gates, axis=0).T
    e1_ref[...] = jnp.concatenate(e1s, axis=0).T
    e2_ref[...] = jnp.concatenate(e2s, axis=0).T
    key_i = _iota((N_KEYS, P_HEADS * TOPK), 0).astype(F32)

    def per_octet(o, carry):
        base = pl.multiple_of(o * SUBLANES, SUBLANES)
        tiles = []
        for j in range(SUBLANES):
            sel1 = key_i == e1_ref[pl.ds(base + j, 1), :]
            sel2 = key_i == e2_ref[pl.ds(base + j, 1), :]
            m1 = jnp.where(sel1, gate_ref[pl.ds(base + j, 1), :], 0.0)
            m2 = jnp.where(sel2, 1.0, 0.0)
            tiles.append(_dot1(m1, m2, _NT))
        by_key = jnp.swapaxes(jnp.stack(tiles, axis=0), 0, 1)
        for i1v in range(N_KEYS):
            g_ref[pl.ds(base, SUBLANES), i1v * N_KEYS:(i1v + 1) * N_KEYS] = by_key[i1v]
        return carry

    lax.fori_loop(0, R // SUBLANES, per_octet, 0, unroll=8)


def _route(h2, wq_bf16, sub_keys):
    n = h2.shape[0]
    assert n % ROWS == 0
    return pl.pallas_call(
        _route_kernel,
        out_shape=jax.ShapeDtypeStruct((n, N_EXPERTS), F32),
        grid=(n // ROWS,),
        in_specs=[pl.BlockSpec((ROWS, D_MODEL), lambda i: (i, 0)),
                  pl.BlockSpec((D_MODEL, D_MODEL), lambda i: (0, 0)),
                  pl.BlockSpec((2, N_KEYS, D_QH), lambda i: (0, 0, 0))],
        out_specs=pl.BlockSpec((ROWS, N_EXPERTS), lambda i: (i, 0)),
        scratch_shapes=[pltpu.VMEM((ROWS, P_HEADS * TOPK), F32) for _ in range(3)],
        compiler_params=_cparams(("arbitrary",)),
        name="route",
    )(h2, wq_bf16, sub_keys)


EXPERT_CHUNK = 512
EXPERT_ROWS_MAX = 1152


EXPERT_SPLIT = 4


def _experts_kernel(h_ref, g_ref, u_ref, v_ref, o_ref):
    @pl.when(pl.program_id(1) == 0)
    def _():
        o_ref[...] = jnp.zeros_like(o_ref)

    u = u_ref[...].astype(BF16)
    v = v_ref[...].astype(BF16)
    rb = h_ref.shape[0] // EXPERT_SPLIT
    rows = [pl.ds(b * rb, rb) for b in range(EXPERT_SPLIT)]
    act = _dg(h_ref[rows[0], :], u, _NT)
    for b in range(EXPERT_SPLIT):
        nxt = _dg(h_ref[rows[b + 1], :], u, _NT) if b + 1 < EXPERT_SPLIT else None
        w = (g_ref[rows[b], :] * _gelu(act)).astype(BF16)
        o_ref[rows[b], :] += _dg(w, v, _NN)
        act = nxt


def _token_block(n, cap):
    for tb in range(min(n, cap), 0, -1):
        if n % tb == 0 and tb % (2 * SUBLANES * EXPERT_SPLIT) == 0:
            return tb
    raise ValueError(n)


def _experts(h2, gmat, peer_u, peer_v):
    n = h2.shape[0]
    tb = _token_block(n, EXPERT_ROWS_MAX)
    ec = EXPERT_CHUNK
    return pl.pallas_call(
        _experts_kernel,
        out_shape=jax.ShapeDtypeStruct((n, D_MODEL), F32),
        grid=(n // tb, N_EXPERTS // ec),
        in_specs=[pl.BlockSpec((tb, D_MODEL), lambda i, e: (i, 0)),
                  pl.BlockSpec((tb, ec), lambda i, e: (i, e)),
                  pl.BlockSpec((ec, D_MODEL), lambda i, e: (e, 0)),
                  pl.BlockSpec((ec, D_MODEL), lambda i, e: (e, 0))],
        out_specs=pl.BlockSpec((tb, D_MODEL), lambda i, e: (i, 0), pipeline_mode=pl.Buffered(1)),
        compiler_params=_cparams(("arbitrary", "arbitrary")),
        name="experts",
    )(h2, gmat, peer_u, peer_v)


def _ln2_kernel(x1_ref, ff_ref, g2_ref, lnw_ref, lnb_ref, o_ref, *, alpha):
    ff = ff_ref[...].reshape(x1_ref.shape)
    o_ref[...] = _layernorm(alpha * x1_ref[...] + (1.0 + g2_ref[...]) * ff, lnw_ref[...], lnb_ref[...])


def _ln2(x1, ff, mod, ln_w, ln_b, ns, tb, alpha):
    S, T, _ = x1.shape
    nt = T // tb
    blk = pl.BlockSpec((ns, tb, D_MODEL), lambda s, t: (s, t, 0))
    vec = pl.BlockSpec((1, D_MODEL), lambda s, t: (0, 0))
    if ff.ndim == 2:
        assert ns == 1
        ffspec = pl.BlockSpec((tb, D_MODEL), lambda s, t: (s * nt + t, 0))
    else:
        ffspec = blk
    return pl.pallas_call(
        functools.partial(_ln2_kernel, alpha=alpha),
        out_shape=jax.ShapeDtypeStruct((S, T, D_MODEL), F32),
        grid=(S // ns, nt),
        in_specs=[blk, ffspec, pl.BlockSpec((ns, 1, D_MODEL), lambda s, t: (s, 0, 5)), vec, vec],
        out_specs=blk,
        compiler_params=_cparams(("arbitrary", "arbitrary")),
        name="ln2",
    )(x1, ff, mod, ln_w.reshape(1, -1), ln_b.reshape(1, -1))


def _pair_blockdiag(w):
    n = w.shape[0] // 2
    w = w.reshape(n, 2, HEAD, HEAD)
    z = jnp.zeros((n, HEAD, HEAD), w.dtype)
    top = jnp.concatenate([w[:, 0], z], axis=2)
    bot = jnp.concatenate([z, w[:, 1]], axis=2)
    return jnp.concatenate([top, bot], axis=1)


def _mix_group(x, mod, p_prev, wkv0, lru0, conv0, prm, *, ns_mix, tb_mix, chunk, t_valid, ns_big, tb_big, alpha):
    S, T, _ = x.shape
    proj, h_last = _inproj(x, mod, prm['w_in'], ns_big, tb_big)
    y, wkv, lru_o, conv_o = _mixer(proj, p_prev[:, None, :], conv0, wkv0, lru0[:, None, :], prm,
                                   ns_mix, tb_mix, chunk, t_valid)
    x1, h2 = _outproj(y, x, mod, prm['w_out'], prm['ln1_w'], prm['ln1_b'], ns_big, tb_big, alpha)
    tv = (t_valid - 1) % SUBLANES
    shift = h_last[:, tv]
    lru = lru_o[:, SUBLANES - 1]
    conv = conv_o[:, tv - (CONV_W - 2):tv + 1]
    return x1, h2, (shift, wkv, lru, conv)


def kernel(x_prompt, x_sample, c_prompt, c_sample, state_shift, state_wkv, state_lru, state_conv, w_ada, b_ada, w_in, mu_shift, w0, w_w2, a0, w_a2, w_g2, k_k, k_a, r_k, lnx_w, lnx_b, conv_w, conv_b, w_gate_a, b_gate_a, w_gate_i, b_gate_i, lru_lambda, w_out, ln1_w, ln1_b, w_q, sub_keys, peer_u, peer_v, ln2_w, ln2_b):
    depth = w_ada.shape[0]
    alpha = (2 * depth) ** 0.25
    bp, tp, _ = x_prompt.shape
    bs, ts, _ = x_sample.shape
    yp = x_prompt
    ys = jnp.pad(x_sample, ((0, 0), (0, SAMPLE_TPAD - ts), (0, 0)))
    head_of = jnp.arange(D_A) // HEAD
    seg = (head_of[:, None] == jnp.arange(LANES)[None, :]).astype(BF16)
    row = lambda a: a.reshape(1, -1)
    outs = [[] for _ in range(8)]
    for l in range(depth):
        zpad = jnp.zeros((LANES - R_W, D_A), F32)
        prm = {
            'w_in': w_in[l].astype(BF16), 'w_out': w_out[l].astype(BF16), 'w_q': w_q[l].astype(BF16),
            'sub_keys': sub_keys[l],
            'mu': row(mu_shift[l]), 'w0': row(w0[l]),
            'ww2': jnp.concatenate([w_w2[l], zpad], axis=0), 'a0': row(a0[l]),
            'wa2': jnp.concatenate([zpad, w_a2[l]], axis=0), 'wg2': w_g2[l].astype(BF16),
            'k_k': row(k_k[l]), 'k_a': row(k_a[l]), 'r_k': row(r_k[l]),
            'lnx_w': row(lnx_w[l]), 'lnx_b': row(lnx_b[l]), 'seg': seg, 'segt': seg.T,
            'conv_w': conv_w[l], 'conv_b': row(conv_b[l]),
            'wga': _pair_blockdiag(w_gate_a[l]).astype(BF16), 'bga': row(b_gate_a[l]),
            'wgi': _pair_blockdiag(w_gate_i[l]).astype(BF16), 'bgi': row(b_gate_i[l]),
            'lam': row(lru_lambda[l]),
            'ln1_w': ln1_w[l], 'ln1_b': ln1_b[l], 'ln2_w': ln2_w[l], 'ln2_b': ln2_b[l],
        }
        c_all = jnp.concatenate([c_prompt, c_sample], axis=0)
        mod = _ada(c_all, w_ada[l], b_ada[l])
        mod_p = mod[:bp, None, :]
        mod_s = mod[bp:, None, :]
        zmod = jnp.zeros((bs // SUBLANES, 1, 6 * D_MODEL), F32)
        pprev, _ = _inproj(state_shift[l].reshape(bs // SUBLANES, SUBLANES, D_MODEL), zmod, prm['w_in'], 2, SUBLANES)
        pprev = pprev.reshape(bs, C_TOT)[:, :C_A]
        x1p, h2p, st_p = _mix_group(
            yp, mod_p, jnp.zeros((bp, C_A), F32), jnp.zeros((bp, N_HEADS, HEAD, HEAD), F32),
            jnp.zeros((bp, D_B), F32), jnp.zeros((bp, CONV_W - 1, D_B), F32), prm,
            ns_mix=1, tb_mix=ROWS, chunk=16, t_valid=ROWS, ns_big=1, tb_big=min(512, tp), alpha=alpha)
        x1s, h2s, st_s = _mix_group(
            ys, mod_s, pprev, state_wkv[l], state_lru[l], state_conv[l], prm,
            ns_mix=ROWS // SAMPLE_TPAD, tb_mix=SAMPLE_TPAD, chunk=SAMPLE_TPAD, t_valid=ts,
            ns_big=min(64, bs), tb_big=SAMPLE_TPAD, alpha=alpha)
        h2s = h2s.reshape(bs, SAMPLE_TPAD, D_MODEL)[:, :ts].reshape(bs * ts, D_MODEL)
        h2 = jnp.concatenate([h2p, h2s], axis=0)
        gmat = _route(h2, prm['w_q'], prm['sub_keys'])
        ff = _experts(h2, gmat, peer_u[l], peer_v[l])
        yp = _ln2(x1p, ff, mod_p, ln2_w[l], ln2_b[l], 1, min(512, tp), alpha)
        ys_real = _ln2(x1s[:, :ts], ff[bp * tp:].reshape(bs, ts, D_MODEL), mod_s, ln2_w[l], ln2_b[l],
                       min(64, bs), ts, alpha)
        ys = jnp.pad(ys_real, ((0, 0), (0, SAMPLE_TPAD - ts), (0, 0)))
        for i, a in enumerate(st_p + st_s):
            outs[i].append(a)
    return (yp, ys[:, :ts]) + tuple(jnp.stack(o) for o in outs)
```

```python
import functools

import jax
import jax.numpy as jnp
from jax import lax
from jax.experimental import pallas as pl
from jax.experimental.pallas import tpu as pltpu

F32 = jnp.float32
BF16 = jnp.bfloat16

D_MODEL = 2048
D_A = 1024
D_B = 1024
HEAD = 64
N_HEADS = D_A // HEAD
N_PAIRS = N_HEADS // 2
R_W = 64
R_A = 64
R_G = 128
C_A = 3 * D_A + R_W + R_A + R_G
C_TOT = C_A + 2 * D_B
CONV_W = 4
C_LRU = 8.0
P_HEADS = 8
N_KEYS = 128
N_EXPERTS = N_KEYS * N_KEYS
TOPK = 16
D_QH = 128
LN_EPS = 1e-5
GN_EPS = 64e-5

LANES = 128
SUBLANES = 8
ROWS = 128
SAMPLE_TPAD = 8
VMEM_LIMIT = 56 * 1024 * 1024


def _cparams(sem):
    return pltpu.CompilerParams(dimension_semantics=sem, vmem_limit_bytes=VMEM_LIMIT)


def _split2(x):
    hi = x.astype(BF16)
    lo = (x - hi.astype(F32)).astype(BF16)
    return hi, lo


_NN = (((1,), (0,)), ((), ()))
_NT = (((1,), (1,)), ((), ()))
_TN = (((0,), (0,)), ((), ()))


def _dg(a, b, dims):
    return lax.dot_general(a, b, dims, preferred_element_type=F32)


def _dot1(a, b, dims=_NN):
    return _dg(a.astype(BF16), b.astype(BF16), dims)


def _dot3(a, b, dims=_NN):
    ah, al = _split2(a)
    bh, bl = _split2(b)
    return _dg(ah, bh, dims) + (_dg(ah, bl, dims) + _dg(al, bh, dims))


_state_dot = _dot1


def _dot_exact_rhs(a, b_exact, dims=_NN):
    a1 = a.astype(BF16)
    r1 = a - a1.astype(F32)
    a2 = r1.astype(BF16)
    a3 = (r1 - a2.astype(F32)).astype(BF16)
    return _dg(a1, b_exact, dims) + (_dg(a2, b_exact, dims) + _dg(a3, b_exact, dims))


def _dot_exact_lhs(a_exact, b, dims=_NN):
    b1 = b.astype(BF16)
    r1 = b - b1.astype(F32)
    b2 = r1.astype(BF16)
    b3 = (r1 - b2.astype(F32)).astype(BF16)
    return _dg(a_exact, b1, dims) + (_dg(a_exact, b2, dims) + _dg(a_exact, b3, dims))


def _sigmoid(x):
    return 1.0 / (1.0 + jnp.exp(-x))


def _softplus(x):
    return jnp.maximum(x, 0.0) + jnp.log(1.0 + jnp.exp(-jnp.abs(x)))


def _gelu(x):
    return 0.5 * x * (1.0 + jnp.tanh(0.7978845608028654 * (x + 0.044715 * (x * x * x))))


def _iota(shape, dim):
    return lax.broadcasted_iota(jnp.int32, shape, dim)


def _ada_kernel(c_ref, w_ref, b_ref, o_ref):
    c = c_ref[...]
    s = c * _sigmoid(c)
    o_ref[...] = _dot3(s, w_ref[...]) + b_ref[...]


def _ada(c, w_ada, b_ada):
    n = c.shape[0]
    tn = 512
    return pl.pallas_call(
        _ada_kernel,
        out_shape=jax.ShapeDtypeStruct((n, 6 * D_MODEL), F32),
        grid=(6 * D_MODEL // tn,),
        in_specs=[pl.BlockSpec((n, D_MODEL), lambda j: (0, 0)),
                  pl.BlockSpec((D_MODEL, tn), lambda j: (0, j)),
                  pl.BlockSpec((1, tn), lambda j: (0, j))],
        out_specs=pl.BlockSpec((n, tn), lambda j: (0, j)),
        compiler_params=_cparams(("arbitrary",)),
        name="ada",
    )(c, w_ada, b_ada.reshape(1, -1))


def _inproj_kernel(x_ref, sc_ref, sh_ref, w_ref, o_ref, h_ref):
    h = x_ref[...] * (1.0 + sc_ref[...]) + sh_ref[...]
    ns, tb, _ = h.shape
    h_ref[...] = h[:, tb - SUBLANES:, :]
    p = _dot1(h.reshape(ns * tb, D_MODEL), w_ref[...])
    o_ref[...] = p.reshape(ns, tb, p.shape[-1])


def _inproj(x, mod, w_bf16, ns, tb):
    S, T, _ = x.shape
    ncol = w_bf16.shape[1]
    tn = ncol // 3
    proj, h_last = pl.pallas_call(
        _inproj_kernel,
        out_shape=(jax.ShapeDtypeStruct((S, T, ncol), F32),
                   jax.ShapeDtypeStruct((3, S, SUBLANES, D_MODEL), F32)),
        grid=(3, S // ns, T // tb),
        in_specs=[pl.BlockSpec((ns, tb, D_MODEL), lambda j, s, t: (s, t, 0)),
                  pl.BlockSpec((ns, 1, D_MODEL), lambda j, s, t: (s, 0, 1)),
                  pl.BlockSpec((ns, 1, D_MODEL), lambda j, s, t: (s, 0, 0)),
                  pl.BlockSpec((D_MODEL, tn), lambda j, s, t: (0, j))],
        out_specs=(pl.BlockSpec((ns, tb, tn), lambda j, s, t: (s, t, j)),
                   pl.BlockSpec((None, ns, SUBLANES, D_MODEL), lambda j, s, t: (j, s, 0, 0))),
        compiler_params=_cparams(("arbitrary", "arbitrary", "arbitrary")),
        name="inproj",
    )(x, mod, mod, w_bf16)
    return proj, h_last[0]


def _pair_tile(ref, s, pr):
    zero_blk = jnp.zeros((HEAD, HEAD), F32)
    top = jnp.concatenate([ref[s, 2 * pr], zero_blk], axis=1)
    bot = jnp.concatenate([zero_blk, ref[s, 2 * pr + 1]], axis=1)
    return jnp.concatenate([top, bot], axis=0)


def _mixer_stream(proj_ref, s0_ref,
                  mu_ref, w0_ref, ww2_ref, a0_ref, wa2_ref, wg2_ref, kk_ref, ka_ref, rk_ref,
                  lnw_ref, lnb_ref, seg_ref, segt_ref,
                  cw_ref, cb_ref, wga_ref, bga_ref, wgi_ref, bgi_ref, lam_ref,
                  y_ref, sout_ref,
                  prev_ref, hst_ref, st_ref,
                  bp_ref, rp_ref, x2_ref, y2_ref, v_ref, kb_ref, ab_ref, wt_ref, yacc_ref,
                  *, ns, tb, chunk, t_valid, carry_state):
    R = ns * tb
    pair_tile = _pair_tile

    cur = proj_ref[...]
    ext = jnp.concatenate([prev_ref[...], cur[:, :, :C_A + D_B]], axis=1)
    prev_ref[...] = cur[:, tb - SUBLANES:, :C_A + D_B]

    row = _iota((R, 1), 0)
    rowin = row % tb
    valid = rowin < t_valid

    pa = cur[:, :, :C_A].reshape(R, C_A)
    shifted = ext[:, SUBLANES - 1:SUBLANES - 1 + tb, :C_A].reshape(R, C_A)
    p = pa + (shifted - pa) * mu_ref[...]
    r = p[:, :D_A]
    k = p[:, D_A:2 * D_A]
    v = p[:, 2 * D_A:3 * D_A]
    xwa = p[:, 3 * D_A:3 * D_A + LANES]
    xg = p[:, 3 * D_A + LANES:]
    w_log = -_softplus(-(w0_ref[...] + _dot3(jnp.tanh(xwa), ww2_ref[...]))) - 0.5
    logw = -jnp.exp(w_log)
    a = _sigmoid(a0_ref[...] + _dot3(xwa, wa2_ref[...]))
    g = _dot1(_sigmoid(xg), wg2_ref[...])
    yield 900
    seg = seg_ref[...]
    segt = segt_ref[...]

    def seg_sum(x):
        return _dot_exact_rhs(_dot_exact_rhs(x, seg), segt)

    kk = k * kk_ref[...]
    kk = kk / jnp.maximum(jnp.sqrt(seg_sum(kk * kk)), 1e-12)
    yield 600
    k = k * (1.0 + (a - 1.0) * ka_ref[...])
    kka = kk * a
    bonus = seg_sum(r * k * rk_ref[...]) * v
    yield 600
    logw = jnp.where(valid, logw, 0.0)
    r = jnp.where(valid, r, 0.0)
    k = jnp.where(valid, k, 0.0)
    v = jnp.where(valid, v, 0.0)
    kk = jnp.where(valid, kk, 0.0)
    kka = jnp.where(valid, kka, 0.0)

    ri = _iota((R, R), 0)
    ci = _iota((R, R), 1)
    same = (ri // chunk) == (ci // chunk)
    incl = same & (ci <= ri)
    strict = same & (ci < ri)
    lmat = jnp.concatenate([jnp.where(incl, 1.0, 0.0), jnp.where(same, 1.0, 0.0)], axis=0).astype(BF16)
    cums = _dot_exact_lhs(lmat, logw)
    cum = cums[:R]
    tot = cums[R:]
    yield 500
    e_prev = jnp.exp(cum - logw)
    e_neg = jnp.exp(-cum)
    e_pos = jnp.exp(cum)
    e_rem = jnp.exp(tot - cum)
    beta = kk * e_prev
    alpha = kka * e_neg
    kappa = k * e_neg
    rho = r * e_pos
    yield 600
    v_ref[...] = v
    kb_ref[...] = k * e_rem
    ab_ref[...] = -(kka * e_rem)
    wt_ref[...] = jnp.exp(tot)
    yield 400

    lane = _iota((1, LANES), 1)
    masks = (lane < HEAD, lane >= HEAD)
    eye = jnp.where(ri == ci, 1.0, 0.0)
    nsq = {16: 3, 8: 2}[chunk]
    heads = [(pr, m) for pr in range(N_PAIRS) for m in masks]
    psl = lambda pr: slice(pr * LANES, (pr + 1) * LANES)
    bms = [jnp.where(m, beta[:, psl(pr)], 0.0) for pr, m in heads]
    rms = [jnp.where(m, rho[:, psl(pr)], 0.0) for pr, m in heads]
    yield 300
    gms = []
    for (pr, m), bm, rm in zip(heads, bms, rms):
        gms.append(_dot1(jnp.concatenate([bm, rm], axis=0),
                         jnp.concatenate([alpha[:, psl(pr)], kappa[:, psl(pr)]], axis=0), _NT))
        yield 70
    l_bas = [jnp.where(strict, gm[:R, :R], 0.0) for gm in gms]
    m_ras = [jnp.where(incl, gm[R:, :R], 0.0) for gm in gms]
    xy1s = []
    for (pr, m), gm in zip(heads, gms):
        xy1s.append(_dot1(jnp.concatenate([jnp.where(strict, gm[:R, R:], 0.0),
                                           jnp.where(incl, gm[R:, R:], 0.0)], axis=0), v[:, psl(pr)]))
        yield 70
    pws = []
    for l in l_bas:
        pws.append(_dot1(l, l))
        yield 40

    def lru_steps():
        for pr in range(N_PAIRS):
            sl = psl(pr)
            bsl = slice(C_A + pr * LANES, C_A + (pr + 1) * LANES)
            gsl = slice(C_A + D_B + pr * LANES, C_A + D_B + (pr + 1) * LANES)
            xc = cb_ref[:, sl]
            for j in range(CONV_W):
                off = SUBLANES - (CONV_W - 1) + j
                xc = xc + ext[:, off:off + tb, bsl].reshape(R, LANES) * cw_ref[j:j + 1, sl]
            yield
            rg = _sigmoid(_dot1(xc, wga_ref[pr]) + bga_ref[:, sl])
            ig = _sigmoid(_dot1(xc, wgi_ref[pr]) + bgi_ref[:, sl])
            yield
            log_a = -C_LRU * rg * _softplus(-lam_ref[:, sl])
            av = jnp.where(valid, jnp.exp(log_a), 1.0)
            bv = jnp.where(valid, jnp.sqrt(1.0 - jnp.exp(2.0 * log_a)) * (ig * xc), 0.0)
            yield
            d = 1
            while d < tb:
                take = rowin >= d
                a_sh = jnp.where(take, pltpu.roll(av, d, axis=0), 1.0)
                b_sh = jnp.where(take, pltpu.roll(bv, d, axis=0), 0.0)
                bv = av * b_sh + bv
                av = av * a_sh
                d *= 2
                yield
            h0 = jnp.broadcast_to(hst_ref[:, SUBLANES - 1:SUBLANES, sl], (ns, tb, LANES)).reshape(R, LANES)
            h3 = (av * h0 + bv).reshape(ns, tb, LANES)
            hst_ref[:, :, sl] = h3[:, tb - SUBLANES:, :]
            gb = cur[:, :, gsl].reshape(R, LANES)
            y_ref[:, :, D_A + pr * LANES:D_A + (pr + 1) * LANES] = h3 * _gelu(gb).reshape(ns, tb, LANES)
            yield

    lru = lru_steps()
    minvs = [eye - l for l in l_bas]
    for i in range(nsq):
        nxt = []
        for mi, pw in zip(minvs, pws):
            nxt.append(mi + _dot1(mi, pw))
            next(lru, None)
            yield 60
        minvs = nxt
        if i + 1 < nsq:
            nxt = []
            for pw in pws:
                nxt.append(_dot1(pw, pw))
                next(lru, None)
                yield 60
            pws = nxt
    for _ in lru:
        yield 40
    t1s = []
    for mi, xy1, bm in zip(minvs, xy1s, bms):
        t1s.append(_dot1(mi, jnp.concatenate([xy1[:R], bm], axis=1)))
        yield 50
    t2s = []
    for m_ra, t1 in zip(m_ras, t1s):
        t2s.append(_dot1(m_ra, t1))
        yield 50
    for pr in range(N_PAIRS):
        h0, h1 = 2 * pr, 2 * pr + 1
        sl = psl(pr)
        m1 = masks[1]
        bp_ref[:, sl] = t1s[h0][:, LANES:] + t1s[h1][:, LANES:]
        rp_ref[:, sl] = (rms[h0] - t2s[h0][:, LANES:]) + (rms[h1] - t2s[h1][:, LANES:])
        x2_ref[:, sl] = jnp.where(m1, t1s[h1][:, :LANES], t1s[h0][:, :LANES])
        y2_ref[:, sl] = jnp.where(m1, xy1s[h1][R:] - t2s[h1][:, :LANES], xy1s[h0][R:] - t2s[h0][:, :LANES])

    bi = _iota((LANES, LANES), 0)
    bj = _iota((LANES, LANES), 1)
    bd = (bi < HEAD) == (bj < HEAD)
    nck = tb // chunk
    sites = [(s, c, pr) for s in range(ns) for c in range(nck) for pr in range(N_PAIRS)]
    rows_of = lambda s, c: slice(s * tb + c * chunk, s * tb + (c + 1) * chunk)
    thetas = {}
    psis = {}
    for s, c, pr in sites:
        rs, sl = rows_of(s, c), psl(pr)
        thetas[s, c, pr] = jnp.where(bd, _state_dot(bp_ref[rs, sl], ab_ref[rs, sl], _TN), 0.0)
        vx = jnp.concatenate([v_ref[rs, sl], x2_ref[rs, sl]], axis=0)
        ka = jnp.concatenate([kb_ref[rs, sl], ab_ref[rs, sl]], axis=0)
        psis[s, c, pr] = jnp.where(bd, _state_dot(vx, ka, _TN), 0.0)
        yield 60
    for s in range(ns):
        sps = [st_ref[s, pr] if carry_state else pair_tile(s0_ref, s, pr) for pr in range(N_PAIRS)]
        for c in range(nck):
            rs = rows_of(s, c)
            for pr in range(N_PAIRS):
                yacc_ref[rs, psl(pr)] = _state_dot(rp_ref[rs, psl(pr)], sps[pr], _NT) + y2_ref[rs, psl(pr)]
            yield 100
            nxt = []
            for pr in range(N_PAIRS):
                nxt.append(sps[pr] * wt_ref[rs.start:rs.start + 1, psl(pr)]
                           + (_state_dot(sps[pr], thetas[s, c, pr]) + psis[s, c, pr]))
                yield 50
            sps = nxt
        for pr in range(N_PAIRS):
            if carry_state:
                st_ref[s, pr] = sps[pr]
            else:
                sout_ref[s, 2 * pr] = sps[pr][:HEAD, :HEAD]
                sout_ref[s, 2 * pr + 1] = sps[pr][HEAD:, HEAD:]
        yield 50

    y = yacc_ref[...]
    mean = seg_sum(y) * (1.0 / HEAD)
    yc = y - mean
    yield 400
    var = seg_sum(yc * yc) * (1.0 / HEAD)
    yn = yc * lax.rsqrt(var + GN_EPS) * lnw_ref[...] + lnb_ref[...]
    ya = (yn + bonus) * g
    y_ref[:, :, :D_A] = ya.reshape(ns, tb, D_A)
    yield 600


def _mixer_kernel(proj_ref, pprev_ref, conv0_ref, s0_ref, lru0_ref, *refs, ns, tb, chunk, t_valid, carry_state, streams):
    n_par = 20
    params = refs[:n_par]
    y_ref, sout_ref, lruo_ref, convo_ref, prev_ref, hst_ref, st_ref = refs[n_par:n_par + 7]
    slabs = refs[n_par + 7:]
    t = pl.program_id(1)

    @pl.when(t == 0)
    def _():
        prev_ref[...] = jnp.zeros_like(prev_ref)
        prev_ref[:, SUBLANES - 1:, :C_A] = pprev_ref[...]
        prev_ref[:, SUBLANES - (CONV_W - 1):, C_A:] = conv0_ref[...]
        hst_ref[...] = jnp.broadcast_to(lru0_ref[...], hst_ref.shape)
        if carry_state:
            for q in range(streams):
                for pr in range(N_PAIRS):
                    st_ref[q, pr] = _pair_tile(s0_ref, q, pr)

    def make(q):
        sq = pl.ds(q * ns, ns)
        return _mixer_stream(
            proj_ref.at[sq], s0_ref.at[sq], *params, y_ref.at[sq], sout_ref.at[sq],
            prev_ref.at[sq], hst_ref.at[sq], st_ref.at[pl.ds(q, 1)], *[r.at[q] for r in slabs],
            ns=ns, tb=tb, chunk=chunk, t_valid=t_valid, carry_state=carry_state)

    gens = [make(q) for q in range(streams)]
    clock = [0.0] * streams
    live = list(range(streams))
    while live:
        q = min(live, key=lambda i: clock[i])
        cost = next(gens[q], None)
        if cost is None:
            live.remove(q)
        else:
            clock[q] += cost

    @pl.when(t == pl.num_programs(1) - 1)
    def _():
        lruo_ref[...] = hst_ref[...]
        convo_ref[...] = proj_ref[:, tb - SUBLANES:, C_A:C_A + D_B]
        if carry_state:
            for q in range(streams):
                for pr in range(N_PAIRS):
                    tile = st_ref[q, pr]
                    sout_ref[q, 2 * pr] = tile[:HEAD, :HEAD]
                    sout_ref[q, 2 * pr + 1] = tile[HEAD:, HEAD:]


def _mixer(proj, p_prev, conv0, s0, lru0, prm, ns, tb, chunk, t_valid, streams=1):
    S, T, _ = proj.shape
    R = ns * tb
    assert R == ROWS
    carry_state = T // tb > 1
    assert ns == 1 or not carry_state
    kern = functools.partial(_mixer_kernel, ns=ns, tb=tb, chunk=chunk, t_valid=t_valid, carry_state=carry_state,
                             streams=streams)
    nb = ns * streams

    def full(a):
        nd = a.ndim
        return pl.BlockSpec(a.shape, lambda s, t, nd=nd: (0,) * nd)

    params = [prm[n] for n in ('mu', 'w0', 'ww2', 'a0', 'wa2', 'wg2', 'k_k', 'k_a', 'r_k', 'lnx_w', 'lnx_b',
                               'seg', 'segt', 'conv_w', 'conv_b', 'wga', 'bga', 'wgi', 'bgi', 'lam')]
    slab = lambda w: pltpu.VMEM((streams, R, w), F32)
    return pl.pallas_call(
        kern,
        out_shape=(jax.ShapeDtypeStruct((S, T, D_A + D_B), F32),
                   jax.ShapeDtypeStruct((S, N_HEADS, HEAD, HEAD), F32),
                   jax.ShapeDtypeStruct((S, SUBLANES, D_B), F32),
                   jax.ShapeDtypeStruct((S, SUBLANES, D_B), F32)),
        grid=(S // nb, T // tb),
        in_specs=[pl.BlockSpec((nb, tb, C_TOT), lambda s, t: (s, t, 0)),
                  pl.BlockSpec((nb, 1, C_A), lambda s, t: (s, 0, 0)),
                  pl.BlockSpec((nb, CONV_W - 1, D_B), lambda s, t: (s, 0, 0)),
                  pl.BlockSpec((nb, N_HEADS, HEAD, HEAD), lambda s, t: (s, 0, 0, 0),
                               pipeline_mode=pl.Buffered(1)),
                  pl.BlockSpec((nb, 1, D_B), lambda s, t: (s, 0, 0))]
                 + [full(a) for a in params],
        out_specs=(pl.BlockSpec((nb, tb, D_A + D_B), lambda s, t: (s, t, 0)),
                   pl.BlockSpec((nb, N_HEADS, HEAD, HEAD), lambda s, t: (s, 0, 0, 0),
                                pipeline_mode=pl.Buffered(1)),
                   pl.BlockSpec((nb, SUBLANES, D_B), lambda s, t: (s, 0, 0)),
                   pl.BlockSpec((nb, SUBLANES, D_B), lambda s, t: (s, 0, 0))),
        scratch_shapes=[pltpu.VMEM((nb, SUBLANES, C_A + D_B), F32),
                        pltpu.VMEM((nb, SUBLANES, D_B), F32),
                        pltpu.VMEM((streams, N_PAIRS, LANES, LANES), F32)]
                       + [slab(D_A) for _ in range(9)],
        compiler_params=_cparams(("arbitrary", "arbitrary")),
        name="mixer",
    )(proj, p_prev, conv0, s0, lru0, *params)


def _layernorm(x, w, b):
    mean = jnp.mean(x, axis=-1, keepdims=True)
    xc = x - mean
    var = jnp.mean(xc * xc, axis=-1, keepdims=True)
    return xc * lax.rsqrt(var + LN_EPS) * w + b


def _outproj_kernel(y_ref, x_ref, g1_ref, sc2_ref, sh2_ref, w_ref, lnw_ref, lnb_ref, x1_ref, h2_ref, *, alpha):
    ns, tb, _ = y_ref.shape
    mix = _dot1(y_ref[...].reshape(ns * tb, D_MODEL), w_ref[...]).reshape(ns, tb, D_MODEL)
    x1 = _layernorm(alpha * x_ref[...] + (1.0 + g1_ref[...]) * mix, lnw_ref[...], lnb_ref[...])
    x1_ref[...] = x1
    h2 = x1 * (1.0 + sc2_ref[...]) + sh2_ref[...]
    h2_ref[...] = h2.reshape(ns * tb, D_MODEL).astype(BF16)


def _outproj(y, x, mod, w_bf16, ln_w, ln_b, ns, tb, alpha):
    S, T, _ = x.shape
    nt = T // tb
    blk = pl.BlockSpec((ns, tb, D_MODEL), lambda s, t: (s, t, 0))
    modspec = lambda i: pl.BlockSpec((ns, 1, D_MODEL), lambda s, t, i=i: (s, 0, i))
    vec = pl.BlockSpec((1, D_MODEL), lambda s, t: (0, 0))
    return pl.pallas_call(
        functools.partial(_outproj_kernel, alpha=alpha),
        out_shape=(jax.ShapeDtypeStruct((S, T, D_MODEL), F32), jax.ShapeDtypeStruct((S * T, D_MODEL), BF16)),
        grid=(S // ns, nt),
        in_specs=[blk, blk, modspec(2), modspec(4), modspec(3),
                  pl.BlockSpec((D_MODEL, D_MODEL), lambda s, t: (0, 0)), vec, vec],
        out_specs=(blk, pl.BlockSpec((ns * tb, D_MODEL), lambda s, t: (s * nt + t, 0))),
        compiler_params=_cparams(("arbitrary", "arbitrary")),
        name="outproj",
    )(y, x, mod, mod, mod, w_bf16, ln_w.reshape(1, -1), ln_b.reshape(1, -1))


def _topk_rows(s, ridx, n_rows):
    out_i = _iota((TOPK, s.shape[1]), 0)
    vals = jnp.zeros((TOPK, s.shape[1]), F32)
    idxs = jnp.zeros((TOPK, s.shape[1]), F32)
    for it in range(TOPK):
        m = jnp.max(s, axis=0, keepdims=True)
        idx = jnp.min(jnp.where(s == m, ridx, float(n_rows)), axis=0, keepdims=True)
        vals = jnp.where(out_i == it, m, vals)
        idxs = jnp.where(out_i == it, idx, idxs)
        s = jnp.where(ridx == idx, -jnp.inf, s)
    return vals, idxs


def _route_kernel(h_ref, wq_ref, sk_ref, g_ref, gate_ref, e1_ref, e2_ref):
    R = ROWS
    q = _dot1(h_ref[...], wq_ref[...])
    sk0 = sk_ref[0]
    sk1 = sk_ref[1]
    gates, e1s, e2s = [], [], []
    key_rows = _iota((N_KEYS, R), 0).astype(F32)
    r16 = _iota((TOPK, R), 0).astype(F32)
    r8 = _iota((SUBLANES, R), 0).astype(F32)
    ea_rank = jnp.where(r8 < 3, 2.0, jnp.where(r8 < 5, 3.0, 4.0))
    eb_rank = jnp.where((r8 == 0) | (r8 == 3) | (r8 == 5), 2.0, jnp.where((r8 == 1) | (r8 == 4), 3.0, 4.0))
    flat = jnp.concatenate([r16, TOPK + r8, TOPK * r16, TOPK * r8 + 1.0, TOPK * ea_rank + eb_rank], axis=0)
    neg = -jnp.inf
    for hd in range(P_HEADS):
        base = hd * 2 * D_QH
        s1 = _dot1(sk0, q[:, base:base + D_QH], _NT)
        s2 = _dot1(sk1, q[:, base + D_QH:base + 2 * D_QH], _NT)
        v1, i1 = _topk_rows(s1, key_rows, N_KEYS)
        v2, i2 = _topk_rows(s2, key_rows, N_KEYS)
        ea = jnp.where(r8 < 3, v1[2:3], jnp.where(r8 < 5, v1[3:4], v1[4:5]))
        eb = jnp.where(eb_rank == 2.0, v2[2:3], jnp.where(eb_rank == 3.0, v2[3:4], v2[4:5]))
        cand = jnp.concatenate([
            v1[0:1] + v2,
            v1[1:2] + v2[:SUBLANES],
            jnp.where(r16 >= 2, v1 + v2[0:1], neg),
            jnp.where(r8 >= 2, v1[:SUBLANES] + v2[1:2], neg),
            jnp.where(r8 < 6, ea + eb, neg),
        ], axis=0)
        sv, ci = _topk_rows(cand, flat, TOPK * TOPK)
        ca = jnp.floor(ci * (1.0 / TOPK))
        cb = ci - TOPK * ca
        e1 = jnp.zeros((TOPK, R), F32)
        e2 = jnp.zeros((TOPK, R), F32)
        for a in range(TOPK):
            e1 = jnp.where(ca == float(a), i1[a:a + 1, :], e1)
            e2 = jnp.where(cb == float(a), i2[a:a + 1, :], e2)
        ex = jnp.exp(sv - jnp.max(sv, axis=0, keepdims=True))
        gates.append(ex / jnp.sum(ex, axis=0, keepdims=True))
        e1s.append(e1)
        e2s.append(e2)
    gate_ref[...] = jnp.concatenate(gates, axis=0).T
    e1_ref[...] = jnp.concatenate(e1s, axis=0).T
    e2_ref[...] = jnp.concatenate(e2s, axis=0).T
    key_i = _iota((N_KEYS, P_HEADS * TOPK), 0).astype(F32)

    def per_octet(o, carry):
        base = pl.multiple_of(o * SUBLANES, SUBLANES)
        tiles = []
        for j in range(SUBLANES):
            sel1 = key_i == e1_ref[pl.ds(base + j, 1), :]
            sel2 = key_i == e2_ref[pl.ds(base + j, 1), :]
            m1 = jnp.where(sel1, gate_ref[pl.ds(base + j, 1), :], 0.0)
            m2 = jnp.where(sel2, 1.0, 0.0)
            tiles.append(_dot1(m1, m2, _NT))
        by_key = jnp.swapaxes(jnp.stack(tiles, axis=0), 0, 1)
        for i1v in range(N_KEYS):
            g_ref[pl.ds(base, SUBLANES), i1v * N_KEYS:(i1v + 1) * N_KEYS] = by_key[i1v]
        return carry

    lax.fori_loop(0, R // SUBLANES, per_octet, 0, unroll=8)


def _route(h2, wq_bf16, sub_keys):
    n = h2.shape[0]
    assert n % ROWS == 0
    return pl.pallas_call(
        _route_kernel,
        out_shape=jax.ShapeDtypeStruct((n, N_EXPERTS), F32),
        grid=(n // ROWS,),
        in_specs=[pl.BlockSpec((ROWS, D_MODEL), lambda i: (i, 0)),
                  pl.BlockSpec((D_MODEL, D_MODEL), lambda i: (0, 0)),
                  pl.BlockSpec((2, N_KEYS, D_QH), lambda i: (0, 0, 0))],
        out_specs=pl.BlockSpec((ROWS, N_EXPERTS), lambda i: (i, 0)),
        scratch_shapes=[pltpu.VMEM((ROWS, P_HEADS * TOPK), F32) for _ in range(3)],
        compiler_params=_cparams(("arbitrary",)),
        name="route",
    )(h2, wq_bf16, sub_keys)


EXPERT_CHUNK = 512
EXPERT_ROWS_MAX = 1152


EXPERT_SPLIT = 4


def _experts_kernel(h_ref, g_ref, u_ref, v_ref, o_ref):
    @pl.when(pl.program_id(1) == 0)
    def _():
        o_ref[...] = jnp.zeros_like(o_ref)

    u = u_ref[...].astype(BF16)
    v = v_ref[...].astype(BF16)
    rb = h_ref.shape[0] // EXPERT_SPLIT
    rows = [pl.ds(b * rb, rb) for b in range(EXPERT_SPLIT)]
    act = _dg(h_ref[rows[0], :], u, _NT)
    for b in range(EXPERT_SPLIT):
        nxt = _dg(h_ref[rows[b + 1], :], u, _NT) if b + 1 < EXPERT_SPLIT else None
        w = (g_ref[rows[b], :] * _gelu(act)).astype(BF16)
        o_ref[rows[b], :] += _dg(w, v, _NN)
        act = nxt


def _token_block(n, cap):
    for tb in range(min(n, cap), 0, -1):
        if n % tb == 0 and tb % (2 * SUBLANES * EXPERT_SPLIT) == 0:
            return tb
    raise ValueError(n)


def _experts(h2, gmat, peer_u, peer_v):
    n = h2.shape[0]
    tb = _token_block(n, EXPERT_ROWS_MAX)
    ec = EXPERT_CHUNK
    return pl.pallas_call(
        _experts_kernel,
        out_shape=jax.ShapeDtypeStruct((n, D_MODEL), F32),
        grid=(n // tb, N_EXPERTS // ec),
        in_specs=[pl.BlockSpec((tb, D_MODEL), lambda i, e: (i, 0)),
                  pl.BlockSpec((tb, ec), lambda i, e: (i, e)),
                  pl.BlockSpec((ec, D_MODEL), lambda i, e: (e, 0)),
                  pl.BlockSpec((ec, D_MODEL), lambda i, e: (e, 0))],
        out_specs=pl.BlockSpec((tb, D_MODEL), lambda i, e: (i, 0), pipeline_mode=pl.Buffered(1)),
        compiler_params=_cparams(("arbitrary", "arbitrary")),
        name="experts",
    )(h2, gmat, peer_u, peer_v)


def _ln2_kernel(x1_ref, ff_ref, g2_ref, lnw_ref, lnb_ref, o_ref, *, alpha):
    ff = ff_ref[...].reshape(x1_ref.shape)
    o_ref[...] = _layernorm(alpha * x1_ref[...] + (1.0 + g2_ref[...]) * ff, lnw_ref[...], lnb_ref[...])


def _ln2(x1, ff, mod, ln_w, ln_b, ns, tb, alpha):
    S, T, _ = x1.shape
    nt = T // tb
    blk = pl.BlockSpec((ns, tb, D_MODEL), lambda s, t: (s, t, 0))
    vec = pl.BlockSpec((1, D_MODEL), lambda s, t: (0, 0))
    if ff.ndim == 2:
        assert ns == 1
        ffspec = pl.BlockSpec((tb, D_MODEL), lambda s, t: (s * nt + t, 0))
    else:
        ffspec = blk
    return pl.pallas_call(
        functools.partial(_ln2_kernel, alpha=alpha),
        out_shape=jax.ShapeDtypeStruct((S, T, D_MODEL), F32),
        grid=(S // ns, nt),
        in_specs=[blk, ffspec, pl.BlockSpec((ns, 1, D_MODEL), lambda s, t: (s, 0, 5)), vec, vec],
        out_specs=blk,
        compiler_params=_cparams(("arbitrary", "arbitrary")),
        name="ln2",
    )(x1, ff, mod, ln_w.reshape(1, -1), ln_b.reshape(1, -1))


def _pair_blockdiag(w):
    n = w.shape[0] // 2
    w = w.reshape(n, 2, HEAD, HEAD)
    z = jnp.zeros((n, HEAD, HEAD), w.dtype)
    top = jnp.concatenate([w[:, 0], z], axis=2)
    bot = jnp.concatenate([z, w[:, 1]], axis=2)
    return jnp.concatenate([top, bot], axis=1)


def _mix_group(x, mod, p_prev, wkv0, lru0, conv0, prm, *, ns_mix, tb_mix, chunk, t_valid, ns_big, tb_big, alpha,
               streams=1):
    S, T, _ = x.shape
    proj, h_last = _inproj(x, mod, prm['w_in'], ns_big, tb_big)
    y, wkv, lru_o, conv_o = _mixer(proj, p_prev[:, None, :], conv0, wkv0, lru0[:, None, :], prm,
                                   ns_mix, tb_mix, chunk, t_valid, streams)
    x1, h2 = _outproj(y, x, mod, prm['w_out'], prm['ln1_w'], prm['ln1_b'], ns_big, tb_big, alpha)
    tv = (t_valid - 1) % SUBLANES
    shift = h_last[:, tv]
    lru = lru_o[:, SUBLANES - 1]
    conv = conv_o[:, tv - (CONV_W - 2):tv + 1]
    return x1, h2, (shift, wkv, lru, conv)


def kernel(x_prompt, x_sample, c_prompt, c_sample, state_shift, state_wkv, state_lru, state_conv, w_ada, b_ada, w_in, mu_shift, w0, w_w2, a0, w_a2, w_g2, k_k, k_a, r_k, lnx_w, lnx_b, conv_w, conv_b, w_gate_a, b_gate_a, w_gate_i, b_gate_i, lru_lambda, w_out, ln1_w, ln1_b, w_q, sub_keys, peer_u, peer_v, ln2_w, ln2_b):
    depth = w_ada.shape[0]
    alpha = (2 * depth) ** 0.25
    bp, tp, _ = x_prompt.shape
    bs, ts, _ = x_sample.shape
    yp = x_prompt
    ys = jnp.pad(x_sample, ((0, 0), (0, SAMPLE_TPAD - ts), (0, 0)))
    head_of = jnp.arange(D_A) // HEAD
    seg = (head_of[:, None] == jnp.arange(LANES)[None, :]).astype(BF16)
    row = lambda a: a.reshape(1, -1)
    outs = [[] for _ in range(8)]
    for l in range(depth):
        zpad = jnp.zeros((LANES - R_W, D_A), F32)
        prm = {
            'w_in': w_in[l].astype(BF16), 'w_out': w_out[l].astype(BF16), 'w_q': w_q[l].astype(BF16),
            'sub_keys': sub_keys[l],
            'mu': row(mu_shift[l]), 'w0': row(w0[l]),
            'ww2': jnp.concatenate([w_w2[l], zpad], axis=0), 'a0': row(a0[l]),
            'wa2': jnp.concatenate([zpad, w_a2[l]], axis=0), 'wg2': w_g2[l].astype(BF16),
            'k_k': row(k_k[l]), 'k_a': row(k_a[l]), 'r_k': row(r_k[l]),
            'lnx_w': row(lnx_w[l]), 'lnx_b': row(lnx_b[l]), 'seg': seg, 'segt': seg.T,
            'conv_w': conv_w[l], 'conv_b': row(conv_b[l]),
            'wga': _pair_blockdiag(w_gate_a[l]).astype(BF16), 'bga': row(b_gate_a[l]),
            'wgi': _pair_blockdiag(w_gate_i[l]).astype(BF16), 'bgi': row(b_gate_i[l]),
            'lam': row(lru_lambda[l]),
            'ln1_w': ln1_w[l], 'ln1_b': ln1_b[l], 'ln2_w': ln2_w[l], 'ln2_b': ln2_b[l],
        }
        c_all = jnp.concatenate([c_prompt, c_sample], axis=0)
        mod = _ada(c_all, w_ada[l], b_ada[l])
        mod_p = mod[:bp, None, :]
        mod_s = mod[bp:, None, :]
        zmod = jnp.zeros((bs // SUBLANES, 1, 6 * D_MODEL), F32)
        pprev, _ = _inproj(state_shift[l].reshape(bs // SUBLANES, SUBLANES, D_MODEL), zmod, prm['w_in'], 2, SUBLANES)
        pprev = pprev.reshape(bs, C_TOT)[:, :C_A]
        x1p, h2p, st_p = _mix_group(
            yp, mod_p, jnp.zeros((bp, C_A), F32), jnp.zeros((bp, N_HEADS, HEAD, HEAD), F32),
            jnp.zeros((bp, D_B), F32), jnp.zeros((bp, CONV_W - 1, D_B), F32), prm,
            ns_mix=1, tb_mix=ROWS, chunk=16, t_valid=ROWS, ns_big=1, tb_big=min(512, tp), alpha=alpha,
            streams=2 if bp % 2 == 0 else 1)
        x1s, h2s, st_s = _mix_group(
            ys, mod_s, pprev, state_wkv[l], state_lru[l], state_conv[l], prm,
            ns_mix=ROWS // SAMPLE_TPAD, tb_mix=SAMPLE_TPAD, chunk=SAMPLE_TPAD, t_valid=ts,
            ns_big=min(64, bs), tb_big=SAMPLE_TPAD, alpha=alpha)
        h2s = h2s.reshape(bs, SAMPLE_TPAD, D_MODEL)[:, :ts].reshape(bs * ts, D_MODEL)
        h2 = jnp.concatenate([h2p, h2s], axis=0)
        gmat = _route(h2, prm['w_q'], prm['sub_keys'])
        ff = _experts(h2, gmat, peer_u[l], peer_v[l])
        yp = _ln2(x1p, ff, mod_p, ln2_w[l], ln2_b[l], 1, min(512, tp), alpha)
        ys_real = _ln2(x1s[:, :ts], ff[bp * tp:].reshape(bs, ts, D_MODEL), mod_s, ln2_w[l], ln2_b[l],
                       min(64, bs), ts, alpha)
        ys = jnp.pad(ys_real, ((0, 0), (0, SAMPLE_TPAD - ts), (0, 0)))
        for i, a in enumerate(st_p + st_s):
            outs[i].append(a)
    return (yp, ys[:, :ts]) + tuple(jnp.stack(o) for o in outs)
```

```python
import functools

import jax
import jax.numpy as jnp
from jax import lax
from jax.experimental import pallas as pl
from jax.experimental.pallas import tpu as pltpu

F32 = jnp.float32
BF16 = jnp.bfloat16

D_MODEL = 2048
D_A = 1024
D_B = 1024
HEAD = 64
N_HEADS = D_A // HEAD
N_PAIRS = N_HEADS // 2
R_W = 64
R_A = 64
R_G = 128
C_A = 3 * D_A + R_W + R_A + R_G
C_TOT = C_A + 2 * D_B
CONV_W = 4
C_LRU = 8.0
P_HEADS = 8
N_KEYS = 128
N_EXPERTS = N_KEYS * N_KEYS
TOPK = 16
D_QH = 128
LN_EPS = 1e-5
GN_EPS = 64e-5

LANES = 128
SUBLANES = 8
ROWS = 128
SAMPLE_TPAD = 8
VMEM_LIMIT = 56 * 1024 * 1024


def _cparams(sem):
    return pltpu.CompilerParams(dimension_semantics=sem, vmem_limit_bytes=VMEM_LIMIT)


def _split2(x):
    hi = x.astype(BF16)
    lo = (x - hi.astype(F32)).astype(BF16)
    return hi, lo


_NN = (((1,), (0,)), ((), ()))
_NT = (((1,), (1,)), ((), ()))
_TN = (((0,), (0,)), ((), ()))


def _dg(a, b, dims):
    return lax.dot_general(a, b, dims, preferred_element_type=F32)


def _dot1(a, b, dims=_NN):
    return _dg(a.astype(BF16), b.astype(BF16), dims)


def _dot3(a, b, dims=_NN):
    ah, al = _split2(a)
    bh, bl = _split2(b)
    return _dg(ah, bh, dims) + (_dg(ah, bl, dims) + _dg(al, bh, dims))


_state_dot = _dot1


def _dot_exact_rhs(a, b_exact, dims=_NN):
    a1 = a.astype(BF16)
    r1 = a - a1.astype(F32)
    a2 = r1.astype(BF16)
    a3 = (r1 - a2.astype(F32)).astype(BF16)
    return _dg(a1, b_exact, dims) + (_dg(a2, b_exact, dims) + _dg(a3, b_exact, dims))


def _dot_exact_lhs(a_exact, b, dims=_NN):
    b1 = b.astype(BF16)
    r1 = b - b1.astype(F32)
    b2 = r1.astype(BF16)
    b3 = (r1 - b2.astype(F32)).astype(BF16)
    return _dg(a_exact, b1, dims) + (_dg(a_exact, b2, dims) + _dg(a_exact, b3, dims))


def _sigmoid(x):
    return 1.0 / (1.0 + jnp.exp(-x))


def _softplus(x):
    return jnp.maximum(x, 0.0) + jnp.log(1.0 + jnp.exp(-jnp.abs(x)))


def _gelu(x):
    return 0.5 * x * (1.0 + jnp.tanh(0.7978845608028654 * (x + 0.044715 * (x * x * x))))


def _iota(shape, dim):
    return lax.broadcasted_iota(jnp.int32, shape, dim)


def _ada_kernel(c_ref, w_ref, b_ref, o_ref):
    c = c_ref[...]
    s = c * _sigmoid(c)
    o_ref[...] = _dot3(s, w_ref[...]) + b_ref[...]


def _ada(c, w_ada, b_ada):
    n = c.shape[0]
    tn = 1024
    return pl.pallas_call(
        _ada_kernel,
        out_shape=jax.ShapeDtypeStruct((n, 6 * D_MODEL), F32),
        grid=(6 * D_MODEL // tn,),
        in_specs=[pl.BlockSpec((n, D_MODEL), lambda j: (0, 0)),
                  pl.BlockSpec((D_MODEL, tn), lambda j: (0, j)),
                  pl.BlockSpec((1, tn), lambda j: (0, j))],
        out_specs=pl.BlockSpec((n, tn), lambda j: (0, j)),
        compiler_params=_cparams(("arbitrary",)),
        name="ada",
    )(c, w_ada, b_ada.reshape(1, -1))


def _inproj_kernel(x_ref, sc_ref, sh_ref, w_ref, o_ref, h_ref):
    h = x_ref[...] * (1.0 + sc_ref[...]) + sh_ref[...]
    ns, tb, _ = h.shape
    h_ref[...] = h[:, tb - SUBLANES:, :]
    p = _dot1(h.reshape(ns * tb, D_MODEL), w_ref[...])
    o_ref[...] = p.reshape(ns, tb, p.shape[-1])


def _inproj(x, mod, w_bf16, ns, tb):
    S, T, _ = x.shape
    ncol = w_bf16.shape[1]
    tn = ncol // 3
    proj, h_last = pl.pallas_call(
        _inproj_kernel,
        out_shape=(jax.ShapeDtypeStruct((S, T, ncol), F32),
                   jax.ShapeDtypeStruct((3, S, SUBLANES, D_MODEL), F32)),
        grid=(3, S // ns, T // tb),
        in_specs=[pl.BlockSpec((ns, tb, D_MODEL), lambda j, s, t: (s, t, 0)),
                  pl.BlockSpec((ns, 1, D_MODEL), lambda j, s, t: (s, 0, 1)),
                  pl.BlockSpec((ns, 1, D_MODEL), lambda j, s, t: (s, 0, 0)),
                  pl.BlockSpec((D_MODEL, tn), lambda j, s, t: (0, j))],
        out_specs=(pl.BlockSpec((ns, tb, tn), lambda j, s, t: (s, t, j)),
                   pl.BlockSpec((None, ns, SUBLANES, D_MODEL), lambda j, s, t: (j, s, 0, 0))),
        compiler_params=_cparams(("arbitrary", "arbitrary", "arbitrary")),
        name="inproj",
    )(x, mod, mod, w_bf16)
    return proj, h_last[0]


def _pair_tile(ref, s, pr):
    zero_blk = jnp.zeros((HEAD, HEAD), F32)
    top = jnp.concatenate([ref[s, 2 * pr], zero_blk], axis=1)
    bot = jnp.concatenate([zero_blk, ref[s, 2 * pr + 1]], axis=1)
    return jnp.concatenate([top, bot], axis=0)


def _mixer_stream(proj_ref, s0_ref,
                  mu_ref, w0_ref, ww2_ref, a0_ref, wa2_ref, wg2_ref, kk_ref, ka_ref, rk_ref,
                  lnw_ref, lnb_ref, seg_ref, segt_ref,
                  cw_ref, cb_ref, wga_ref, bga_ref, wgi_ref, bgi_ref, lam_ref,
                  y_ref, sout_ref,
                  prev_ref, hst_ref, st_ref,
                  bp_ref, rp_ref, x2_ref, y2_ref, v_ref, kb_ref, ab_ref, wt_ref, yacc_ref,
                  *, ns, tb, chunk, t_valid, carry_state):
    R = ns * tb
    pair_tile = _pair_tile

    cur = proj_ref[...]
    ext = jnp.concatenate([prev_ref[...], cur[:, :, :C_A + D_B]], axis=1)
    prev_ref[...] = cur[:, tb - SUBLANES:, :C_A + D_B]

    row = _iota((R, 1), 0)
    rowin = row % tb
    valid = rowin < t_valid

    pa = cur[:, :, :C_A].reshape(R, C_A)
    shifted = ext[:, SUBLANES - 1:SUBLANES - 1 + tb, :C_A].reshape(R, C_A)
    p = pa + (shifted - pa) * mu_ref[...]
    r = p[:, :D_A]
    k = p[:, D_A:2 * D_A]
    v = p[:, 2 * D_A:3 * D_A]
    xwa = p[:, 3 * D_A:3 * D_A + LANES]
    xg = p[:, 3 * D_A + LANES:]
    w_log = -_softplus(-(w0_ref[...] + _dot3(jnp.tanh(xwa), ww2_ref[...]))) - 0.5
    logw = -jnp.exp(w_log)
    a = _sigmoid(a0_ref[...] + _dot3(xwa, wa2_ref[...]))
    g = _dot1(_sigmoid(xg), wg2_ref[...])
    yield 900
    seg = seg_ref[...]
    segt = segt_ref[...]

    def seg_sum(x):
        return _dot_exact_rhs(_dot_exact_rhs(x, seg), segt)

    kk = k * kk_ref[...]
    kk = kk / jnp.maximum(jnp.sqrt(seg_sum(kk * kk)), 1e-12)
    yield 600
    k = k * (1.0 + (a - 1.0) * ka_ref[...])
    kka = kk * a
    bonus = seg_sum(r * k * rk_ref[...]) * v
    yield 600
    logw = jnp.where(valid, logw, 0.0)
    r = jnp.where(valid, r, 0.0)
    k = jnp.where(valid, k, 0.0)
    v = jnp.where(valid, v, 0.0)
    kk = jnp.where(valid, kk, 0.0)
    kka = jnp.where(valid, kka, 0.0)

    ri = _iota((R, R), 0)
    ci = _iota((R, R), 1)
    same = (ri // chunk) == (ci // chunk)
    incl = same & (ci <= ri)
    strict = same & (ci < ri)
    lmat = jnp.concatenate([jnp.where(incl, 1.0, 0.0), jnp.where(same, 1.0, 0.0)], axis=0).astype(BF16)
    cums = _dot_exact_lhs(lmat, logw)
    cum = cums[:R]
    tot = cums[R:]
    yield 500
    e_prev = jnp.exp(cum - logw)
    e_neg = jnp.exp(-cum)
    e_pos = jnp.exp(cum)
    e_rem = jnp.exp(tot - cum)
    beta = kk * e_prev
    alpha = kka * e_neg
    kappa = k * e_neg
    rho = r * e_pos
    yield 600
    v_ref[...] = v
    kb_ref[...] = k * e_rem
    ab_ref[...] = -(kka * e_rem)
    wt_ref[...] = jnp.exp(tot)
    yield 400

    lane = _iota((1, LANES), 1)
    masks = (lane < HEAD, lane >= HEAD)
    eye = jnp.where(ri == ci, 1.0, 0.0)
    nsq = {16: 3, 8: 2}[chunk]
    heads = [(pr, m) for pr in range(N_PAIRS) for m in masks]
    psl = lambda pr: slice(pr * LANES, (pr + 1) * LANES)
    bms = [jnp.where(m, beta[:, psl(pr)], 0.0) for pr, m in heads]
    rms = [jnp.where(m, rho[:, psl(pr)], 0.0) for pr, m in heads]
    yield 300
    gms = []
    for (pr, m), bm, rm in zip(heads, bms, rms):
        gms.append(_dot1(jnp.concatenate([bm, rm], axis=0),
                         jnp.concatenate([alpha[:, psl(pr)], kappa[:, psl(pr)]], axis=0), _NT))
        yield 70
    l_bas = [jnp.where(strict, gm[:R, :R], 0.0) for gm in gms]
    m_ras = [jnp.where(incl, gm[R:, :R], 0.0) for gm in gms]
    xy1s = []
    for (pr, m), gm in zip(heads, gms):
        xy1s.append(_dot1(jnp.concatenate([jnp.where(strict, gm[:R, R:], 0.0),
                                           jnp.where(incl, gm[R:, R:], 0.0)], axis=0), v[:, psl(pr)]))
        yield 70
    pws = []
    for l in l_bas:
        pws.append(_dot1(l, l))
        yield 40

    def lru_steps():
        for pr in range(N_PAIRS):
            sl = psl(pr)
            bsl = slice(C_A + pr * LANES, C_A + (pr + 1) * LANES)
            gsl = slice(C_A + D_B + pr * LANES, C_A + D_B + (pr + 1) * LANES)
            xc = cb_ref[:, sl]
            for j in range(CONV_W):
                off = SUBLANES - (CONV_W - 1) + j
                xc = xc + ext[:, off:off + tb, bsl].reshape(R, LANES) * cw_ref[j:j + 1, sl]
            yield
            rg = _sigmoid(_dot1(xc, wga_ref[pr]) + bga_ref[:, sl])
            ig = _sigmoid(_dot1(xc, wgi_ref[pr]) + bgi_ref[:, sl])
            yield
            log_a = -C_LRU * rg * _softplus(-lam_ref[:, sl])
            av = jnp.where(valid, jnp.exp(log_a), 1.0)
            bv = jnp.where(valid, jnp.sqrt(1.0 - jnp.exp(2.0 * log_a)) * (ig * xc), 0.0)
            yield
            d = 1
            while d < tb:
                take = rowin >= d
                a_sh = jnp.where(take, pltpu.roll(av, d, axis=0), 1.0)
                b_sh = jnp.where(take, pltpu.roll(bv, d, axis=0), 0.0)
                bv = av * b_sh + bv
                av = av * a_sh
                d *= 2
                yield
            h0 = jnp.broadcast_to(hst_ref[:, SUBLANES - 1:SUBLANES, sl], (ns, tb, LANES)).reshape(R, LANES)
            h3 = (av * h0 + bv).reshape(ns, tb, LANES)
            hst_ref[:, :, sl] = h3[:, tb - SUBLANES:, :]
            gb = cur[:, :, gsl].reshape(R, LANES)
            y_ref[:, :, D_A + pr * LANES:D_A + (pr + 1) * LANES] = h3 * _gelu(gb).reshape(ns, tb, LANES)
            yield

    lru = lru_steps()
    minvs = [eye - l for l in l_bas]
    for i in range(nsq):
        nxt = []
        for mi, pw in zip(minvs, pws):
            nxt.append(mi + _dot1(mi, pw))
            next(lru, None)
            yield 60
        minvs = nxt
        if i + 1 < nsq:
            nxt = []
            for pw in pws:
                nxt.append(_dot1(pw, pw))
                next(lru, None)
                yield 60
            pws = nxt
    for _ in lru:
        yield 40
    t1s = []
    for mi, xy1, bm in zip(minvs, xy1s, bms):
        t1s.append(_dot1(mi, jnp.concatenate([xy1[:R], bm], axis=1)))
        yield 50
    t2s = []
    for m_ra, t1 in zip(m_ras, t1s):
        t2s.append(_dot1(m_ra, t1))
        yield 50
    for pr in range(N_PAIRS):
        h0, h1 = 2 * pr, 2 * pr + 1
        sl = psl(pr)
        m1 = masks[1]
        bp_ref[:, sl] = t1s[h0][:, LANES:] + t1s[h1][:, LANES:]
        rp_ref[:, sl] = (rms[h0] - t2s[h0][:, LANES:]) + (rms[h1] - t2s[h1][:, LANES:])
        x2_ref[:, sl] = jnp.where(m1, t1s[h1][:, :LANES], t1s[h0][:, :LANES])
        y2_ref[:, sl] = jnp.where(m1, xy1s[h1][R:] - t2s[h1][:, :LANES], xy1s[h0][R:] - t2s[h0][:, :LANES])

    bi = _iota((LANES, LANES), 0)
    bj = _iota((LANES, LANES), 1)
    bd = (bi < HEAD) == (bj < HEAD)
    nck = tb // chunk
    sites = [(s, c, pr) for s in range(ns) for c in range(nck) for pr in range(N_PAIRS)]
    rows_of = lambda s, c: slice(s * tb + c * chunk, s * tb + (c + 1) * chunk)
    thetas = {}
    psis = {}
    for s, c, pr in sites:
        rs, sl = rows_of(s, c), psl(pr)
        thetas[s, c, pr] = jnp.where(bd, _state_dot(bp_ref[rs, sl], ab_ref[rs, sl], _TN), 0.0)
        vx = jnp.concatenate([v_ref[rs, sl], x2_ref[rs, sl]], axis=0)
        ka = jnp.concatenate([kb_ref[rs, sl], ab_ref[rs, sl]], axis=0)
        psis[s, c, pr] = jnp.where(bd, _state_dot(vx, ka, _TN), 0.0)
        yield 60
    for s in range(ns):
        sps = [st_ref[s, pr] if carry_state else pair_tile(s0_ref, s, pr) for pr in range(N_PAIRS)]
        for c in range(nck):
            rs = rows_of(s, c)
            for pr in range(N_PAIRS):
                yacc_ref[rs, psl(pr)] = _state_dot(rp_ref[rs, psl(pr)], sps[pr], _NT) + y2_ref[rs, psl(pr)]
            yield 100
            nxt = []
            for pr in range(N_PAIRS):
                nxt.append(sps[pr] * wt_ref[rs.start:rs.start + 1, psl(pr)]
                           + (_state_dot(sps[pr], thetas[s, c, pr]) + psis[s, c, pr]))
                yield 50
            sps = nxt
        for pr in range(N_PAIRS):
            if carry_state:
                st_ref[s, pr] = sps[pr]
            else:
                sout_ref[s, 2 * pr] = sps[pr][:HEAD, :HEAD]
                sout_ref[s, 2 * pr + 1] = sps[pr][HEAD:, HEAD:]
        yield 50

    y = yacc_ref[...]
    mean = seg_sum(y) * (1.0 / HEAD)
    yc = y - mean
    yield 400
    var = seg_sum(yc * yc) * (1.0 / HEAD)
    yn = yc * lax.rsqrt(var + GN_EPS) * lnw_ref[...] + lnb_ref[...]
    ya = (yn + bonus) * g
    y_ref[:, :, :D_A] = ya.reshape(ns, tb, D_A)
    yield 600


def _mixer_kernel(proj_ref, pprev_ref, conv0_ref, s0_ref, lru0_ref, *refs, ns, tb, chunk, t_valid, carry_state, streams):
    n_par = 20
    params = refs[:n_par]
    y_ref, sout_ref, lruo_ref, convo_ref, prev_ref, hst_ref, st_ref = refs[n_par:n_par + 7]
    slabs = refs[n_par + 7:]
    t = pl.program_id(1)

    @pl.when(t == 0)
    def _():
        prev_ref[...] = jnp.zeros_like(prev_ref)
        prev_ref[:, SUBLANES - 1:, :C_A] = pprev_ref[...]
        prev_ref[:, SUBLANES - (CONV_W - 1):, C_A:] = conv0_ref[...]
        hst_ref[...] = jnp.broadcast_to(lru0_ref[...], hst_ref.shape)
        if carry_state:
            for q in range(streams):
                for pr in range(N_PAIRS):
                    st_ref[q, pr] = _pair_tile(s0_ref, q, pr)

    def make(q):
        sq = pl.ds(q * ns, ns)
        return _mixer_stream(
            proj_ref.at[sq], s0_ref.at[sq], *params, y_ref.at[sq], sout_ref.at[sq],
            prev_ref.at[sq], hst_ref.at[sq], st_ref.at[pl.ds(q, 1)], *[r.at[q] for r in slabs],
            ns=ns, tb=tb, chunk=chunk, t_valid=t_valid, carry_state=carry_state)

    gens = [make(q) for q in range(streams)]
    clock = [0.0] * streams
    live = list(range(streams))
    while live:
        q = min(live, key=lambda i: clock[i])
        cost = next(gens[q], None)
        if cost is None:
            live.remove(q)
        else:
            clock[q] += cost

    @pl.when(t == pl.num_programs(1) - 1)
    def _():
        lruo_ref[...] = hst_ref[...]
        convo_ref[...] = proj_ref[:, tb - SUBLANES:, C_A:C_A + D_B]
        if carry_state:
            for q in range(streams):
                for pr in range(N_PAIRS):
                    tile = st_ref[q, pr]
                    sout_ref[q, 2 * pr] = tile[:HEAD, :HEAD]
                    sout_ref[q, 2 * pr + 1] = tile[HEAD:, HEAD:]


def _mixer(proj, p_prev, conv0, s0, lru0, prm, ns, tb, chunk, t_valid, streams=1):
    S, T = s0.shape[0], proj.shape[1]
    R = ns * tb
    assert R == ROWS
    carry_state = T // tb > 1
    assert ns == 1 or not carry_state
    kern = functools.partial(_mixer_kernel, ns=ns, tb=tb, chunk=chunk, t_valid=t_valid, carry_state=carry_state,
                             streams=streams)
    nb = ns * streams

    def full(a):
        nd = a.ndim
        return pl.BlockSpec(a.shape, lambda s, t, nd=nd: (0,) * nd)

    params = [prm[n] for n in ('mu', 'w0', 'ww2', 'a0', 'wa2', 'wg2', 'k_k', 'k_a', 'r_k', 'lnx_w', 'lnx_b',
                               'seg', 'segt', 'conv_w', 'conv_b', 'wga', 'bga', 'wgi', 'bgi', 'lam')]
    slab = lambda w: pltpu.VMEM((streams, R, w), F32)
    return pl.pallas_call(
        kern,
        out_shape=(jax.ShapeDtypeStruct((S, T, D_A + D_B), F32),
                   jax.ShapeDtypeStruct((S, N_HEADS, HEAD, HEAD), F32),
                   jax.ShapeDtypeStruct((S, SUBLANES, D_B), F32),
                   jax.ShapeDtypeStruct((S, SUBLANES, D_B), F32)),
        grid=(S // nb, T // tb),
        in_specs=[pl.BlockSpec((nb, tb, C_TOT), lambda s, t: (s, t, 0)),
                  pl.BlockSpec((nb, 1, C_A), lambda s, t: (s, 0, 0)),
                  pl.BlockSpec((nb, CONV_W - 1, D_B), lambda s, t: (s, 0, 0)),
                  pl.BlockSpec((nb, N_HEADS, HEAD, HEAD), lambda s, t: (s, 0, 0, 0),
                               pipeline_mode=pl.Buffered(1)),
                  pl.BlockSpec((nb, 1, D_B), lambda s, t: (s, 0, 0))]
                 + [full(a) for a in params],
        out_specs=(pl.BlockSpec((nb, tb, D_A + D_B), lambda s, t: (s, t, 0)),
                   pl.BlockSpec((nb, N_HEADS, HEAD, HEAD), lambda s, t: (s, 0, 0, 0),
                                pipeline_mode=pl.Buffered(1)),
                   pl.BlockSpec((nb, SUBLANES, D_B), lambda s, t: (s, 0, 0)),
                   pl.BlockSpec((nb, SUBLANES, D_B), lambda s, t: (s, 0, 0))),
        scratch_shapes=[pltpu.VMEM((nb, SUBLANES, C_A + D_B), F32),
                        pltpu.VMEM((nb, SUBLANES, D_B), F32),
                        pltpu.VMEM((streams, N_PAIRS, LANES, LANES), F32)]
                       + [slab(D_A) for _ in range(9)],
        compiler_params=_cparams(("arbitrary", "arbitrary")),
        name="mixer",
    )(proj, p_prev, conv0, s0, lru0, *params)


def _layernorm(x, w, b):
    mean = jnp.mean(x, axis=-1, keepdims=True)
    xc = x - mean
    var = jnp.mean(xc * xc, axis=-1, keepdims=True)
    return xc * lax.rsqrt(var + LN_EPS) * w + b


def _outproj_kernel(y_ref, x_ref, g1_ref, sc2_ref, sh2_ref, w_ref, lnw_ref, lnb_ref, x1_ref, h2_ref, *, alpha):
    ns, tb, _ = y_ref.shape
    mix = _dot1(y_ref[...].reshape(ns * tb, D_MODEL), w_ref[...]).reshape(ns, tb, D_MODEL)
    x1 = _layernorm(alpha * x_ref[...] + (1.0 + g1_ref[...]) * mix, lnw_ref[...], lnb_ref[...])
    x1_ref[...] = x1
    h2 = x1 * (1.0 + sc2_ref[...]) + sh2_ref[...]
    h2_ref[...] = h2.reshape(ns * tb, D_MODEL).astype(BF16)


def _outproj(y, x, mod, w_bf16, ln_w, ln_b, ns, tb, alpha):
    S, T, _ = x.shape
    nt = T // tb
    blk = pl.BlockSpec((ns, tb, D_MODEL), lambda s, t: (s, t, 0))
    modspec = lambda i: pl.BlockSpec((ns, 1, D_MODEL), lambda s, t, i=i: (s, 0, i))
    vec = pl.BlockSpec((1, D_MODEL), lambda s, t: (0, 0))
    return pl.pallas_call(
        functools.partial(_outproj_kernel, alpha=alpha),
        out_shape=(jax.ShapeDtypeStruct((S, T, D_MODEL), F32), jax.ShapeDtypeStruct((S * T, D_MODEL), BF16)),
        grid=(S // ns, nt),
        in_specs=[blk, blk, modspec(2), modspec(4), modspec(3),
                  pl.BlockSpec((D_MODEL, D_MODEL), lambda s, t: (0, 0)), vec, vec],
        out_specs=(blk, pl.BlockSpec((ns * tb, D_MODEL), lambda s, t: (s * nt + t, 0))),
        compiler_params=_cparams(("arbitrary", "arbitrary")),
        name="outproj",
    )(y, x, mod, mod, mod, w_bf16, ln_w.reshape(1, -1), ln_b.reshape(1, -1))


def _topk_rows(s, ridx, n_rows):
    out_i = _iota((TOPK, s.shape[1]), 0)
    vals = jnp.zeros((TOPK, s.shape[1]), F32)
    idxs = jnp.zeros((TOPK, s.shape[1]), F32)
    for it in range(TOPK):
        m = jnp.max(s, axis=0, keepdims=True)
        idx = jnp.min(jnp.where(s == m, ridx, float(n_rows)), axis=0, keepdims=True)
        vals = jnp.where(out_i == it, m, vals)
        idxs = jnp.where(out_i == it, idx, idxs)
        s = jnp.where(ridx == idx, -jnp.inf, s)
    return vals, idxs


def _route_kernel(h_ref, wq_ref, sk_ref, g_ref, gate_ref, e1_ref, e2_ref):
    R = ROWS
    q = _dot1(h_ref[...], wq_ref[...])
    sk0 = sk_ref[0]
    sk1 = sk_ref[1]
    gates, e1s, e2s = [], [], []
    key_rows = _iota((N_KEYS, R), 0).astype(F32)
    r16 = _iota((TOPK, R), 0).astype(F32)
    r8 = _iota((SUBLANES, R), 0).astype(F32)
    ea_rank = jnp.where(r8 < 3, 2.0, jnp.where(r8 < 5, 3.0, 4.0))
    eb_rank = jnp.where((r8 == 0) | (r8 == 3) | (r8 == 5), 2.0, jnp.where((r8 == 1) | (r8 == 4), 3.0, 4.0))
    flat = jnp.concatenate([r16, TOPK + r8, TOPK * r16, TOPK * r8 + 1.0, TOPK * ea_rank + eb_rank], axis=0)
    neg = -jnp.inf
    for hd in range(P_HEADS):
        base = hd * 2 * D_QH
        s1 = _dot1(sk0, q[:, base:base + D_QH], _NT)
        s2 = _dot1(sk1, q[:, base + D_QH:base + 2 * D_QH], _NT)
        v1, i1 = _topk_rows(s1, key_rows, N_KEYS)
        v2, i2 = _topk_rows(s2, key_rows, N_KEYS)
        ea = jnp.where(r8 < 3, v1[2:3], jnp.where(r8 < 5, v1[3:4], v1[4:5]))
        eb = jnp.where(eb_rank == 2.0, v2[2:3], jnp.where(eb_rank == 3.0, v2[3:4], v2[4:5]))
        cand = jnp.concatenate([
            v1[0:1] + v2,
            v1[1:2] + v2[:SUBLANES],
            jnp.where(r16 >= 2, v1 + v2[0:1], neg),
            jnp.where(r8 >= 2, v1[:SUBLANES] + v2[1:2], neg),
            jnp.where(r8 < 6, ea + eb, neg),
        ], axis=0)
        sv, ci = _topk_rows(cand, flat, TOPK * TOPK)
        ca = jnp.floor(ci * (1.0 / TOPK))
        cb = ci - TOPK * ca
        e1 = jnp.zeros((TOPK, R), F32)
        e2 = jnp.zeros((TOPK, R), F32)
        for a in range(TOPK):
            e1 = jnp.where(ca == float(a), i1[a:a + 1, :], e1)
            e2 = jnp.where(cb == float(a), i2[a:a + 1, :], e2)
        ex = jnp.exp(sv - jnp.max(sv, axis=0, keepdims=True))
        gates.append(ex / jnp.sum(ex, axis=0, keepdims=True))
        e1s.append(e1)
        e2s.append(e2)
    gate_ref[...] = jnp.concatenate(gates, axis=0).T
    e1_ref[...] = jnp.concatenate(e1s, axis=0).T
    e2_ref[...] = jnp.concatenate(e2s, axis=0).T
    key_i = _iota((N_KEYS, P_HEADS * TOPK), 0).astype(F32)

    def per_octet(o, carry):
        base = pl.multiple_of(o * SUBLANES, SUBLANES)
        tiles = []
        for j in range(SUBLANES):
            sel1 = key_i == e1_ref[pl.ds(base + j, 1), :]
            sel2 = key_i == e2_ref[pl.ds(base + j, 1), :]
            m1 = jnp.where(sel1, gate_ref[pl.ds(base + j, 1), :], 0.0)
            m2 = jnp.where(sel2, 1.0, 0.0)
            tiles.append(_dot1(m1, m2, _NT))
        by_key = jnp.swapaxes(jnp.stack(tiles, axis=0), 0, 1)
        for i1v in range(N_KEYS):
            g_ref[pl.ds(base, SUBLANES), i1v * N_KEYS:(i1v + 1) * N_KEYS] = by_key[i1v]
        return carry

    lax.fori_loop(0, R // SUBLANES, per_octet, 0, unroll=8)


def _route(h2, wq_bf16, sub_keys):
    n = h2.shape[0]
    assert n % ROWS == 0
    return pl.pallas_call(
        _route_kernel,
        out_shape=jax.ShapeDtypeStruct((n, N_EXPERTS), F32),
        grid=(n // ROWS,),
        in_specs=[pl.BlockSpec((ROWS, D_MODEL), lambda i: (i, 0)),
                  pl.BlockSpec((D_MODEL, D_MODEL), lambda i: (0, 0)),
                  pl.BlockSpec((2, N_KEYS, D_QH), lambda i: (0, 0, 0))],
        out_specs=pl.BlockSpec((ROWS, N_EXPERTS), lambda i: (i, 0)),
        scratch_shapes=[pltpu.VMEM((ROWS, P_HEADS * TOPK), F32) for _ in range(3)],
        compiler_params=_cparams(("arbitrary",)),
        name="route",
    )(h2, wq_bf16, sub_keys)


EXPERT_CHUNK = 512
EXPERT_ROWS_MAX = 1152


EXPERT_SPLIT = 4


def _experts_kernel(h_ref, g_ref, u_ref, v_ref, o_ref):
    @pl.when(pl.program_id(1) == 0)
    def _():
        o_ref[...] = jnp.zeros_like(o_ref)

    u = u_ref[...].astype(BF16)
    v = v_ref[...].astype(BF16)
    rb = h_ref.shape[0] // EXPERT_SPLIT
    rows = [pl.ds(b * rb, rb) for b in range(EXPERT_SPLIT)]
    act = _dg(h_ref[rows[0], :], u, _NT)
    for b in range(EXPERT_SPLIT):
        nxt = _dg(h_ref[rows[b + 1], :], u, _NT) if b + 1 < EXPERT_SPLIT else None
        w = (g_ref[rows[b], :] * _gelu(act)).astype(BF16)
        o_ref[rows[b], :] += _dg(w, v, _NN)
        act = nxt


def _token_block(n, cap):
    for tb in range(min(n, cap), 0, -1):
        if n % tb == 0 and tb % (2 * SUBLANES * EXPERT_SPLIT) == 0:
            return tb
    raise ValueError(n)


def _experts(h2, gmat, peer_u, peer_v):
    n = h2.shape[0]
    tb = _token_block(n, EXPERT_ROWS_MAX)
    ec = EXPERT_CHUNK
    return pl.pallas_call(
        _experts_kernel,
        out_shape=jax.ShapeDtypeStruct((n, D_MODEL), F32),
        grid=(n // tb, N_EXPERTS // ec),
        in_specs=[pl.BlockSpec((tb, D_MODEL), lambda i, e: (i, 0)),
                  pl.BlockSpec((tb, ec), lambda i, e: (i, e)),
                  pl.BlockSpec((ec, D_MODEL), lambda i, e: (e, 0)),
                  pl.BlockSpec((ec, D_MODEL), lambda i, e: (e, 0))],
        out_specs=pl.BlockSpec((tb, D_MODEL), lambda i, e: (i, 0), pipeline_mode=pl.Buffered(1)),
        compiler_params=_cparams(("arbitrary", "arbitrary")),
        name="experts",
    )(h2, gmat, peer_u, peer_v)


def _ln2_kernel(x1_ref, ff_ref, g2_ref, lnw_ref, lnb_ref, o_ref, *, alpha):
    ff = ff_ref[...].reshape(x1_ref.shape)
    o_ref[...] = _layernorm(alpha * x1_ref[...] + (1.0 + g2_ref[...]) * ff, lnw_ref[...], lnb_ref[...])


def _ln2(x1, ff, mod, ln_w, ln_b, ns, tb, alpha):
    S, T, _ = x1.shape
    nt = T // tb
    blk = pl.BlockSpec((ns, tb, D_MODEL), lambda s, t: (s, t, 0))
    vec = pl.BlockSpec((1, D_MODEL), lambda s, t: (0, 0))
    if ff.ndim == 2:
        assert ns == 1
        ffspec = pl.BlockSpec((tb, D_MODEL), lambda s, t: (s * nt + t, 0))
    else:
        ffspec = blk
    return pl.pallas_call(
        functools.partial(_ln2_kernel, alpha=alpha),
        out_shape=jax.ShapeDtypeStruct((S, T, D_MODEL), F32),
        grid=(S // ns, nt),
        in_specs=[blk, ffspec, pl.BlockSpec((ns, 1, D_MODEL), lambda s, t: (s, 0, 5)), vec, vec],
        out_specs=blk,
        compiler_params=_cparams(("arbitrary", "arbitrary")),
        name="ln2",
    )(x1, ff, mod, ln_w.reshape(1, -1), ln_b.reshape(1, -1))


def _pair_blockdiag(w):
    n = w.shape[0] // 2
    w = w.reshape(n, 2, HEAD, HEAD)
    z = jnp.zeros((n, HEAD, HEAD), w.dtype)
    top = jnp.concatenate([w[:, 0], z], axis=2)
    bot = jnp.concatenate([z, w[:, 1]], axis=2)
    return jnp.concatenate([top, bot], axis=1)


def _mix_group(x, mod, p_prev, wkv0, lru0, conv0, prm, *, ns_mix, tb_mix, chunk, t_valid, ns_big, tb_big, alpha,
               streams=1, shift_rows=None):
    S, T, _ = x.shape
    if shift_rows is None:
        proj, h_last = _inproj(x, mod, prm['w_in'], ns_big, tb_big)
    else:
        extra = S // T
        x_ext = jnp.concatenate([x, shift_rows.reshape(extra, T, D_MODEL)], axis=0)
        mod_ext = jnp.concatenate([mod, jnp.zeros((extra, 1, 6 * D_MODEL), F32)], axis=0)
        ns_ext = max(d for d in range(1, ns_big + 1) if (S + extra) % d == 0)
        proj, h_last = _inproj(x_ext, mod_ext, prm['w_in'], ns_ext, tb_big)
        p_prev = proj[S:].reshape(S, C_TOT)[:, :C_A]
        h_last = h_last[:S]
    y, wkv, lru_o, conv_o = _mixer(proj, p_prev[:, None, :], conv0, wkv0, lru0[:, None, :], prm,
                                   ns_mix, tb_mix, chunk, t_valid, streams)
    x1, h2 = _outproj(y, x, mod, prm['w_out'], prm['ln1_w'], prm['ln1_b'], ns_big, tb_big, alpha)
    tv = (t_valid - 1) % SUBLANES
    shift = h_last[:, tv]
    lru = lru_o[:, SUBLANES - 1]
    conv = conv_o[:, tv - (CONV_W - 2):tv + 1]
    return x1, h2, (shift, wkv, lru, conv)


def kernel(x_prompt, x_sample, c_prompt, c_sample, state_shift, state_wkv, state_lru, state_conv, w_ada, b_ada, w_in, mu_shift, w0, w_w2, a0, w_a2, w_g2, k_k, k_a, r_k, lnx_w, lnx_b, conv_w, conv_b, w_gate_a, b_gate_a, w_gate_i, b_gate_i, lru_lambda, w_out, ln1_w, ln1_b, w_q, sub_keys, peer_u, peer_v, ln2_w, ln2_b):
    depth = w_ada.shape[0]
    alpha = (2 * depth) ** 0.25
    bp, tp, _ = x_prompt.shape
    bs, ts, _ = x_sample.shape
    yp = x_prompt
    ys = jnp.pad(x_sample, ((0, 0), (0, SAMPLE_TPAD - ts), (0, 0)))
    head_of = jnp.arange(D_A) // HEAD
    seg = (head_of[:, None] == jnp.arange(LANES)[None, :]).astype(BF16)
    row = lambda a: a.reshape(1, -1)
    outs = [[] for _ in range(8)]
    for l in range(depth):
        zpad = jnp.zeros((LANES - R_W, D_A), F32)
        prm = {
            'w_in': w_in[l].astype(BF16), 'w_out': w_out[l].astype(BF16), 'w_q': w_q[l].astype(BF16),
            'sub_keys': sub_keys[l],
            'mu': row(mu_shift[l]), 'w0': row(w0[l]),
            'ww2': jnp.concatenate([w_w2[l], zpad], axis=0), 'a0': row(a0[l]),
            'wa2': jnp.concatenate([zpad, w_a2[l]], axis=0), 'wg2': w_g2[l].astype(BF16),
            'k_k': row(k_k[l]), 'k_a': row(k_a[l]), 'r_k': row(r_k[l]),
            'lnx_w': row(lnx_w[l]), 'lnx_b': row(lnx_b[l]), 'seg': seg, 'segt': seg.T,
            'conv_w': conv_w[l], 'conv_b': row(conv_b[l]),
            'wga': _pair_blockdiag(w_gate_a[l]).astype(BF16), 'bga': row(b_gate_a[l]),
            'wgi': _pair_blockdiag(w_gate_i[l]).astype(BF16), 'bgi': row(b_gate_i[l]),
            'lam': row(lru_lambda[l]),
            'ln1_w': ln1_w[l], 'ln1_b': ln1_b[l], 'ln2_w': ln2_w[l], 'ln2_b': ln2_b[l],
        }
        c_all = jnp.concatenate([c_prompt, c_sample], axis=0)
        mod = _ada(c_all, w_ada[l], b_ada[l])
        mod_p = mod[:bp, None, :]
        mod_s = mod[bp:, None, :]
        x1p, h2p, st_p = _mix_group(
            yp, mod_p, jnp.zeros((bp, C_A), F32), jnp.zeros((bp, N_HEADS, HEAD, HEAD), F32),
            jnp.zeros((bp, D_B), F32), jnp.zeros((bp, CONV_W - 1, D_B), F32), prm,
            ns_mix=1, tb_mix=ROWS, chunk=16, t_valid=ROWS, ns_big=1, tb_big=min(512, tp), alpha=alpha,
            streams=2 if bp % 2 == 0 else 1)
        x1s, h2s, st_s = _mix_group(
            ys, mod_s, None, state_wkv[l], state_lru[l], state_conv[l], prm,
            ns_mix=ROWS // SAMPLE_TPAD, tb_mix=SAMPLE_TPAD, chunk=SAMPLE_TPAD, t_valid=ts,
            ns_big=min(64, bs), tb_big=SAMPLE_TPAD, alpha=alpha, shift_rows=state_shift[l])
        h2s = h2s.reshape(bs, SAMPLE_TPAD, D_MODEL)[:, :ts].reshape(bs * ts, D_MODEL)
        h2 = jnp.concatenate([h2p, h2s], axis=0)
        gmat = _route(h2, prm['w_q'], prm['sub_keys'])
        ff = _experts(h2, gmat, peer_u[l], peer_v[l])
        yp = _ln2(x1p, ff, mod_p, ln2_w[l], ln2_b[l], 1, min(512, tp), alpha)
        ys_real = _ln2(x1s[:, :ts], ff[bp * tp:].reshape(bs, ts, D_MODEL), mod_s, ln2_w[l], ln2_b[l],
                       min(64, bs), ts, alpha)
        ys = jnp.pad(ys_real, ((0, 0), (0, SAMPLE_TPAD - ts), (0, 0)))
        for i, a in enumerate(st_p + st_s):
            outs[i].append(a)
    return (yp, ys[:, :ts]) + tuple(jnp.stack(o) for o in outs)
```

```python
import functools

import jax
import jax.numpy as jnp
from jax import lax
from jax.experimental import pallas as pl
from jax.experimental.pallas import tpu as pltpu

F32 = jnp.float32
BF16 = jnp.bfloat16

D_MODEL = 2048
D_A = 1024
D_B = 1024
HEAD = 64
N_HEADS = D_A // HEAD
N_PAIRS = N_HEADS // 2
R_W = 64
R_A = 64
R_G = 128
C_A = 3 * D_A + R_W + R_A + R_G
C_TOT = C_A + 2 * D_B
CONV_W = 4
C_LRU = 8.0
P_HEADS = 8
N_KEYS = 128
N_EXPERTS = N_KEYS * N_KEYS
TOPK = 16
D_QH = 128
LN_EPS = 1e-5
GN_EPS = 64e-5

LANES = 128
SUBLANES = 8
ROWS = 128
SAMPLE_TPAD = 8
VMEM_LIMIT = 56 * 1024 * 1024


def _cparams(sem):
    return pltpu.CompilerParams(dimension_semantics=sem, vmem_limit_bytes=VMEM_LIMIT)


def _split2(x):
    hi = x.astype(BF16)
    lo = (x - hi.astype(F32)).astype(BF16)
    return hi, lo


_NN = (((1,), (0,)), ((), ()))
_NT = (((1,), (1,)), ((), ()))
_TN = (((0,), (0,)), ((), ()))


def _dg(a, b, dims):
    return lax.dot_general(a, b, dims, preferred_element_type=F32)


def _dot1(a, b, dims=_NN):
    return _dg(a.astype(BF16), b.astype(BF16), dims)


def _dot3(a, b, dims=_NN):
    ah, al = _split2(a)
    bh, bl = _split2(b)
    return _dg(ah, bh, dims) + (_dg(ah, bl, dims) + _dg(al, bh, dims))


_state_dot = _dot1


def _dot_exact_rhs(a, b_exact, dims=_NN):
    a1 = a.astype(BF16)
    r1 = a - a1.astype(F32)
    a2 = r1.astype(BF16)
    a3 = (r1 - a2.astype(F32)).astype(BF16)
    return _dg(a1, b_exact, dims) + (_dg(a2, b_exact, dims) + _dg(a3, b_exact, dims))


def _dot_exact_lhs(a_exact, b, dims=_NN):
    b1 = b.astype(BF16)
    r1 = b - b1.astype(F32)
    b2 = r1.astype(BF16)
    b3 = (r1 - b2.astype(F32)).astype(BF16)
    return _dg(a_exact, b1, dims) + (_dg(a_exact, b2, dims) + _dg(a_exact, b3, dims))


def _sigmoid(x):
    return 1.0 / (1.0 + jnp.exp(-x))


def _softplus(x):
    return jnp.maximum(x, 0.0) + jnp.log(1.0 + jnp.exp(-jnp.abs(x)))


def _gelu(x):
    return 0.5 * x * (1.0 + jnp.tanh(0.7978845608028654 * (x + 0.044715 * (x * x * x))))


def _iota(shape, dim):
    return lax.broadcasted_iota(jnp.int32, shape, dim)


def _ada_kernel(c_ref, w_ref, b_ref, o_ref):
    c = c_ref[...]
    s = c * _sigmoid(c)
    o_ref[...] = _dot3(s, w_ref[...]) + b_ref[...]


def _ada(c, w_ada, b_ada):
    n = c.shape[0]
    tn = 1024
    return pl.pallas_call(
        _ada_kernel,
        out_shape=jax.ShapeDtypeStruct((n, 6 * D_MODEL), F32),
        grid=(6 * D_MODEL // tn,),
        in_specs=[pl.BlockSpec((n, D_MODEL), lambda j: (0, 0)),
                  pl.BlockSpec((D_MODEL, tn), lambda j: (0, j)),
                  pl.BlockSpec((1, tn), lambda j: (0, j))],
        out_specs=pl.BlockSpec((n, tn), lambda j: (0, j)),
        compiler_params=_cparams(("arbitrary",)),
        name="ada",
    )(c, w_ada, b_ada.reshape(1, -1))


def _inproj_kernel(x_ref, sc_ref, sh_ref, w_ref, o_ref, h_ref):
    h = x_ref[...] * (1.0 + sc_ref[...]) + sh_ref[...]
    ns, tb, _ = h.shape
    h_ref[...] = h[:, tb - SUBLANES:, :]
    p = _dot1(h.reshape(ns * tb, D_MODEL), w_ref[...])
    o_ref[...] = p.reshape(ns, tb, p.shape[-1])


def _inproj(x, mod, w_bf16, ns, tb):
    S, T, _ = x.shape
    ncol = w_bf16.shape[1]
    tn = ncol // 3
    proj, h_last = pl.pallas_call(
        _inproj_kernel,
        out_shape=(jax.ShapeDtypeStruct((S, T, ncol), F32),
                   jax.ShapeDtypeStruct((3, S, SUBLANES, D_MODEL), F32)),
        grid=(3, S // ns, T // tb),
        in_specs=[pl.BlockSpec((ns, tb, D_MODEL), lambda j, s, t: (s, t, 0)),
                  pl.BlockSpec((ns, 1, D_MODEL), lambda j, s, t: (s, 0, 1)),
                  pl.BlockSpec((ns, 1, D_MODEL), lambda j, s, t: (s, 0, 0)),
                  pl.BlockSpec((D_MODEL, tn), lambda j, s, t: (0, j))],
        out_specs=(pl.BlockSpec((ns, tb, tn), lambda j, s, t: (s, t, j)),
                   pl.BlockSpec((None, ns, SUBLANES, D_MODEL), lambda j, s, t: (j, s, 0, 0))),
        compiler_params=_cparams(("arbitrary", "arbitrary", "arbitrary")),
        name="inproj",
    )(x, mod, mod, w_bf16)
    return proj, h_last[0]


def _pair_tile(ref, s, pr):
    zero_blk = jnp.zeros((HEAD, HEAD), F32)
    top = jnp.concatenate([ref[s, 2 * pr], zero_blk], axis=1)
    bot = jnp.concatenate([zero_blk, ref[s, 2 * pr + 1]], axis=1)
    return jnp.concatenate([top, bot], axis=0)


def _mixer_stream(proj_ref, s0_ref,
                  mu_ref, w0_ref, ww2_ref, a0_ref, wa2_ref, wg2_ref, kk_ref, ka_ref, rk_ref,
                  lnw_ref, lnb_ref, seg_ref, segt_ref,
                  cw_ref, cb_ref, wga_ref, bga_ref, wgi_ref, bgi_ref, lam_ref,
                  y_ref, sout_ref,
                  prev_ref, hst_ref, st_ref,
                  bp_ref, rp_ref, x2_ref, y2_ref, v_ref, kb_ref, ab_ref, wt_ref, yacc_ref,
                  *, ns, tb, chunk, t_valid, carry_state):
    R = ns * tb
    pair_tile = _pair_tile

    cur = proj_ref[...]
    ext = jnp.concatenate([prev_ref[...], cur[:, :, :C_A + D_B]], axis=1)
    prev_ref[...] = cur[:, tb - SUBLANES:, :C_A + D_B]

    row = _iota((R, 1), 0)
    rowin = row % tb
    valid = rowin < t_valid

    pa = cur[:, :, :C_A].reshape(R, C_A)
    shifted = ext[:, SUBLANES - 1:SUBLANES - 1 + tb, :C_A].reshape(R, C_A)
    p = pa + (shifted - pa) * mu_ref[...]
    r = p[:, :D_A]
    k = p[:, D_A:2 * D_A]
    v = p[:, 2 * D_A:3 * D_A]
    xwa = p[:, 3 * D_A:3 * D_A + LANES]
    xg = p[:, 3 * D_A + LANES:]
    w_log = -_softplus(-(w0_ref[...] + _dot3(jnp.tanh(xwa), ww2_ref[...]))) - 0.5
    logw = -jnp.exp(w_log)
    a = _sigmoid(a0_ref[...] + _dot3(xwa, wa2_ref[...]))
    g = _dot1(_sigmoid(xg), wg2_ref[...])
    yield 900
    seg = seg_ref[...]
    segt = segt_ref[...]

    def seg_sum(x):
        return _dot_exact_rhs(_dot_exact_rhs(x, seg), segt)

    kk = k * kk_ref[...]
    kk = kk / jnp.maximum(jnp.sqrt(seg_sum(kk * kk)), 1e-12)
    yield 600
    k = k * (1.0 + (a - 1.0) * ka_ref[...])
    kka = kk * a
    bonus = seg_sum(r * k * rk_ref[...]) * v
    yield 600
    logw = jnp.where(valid, logw, 0.0)
    r = jnp.where(valid, r, 0.0)
    k = jnp.where(valid, k, 0.0)
    v = jnp.where(valid, v, 0.0)
    kk = jnp.where(valid, kk, 0.0)
    kka = jnp.where(valid, kka, 0.0)

    ri = _iota((R, R), 0)
    ci = _iota((R, R), 1)
    same = (ri // chunk) == (ci // chunk)
    incl = same & (ci <= ri)
    strict = same & (ci < ri)
    lmat = jnp.concatenate([jnp.where(incl, 1.0, 0.0), jnp.where(same, 1.0, 0.0)], axis=0).astype(BF16)
    cums = _dot_exact_lhs(lmat, logw)
    cum = cums[:R]
    tot = cums[R:]
    yield 500
    e_prev = jnp.exp(cum - logw)
    e_neg = jnp.exp(-cum)
    e_pos = jnp.exp(cum)
    e_rem = jnp.exp(tot - cum)
    beta = kk * e_prev
    alpha = kka * e_neg
    kappa = k * e_neg
    rho = r * e_pos
    yield 600
    v_ref[...] = v
    kb_ref[...] = k * e_rem
    ab_ref[...] = -(kka * e_rem)
    wt_ref[...] = jnp.exp(tot)
    yield 400

    lane = _iota((1, LANES), 1)
    masks = (lane < HEAD, lane >= HEAD)
    eye = jnp.where(ri == ci, 1.0, 0.0)
    nsq = {16: 3, 8: 2}[chunk]
    heads = [(pr, m) for pr in range(N_PAIRS) for m in masks]
    psl = lambda pr: slice(pr * LANES, (pr + 1) * LANES)
    bms = [jnp.where(m, beta[:, psl(pr)], 0.0) for pr, m in heads]
    rms = [jnp.where(m, rho[:, psl(pr)], 0.0) for pr, m in heads]
    yield 300
    gms = []
    for (pr, m), bm, rm in zip(heads, bms, rms):
        gms.append(_dot1(jnp.concatenate([bm, rm], axis=0),
                         jnp.concatenate([alpha[:, psl(pr)], kappa[:, psl(pr)]], axis=0), _NT))
        yield 70
    l_bas = [jnp.where(strict, gm[:R, :R], 0.0) for gm in gms]
    m_ras = [jnp.where(incl, gm[R:, :R], 0.0) for gm in gms]
    xy1s = []
    for (pr, m), gm in zip(heads, gms):
        xy1s.append(_dot1(jnp.concatenate([jnp.where(strict, gm[:R, R:], 0.0),
                                           jnp.where(incl, gm[R:, R:], 0.0)], axis=0), v[:, psl(pr)]))
        yield 70
    pws = []
    for l in l_bas:
        pws.append(_dot1(l, l))
        yield 40

    def lru_steps():
        for pr in range(N_PAIRS):
            sl = psl(pr)
            bsl = slice(C_A + pr * LANES, C_A + (pr + 1) * LANES)
            gsl = slice(C_A + D_B + pr * LANES, C_A + D_B + (pr + 1) * LANES)
            xc = cb_ref[:, sl]
            for j in range(CONV_W):
                off = SUBLANES - (CONV_W - 1) + j
                xc = xc + ext[:, off:off + tb, bsl].reshape(R, LANES) * cw_ref[j:j + 1, sl]
            yield
            rg = _sigmoid(_dot1(xc, wga_ref[pr]) + bga_ref[:, sl])
            ig = _sigmoid(_dot1(xc, wgi_ref[pr]) + bgi_ref[:, sl])
            yield
            log_a = -C_LRU * rg * _softplus(-lam_ref[:, sl])
            av = jnp.where(valid, jnp.exp(log_a), 1.0)
            bv = jnp.where(valid, jnp.sqrt(1.0 - jnp.exp(2.0 * log_a)) * (ig * xc), 0.0)
            yield
            d = 1
            while d < tb:
                take = rowin >= d
                a_sh = jnp.where(take, pltpu.roll(av, d, axis=0), 1.0)
                b_sh = jnp.where(take, pltpu.roll(bv, d, axis=0), 0.0)
                bv = av * b_sh + bv
                av = av * a_sh
                d *= 2
                yield
            h0 = jnp.broadcast_to(hst_ref[:, SUBLANES - 1:SUBLANES, sl], (ns, tb, LANES)).reshape(R, LANES)
            h3 = (av * h0 + bv).reshape(ns, tb, LANES)
            hst_ref[:, :, sl] = h3[:, tb - SUBLANES:, :]
            gb = cur[:, :, gsl].reshape(R, LANES)
            y_ref[:, :, D_A + pr * LANES:D_A + (pr + 1) * LANES] = h3 * _gelu(gb).reshape(ns, tb, LANES)
            yield

    lru = lru_steps()
    minvs = [eye - l for l in l_bas]
    for i in range(nsq):
        nxt = []
        for mi, pw in zip(minvs, pws):
            nxt.append(mi + _dot1(mi, pw))
            next(lru, None)
            yield 60
        minvs = nxt
        if i + 1 < nsq:
            nxt = []
            for pw in pws:
                nxt.append(_dot1(pw, pw))
                next(lru, None)
                yield 60
            pws = nxt
    for _ in lru:
        yield 40
    t1s = []
    for mi, xy1, bm in zip(minvs, xy1s, bms):
        t1s.append(_dot1(mi, jnp.concatenate([xy1[:R], bm], axis=1)))
        yield 50
    t2s = []
    for m_ra, t1 in zip(m_ras, t1s):
        t2s.append(_dot1(m_ra, t1))
        yield 50
    for pr in range(N_PAIRS):
        h0, h1 = 2 * pr, 2 * pr + 1
        sl = psl(pr)
        m1 = masks[1]
        bp_ref[:, sl] = t1s[h0][:, LANES:] + t1s[h1][:, LANES:]
        rp_ref[:, sl] = (rms[h0] - t2s[h0][:, LANES:]) + (rms[h1] - t2s[h1][:, LANES:])
        x2_ref[:, sl] = jnp.where(m1, t1s[h1][:, :LANES], t1s[h0][:, :LANES])
        y2_ref[:, sl] = jnp.where(m1, xy1s[h1][R:] - t2s[h1][:, :LANES], xy1s[h0][R:] - t2s[h0][:, :LANES])

    bi = _iota((LANES, LANES), 0)
    bj = _iota((LANES, LANES), 1)
    bd = (bi < HEAD) == (bj < HEAD)
    nck = tb // chunk
    sites = [(s, c, pr) for s in range(ns) for c in range(nck) for pr in range(N_PAIRS)]
    rows_of = lambda s, c: slice(s * tb + c * chunk, s * tb + (c + 1) * chunk)
    thetas = {}
    psis = {}
    for s, c, pr in sites:
        rs, sl = rows_of(s, c), psl(pr)
        thetas[s, c, pr] = jnp.where(bd, _state_dot(bp_ref[rs, sl], ab_ref[rs, sl], _TN), 0.0)
        vx = jnp.concatenate([v_ref[rs, sl], x2_ref[rs, sl]], axis=0)
        ka = jnp.concatenate([kb_ref[rs, sl], ab_ref[rs, sl]], axis=0)
        psis[s, c, pr] = jnp.where(bd, _state_dot(vx, ka, _TN), 0.0)
        yield 60
    for s in range(ns):
        sps = [st_ref[s, pr] if carry_state else pair_tile(s0_ref, s, pr) for pr in range(N_PAIRS)]
        for c in range(nck):
            rs = rows_of(s, c)
            for pr in range(N_PAIRS):
                yacc_ref[rs, psl(pr)] = _state_dot(rp_ref[rs, psl(pr)], sps[pr], _NT) + y2_ref[rs, psl(pr)]
            yield 100
            nxt = []
            for pr in range(N_PAIRS):
                nxt.append(sps[pr] * wt_ref[rs.start:rs.start + 1, psl(pr)]
                           + (_state_dot(sps[pr], thetas[s, c, pr]) + psis[s, c, pr]))
                yield 50
            sps = nxt
        for pr in range(N_PAIRS):
            if carry_state:
                st_ref[s, pr] = sps[pr]
            else:
                sout_ref[s, 2 * pr] = sps[pr][:HEAD, :HEAD]
                sout_ref[s, 2 * pr + 1] = sps[pr][HEAD:, HEAD:]
        yield 50

    y = yacc_ref[...]
    mean = seg_sum(y) * (1.0 / HEAD)
    yc = y - mean
    yield 400
    var = seg_sum(yc * yc) * (1.0 / HEAD)
    yn = yc * lax.rsqrt(var + GN_EPS) * lnw_ref[...] + lnb_ref[...]
    ya = (yn + bonus) * g
    y_ref[:, :, :D_A] = ya.reshape(ns, tb, D_A)
    yield 600


def _mixer_kernel(proj_ref, pprev_ref, conv0_ref, s0_ref, lru0_ref, *refs, ns, tb, chunk, t_valid, carry_state, streams):
    n_par = 20
    params = refs[:n_par]
    y_ref, sout_ref, lruo_ref, convo_ref, prev_ref, hst_ref, st_ref = refs[n_par:n_par + 7]
    slabs = refs[n_par + 7:]
    t = pl.program_id(1)

    @pl.when(t == 0)
    def _():
        prev_ref[...] = jnp.zeros_like(prev_ref)
        prev_ref[:, SUBLANES - 1:, :C_A] = pprev_ref[...]
        prev_ref[:, SUBLANES - (CONV_W - 1):, C_A:] = conv0_ref[...]
        hst_ref[...] = jnp.broadcast_to(lru0_ref[...], hst_ref.shape)
        if carry_state:
            for q in range(streams):
                for pr in range(N_PAIRS):
                    st_ref[q, pr] = _pair_tile(s0_ref, q, pr)

    def make(q):
        sq = pl.ds(q * ns, ns)
        return _mixer_stream(
            proj_ref.at[sq], s0_ref.at[sq], *params, y_ref.at[sq], sout_ref.at[sq],
            prev_ref.at[sq], hst_ref.at[sq], st_ref.at[pl.ds(q, 1)], *[r.at[q] for r in slabs],
            ns=ns, tb=tb, chunk=chunk, t_valid=t_valid, carry_state=carry_state)

    gens = [make(q) for q in range(streams)]
    clock = [0.0] * streams
    live = list(range(streams))
    while live:
        q = min(live, key=lambda i: clock[i])
        cost = next(gens[q], None)
        if cost is None:
            live.remove(q)
        else:
            clock[q] += cost

    @pl.when(t == pl.num_programs(1) - 1)
    def _():
        lruo_ref[...] = hst_ref[...]
        convo_ref[...] = proj_ref[:, tb - SUBLANES:, C_A:C_A + D_B]
        if carry_state:
            for q in range(streams):
                for pr in range(N_PAIRS):
                    tile = st_ref[q, pr]
                    sout_ref[q, 2 * pr] = tile[:HEAD, :HEAD]
                    sout_ref[q, 2 * pr + 1] = tile[HEAD:, HEAD:]


def _mixer(proj, p_prev, conv0, s0, lru0, prm, ns, tb, chunk, t_valid, streams=1):
    S, T = s0.shape[0], proj.shape[1]
    R = ns * tb
    assert R == ROWS
    carry_state = T // tb > 1
    assert ns == 1 or not carry_state
    kern = functools.partial(_mixer_kernel, ns=ns, tb=tb, chunk=chunk, t_valid=t_valid, carry_state=carry_state,
                             streams=streams)
    nb = ns * streams

    def full(a):
        nd = a.ndim
        return pl.BlockSpec(a.shape, lambda s, t, nd=nd: (0,) * nd)

    params = [prm[n] for n in ('mu', 'w0', 'ww2', 'a0', 'wa2', 'wg2', 'k_k', 'k_a', 'r_k', 'lnx_w', 'lnx_b',
                               'seg', 'segt', 'conv_w', 'conv_b', 'wga', 'bga', 'wgi', 'bgi', 'lam')]
    slab = lambda w: pltpu.VMEM((streams, R, w), F32)
    return pl.pallas_call(
        kern,
        out_shape=(jax.ShapeDtypeStruct((S, T, D_A + D_B), F32),
                   jax.ShapeDtypeStruct((S, N_HEADS, HEAD, HEAD), F32),
                   jax.ShapeDtypeStruct((S, SUBLANES, D_B), F32),
                   jax.ShapeDtypeStruct((S, SUBLANES, D_B), F32)),
        grid=(S // nb, T // tb),
        in_specs=[pl.BlockSpec((nb, tb, C_TOT), lambda s, t: (s, t, 0)),
                  pl.BlockSpec((nb, 1, C_A), lambda s, t: (s, 0, 0)),
                  pl.BlockSpec((nb, CONV_W - 1, D_B), lambda s, t: (s, 0, 0)),
                  pl.BlockSpec((nb, N_HEADS, HEAD, HEAD), lambda s, t: (s, 0, 0, 0),
                               pipeline_mode=pl.Buffered(1)),
                  pl.BlockSpec((nb, 1, D_B), lambda s, t: (s, 0, 0))]
                 + [full(a) for a in params],
        out_specs=(pl.BlockSpec((nb, tb, D_A + D_B), lambda s, t: (s, t, 0)),
                   pl.BlockSpec((nb, N_HEADS, HEAD, HEAD), lambda s, t: (s, 0, 0, 0),
                                pipeline_mode=pl.Buffered(1)),
                   pl.BlockSpec((nb, SUBLANES, D_B), lambda s, t: (s, 0, 0)),
                   pl.BlockSpec((nb, SUBLANES, D_B), lambda s, t: (s, 0, 0))),
        scratch_shapes=[pltpu.VMEM((nb, SUBLANES, C_A + D_B), F32),
                        pltpu.VMEM((nb, SUBLANES, D_B), F32),
                        pltpu.VMEM((streams, N_PAIRS, LANES, LANES), F32)]
                       + [slab(D_A) for _ in range(9)],
        compiler_params=_cparams(("arbitrary", "arbitrary")),
        name="mixer",
    )(proj, p_prev, conv0, s0, lru0, *params)


def _layernorm(x, w, b):
    mean = jnp.mean(x, axis=-1, keepdims=True)
    xc = x - mean
    var = jnp.mean(xc * xc, axis=-1, keepdims=True)
    return xc * lax.rsqrt(var + LN_EPS) * w + b


def _outproj_kernel(y_ref, x_ref, g1_ref, sc2_ref, sh2_ref, w_ref, lnw_ref, lnb_ref, x1_ref, h2_ref, *, alpha):
    ns, tb, _ = y_ref.shape
    mix = _dot1(y_ref[...].reshape(ns * tb, D_MODEL), w_ref[...]).reshape(ns, tb, D_MODEL)
    x1 = _layernorm(alpha * x_ref[...] + (1.0 + g1_ref[...]) * mix, lnw_ref[...], lnb_ref[...])
    x1_ref[...] = x1
    h2 = x1 * (1.0 + sc2_ref[...]) + sh2_ref[...]
    h2_ref[...] = h2.reshape(ns * tb, D_MODEL).astype(BF16)


def _outproj(y, x, mod, w_bf16, ln_w, ln_b, ns, tb, alpha):
    S, T, _ = x.shape
    nt = T // tb
    blk = pl.BlockSpec((ns, tb, D_MODEL), lambda s, t: (s, t, 0))
    modspec = lambda i: pl.BlockSpec((ns, 1, D_MODEL), lambda s, t, i=i: (s, 0, i))
    vec = pl.BlockSpec((1, D_MODEL), lambda s, t: (0, 0))
    return pl.pallas_call(
        functools.partial(_outproj_kernel, alpha=alpha),
        out_shape=(jax.ShapeDtypeStruct((S, T, D_MODEL), F32), jax.ShapeDtypeStruct((S * T, D_MODEL), BF16)),
        grid=(S // ns, nt),
        in_specs=[blk, blk, modspec(2), modspec(4), modspec(3),
                  pl.BlockSpec((D_MODEL, D_MODEL), lambda s, t: (0, 0)), vec, vec],
        out_specs=(blk, pl.BlockSpec((ns * tb, D_MODEL), lambda s, t: (s * nt + t, 0))),
        compiler_params=_cparams(("arbitrary", "arbitrary")),
        name="outproj",
    )(y, x, mod, mod, mod, w_bf16, ln_w.reshape(1, -1), ln_b.reshape(1, -1))


def _topk_rows(s, ridx, n_rows):
    out_i = _iota((TOPK, s.shape[1]), 0)
    vals = jnp.zeros((TOPK, s.shape[1]), F32)
    idxs = jnp.zeros((TOPK, s.shape[1]), F32)
    for it in range(TOPK):
        m = jnp.max(s, axis=0, keepdims=True)
        idx = jnp.min(jnp.where(s == m, ridx, float(n_rows)), axis=0, keepdims=True)
        vals = jnp.where(out_i == it, m, vals)
        idxs = jnp.where(out_i == it, idx, idxs)
        s = jnp.where(ridx == idx, -jnp.inf, s)
    return vals, idxs


def _route_kernel(h_ref, wq_ref, sk_ref, g_ref, gate_ref, e1_ref, e2_ref):
    R = ROWS
    q = _dot1(h_ref[...], wq_ref[...])
    sk0 = sk_ref[0]
    sk1 = sk_ref[1]
    gates, e1s, e2s = [], [], []
    key_rows = _iota((N_KEYS, R), 0).astype(F32)
    r16 = _iota((TOPK, R), 0).astype(F32)
    r8 = _iota((SUBLANES, R), 0).astype(F32)
    ea_rank = jnp.where(r8 < 3, 2.0, jnp.where(r8 < 5, 3.0, 4.0))
    eb_rank = jnp.where((r8 == 0) | (r8 == 3) | (r8 == 5), 2.0, jnp.where((r8 == 1) | (r8 == 4), 3.0, 4.0))
    flat = jnp.concatenate([r16, TOPK + r8, TOPK * r16, TOPK * r8 + 1.0, TOPK * ea_rank + eb_rank], axis=0)
    neg = -jnp.inf
    for hd in range(P_HEADS):
        base = hd * 2 * D_QH
        s1 = _dot1(sk0, q[:, base:base + D_QH], _NT)
        s2 = _dot1(sk1, q[:, base + D_QH:base + 2 * D_QH], _NT)
        v1, i1 = _topk_rows(s1, key_rows, N_KEYS)
        v2, i2 = _topk_rows(s2, key_rows, N_KEYS)
        ea = jnp.where(r8 < 3, v1[2:3], jnp.where(r8 < 5, v1[3:4], v1[4:5]))
        eb = jnp.where(eb_rank == 2.0, v2[2:3], jnp.where(eb_rank == 3.0, v2[3:4], v2[4:5]))
        cand = jnp.concatenate([
            v1[0:1] + v2,
            v1[1:2] + v2[:SUBLANES],
            jnp.where(r16 >= 2, v1 + v2[0:1], neg),
            jnp.where(r8 >= 2, v1[:SUBLANES] + v2[1:2], neg),
            jnp.where(r8 < 6, ea + eb, neg),
        ], axis=0)
        sv, ci = _topk_rows(cand, flat, TOPK * TOPK)
        ca = jnp.floor(ci * (1.0 / TOPK))
        cb = ci - TOPK * ca
        e1 = jnp.zeros((TOPK, R), F32)
        e2 = jnp.zeros((TOPK, R), F32)
        for a in range(TOPK):
            e1 = jnp.where(ca == float(a), i1[a:a + 1, :], e1)
            e2 = jnp.where(cb == float(a), i2[a:a + 1, :], e2)
        ex = jnp.exp(sv - jnp.max(sv, axis=0, keepdims=True))
        gates.append(ex / jnp.sum(ex, axis=0, keepdims=True))
        e1s.append(e1)
        e2s.append(e2)
    gate_ref[...] = jnp.concatenate(gates, axis=0).T
    e1_ref[...] = jnp.concatenate(e1s, axis=0).T
    e2_ref[...] = jnp.concatenate(e2s, axis=0).T
    key_i = _iota((N_KEYS, P_HEADS * TOPK), 0).astype(F32)

    def per_octet(o, carry):
        base = pl.multiple_of(o * SUBLANES, SUBLANES)
        tiles = []
        for j in range(SUBLANES):
            sel1 = key_i == e1_ref[pl.ds(base + j, 1), :]
            sel2 = key_i == e2_ref[pl.ds(base + j, 1), :]
            m1 = jnp.where(sel1, gate_ref[pl.ds(base + j, 1), :], 0.0)
            m2 = jnp.where(sel2, 1.0, 0.0)
            tiles.append(_dot1(m1, m2, _NT))
        by_key = jnp.swapaxes(jnp.stack(tiles, axis=0), 0, 1)
        for i1v in range(N_KEYS):
            g_ref[pl.ds(base, SUBLANES), i1v * N_KEYS:(i1v + 1) * N_KEYS] = by_key[i1v]
        return carry

    lax.fori_loop(0, R // SUBLANES, per_octet, 0, unroll=8)


def _route(h2, wq_bf16, sub_keys):
    n = h2.shape[0]
    assert n % ROWS == 0
    return pl.pallas_call(
        _route_kernel,
        out_shape=jax.ShapeDtypeStruct((n, N_EXPERTS), F32),
        grid=(n // ROWS,),
        in_specs=[pl.BlockSpec((ROWS, D_MODEL), lambda i: (i, 0)),
                  pl.BlockSpec((D_MODEL, D_MODEL), lambda i: (0, 0)),
                  pl.BlockSpec((2, N_KEYS, D_QH), lambda i: (0, 0, 0))],
        out_specs=pl.BlockSpec((ROWS, N_EXPERTS), lambda i: (i, 0)),
        scratch_shapes=[pltpu.VMEM((ROWS, P_HEADS * TOPK), F32) for _ in range(3)],
        compiler_params=_cparams(("arbitrary",)),
        name="route",
    )(h2, wq_bf16, sub_keys)


EXPERT_CHUNK = 512
EXPERT_ROWS_MAX = 1152


EXPERT_SPLIT = 4


def _experts_kernel(h_ref, g_ref, u_ref, v_ref, o_ref):
    @pl.when(pl.program_id(1) == 0)
    def _():
        o_ref[...] = jnp.zeros_like(o_ref)

    u = u_ref[...].astype(BF16)
    v = v_ref[...].astype(BF16)
    rb = h_ref.shape[0] // EXPERT_SPLIT
    rows = [pl.ds(b * rb, rb) for b in range(EXPERT_SPLIT)]
    act = _dg(h_ref[rows[0], :], u, _NT)
    for b in range(EXPERT_SPLIT):
        nxt = _dg(h_ref[rows[b + 1], :], u, _NT) if b + 1 < EXPERT_SPLIT else None
        w = (g_ref[rows[b], :] * _gelu(act)).astype(BF16)
        o_ref[rows[b], :] += _dg(w, v, _NN)
        act = nxt


def _token_block(n, cap):
    for tb in range(min(n, cap), 0, -1):
        if n % tb == 0 and tb % (2 * SUBLANES * EXPERT_SPLIT) == 0:
            return tb
    raise ValueError(n)


def _experts(h2, gmat, peer_u, peer_v):
    n = h2.shape[0]
    tb = _token_block(n, EXPERT_ROWS_MAX)
    ec = EXPERT_CHUNK
    return pl.pallas_call(
        _experts_kernel,
        out_shape=jax.ShapeDtypeStruct((n, D_MODEL), F32),
        grid=(n // tb, N_EXPERTS // ec),
        in_specs=[pl.BlockSpec((tb, D_MODEL), lambda i, e: (i, 0)),
                  pl.BlockSpec((tb, ec), lambda i, e: (i, e)),
                  pl.BlockSpec((ec, D_MODEL), lambda i, e: (e, 0)),
                  pl.BlockSpec((ec, D_MODEL), lambda i, e: (e, 0))],
        out_specs=pl.BlockSpec((tb, D_MODEL), lambda i, e: (i, 0), pipeline_mode=pl.Buffered(1)),
        compiler_params=_cparams(("arbitrary", "arbitrary")),
        name="experts",
    )(h2, gmat, peer_u, peer_v)


def _ln2_kernel(x1_ref, ff_ref, g2_ref, lnw_ref, lnb_ref, o_ref, *, alpha):
    ff = ff_ref[...].reshape(x1_ref.shape)
    o_ref[...] = _layernorm(alpha * x1_ref[...] + (1.0 + g2_ref[...]) * ff, lnw_ref[...], lnb_ref[...])


def _ln2(x1, ff, mod, ln_w, ln_b, ns, tb, alpha):
    S, T, _ = x1.shape
    nt = T // tb
    blk = pl.BlockSpec((ns, tb, D_MODEL), lambda s, t: (s, t, 0))
    vec = pl.BlockSpec((1, D_MODEL), lambda s, t: (0, 0))
    if ff.ndim == 2:
        assert ns == 1
        ffspec = pl.BlockSpec((tb, D_MODEL), lambda s, t: (s * nt + t, 0))
    else:
        ffspec = blk
    return pl.pallas_call(
        functools.partial(_ln2_kernel, alpha=alpha),
        out_shape=jax.ShapeDtypeStruct((S, T, D_MODEL), F32),
        grid=(S // ns, nt),
        in_specs=[blk, ffspec, pl.BlockSpec((ns, 1, D_MODEL), lambda s, t: (s, 0, 5)), vec, vec],
        out_specs=blk,
        compiler_params=_cparams(("arbitrary", "arbitrary")),
        name="ln2",
    )(x1, ff, mod, ln_w.reshape(1, -1), ln_b.reshape(1, -1))


def _pair_blockdiag(w):
    n = w.shape[0] // 2
    w = w.reshape(n, 2, HEAD, HEAD)
    z = jnp.zeros((n, HEAD, HEAD), w.dtype)
    top = jnp.concatenate([w[:, 0], z], axis=2)
    bot = jnp.concatenate([z, w[:, 1]], axis=2)
    return jnp.concatenate([top, bot], axis=1)


def _mix_group(x, mod, p_prev, wkv0, lru0, conv0, prm, *, ns_mix, tb_mix, chunk, t_valid, ns_big, tb_big, alpha,
               streams=1, shift_rows=None):
    S, T, _ = x.shape
    if shift_rows is None:
        proj, h_last = _inproj(x, mod, prm['w_in'], ns_big, tb_big)
    else:
        extra = S // T
        x_ext = jnp.concatenate([x, shift_rows.reshape(extra, T, D_MODEL)], axis=0)
        mod_ext = jnp.concatenate([mod, jnp.zeros((extra, 1, 6 * D_MODEL), F32)], axis=0)
        ns_ext = max(d for d in range(1, ns_big + extra // 2 + 1) if (S + extra) % d == 0)
        proj, h_last = _inproj(x_ext, mod_ext, prm['w_in'], ns_ext, tb_big)
        p_prev = proj[S:].reshape(S, C_TOT)[:, :C_A]
        h_last = h_last[:S]
    y, wkv, lru_o, conv_o = _mixer(proj, p_prev[:, None, :], conv0, wkv0, lru0[:, None, :], prm,
                                   ns_mix, tb_mix, chunk, t_valid, streams)
    x1, h2 = _outproj(y, x, mod, prm['w_out'], prm['ln1_w'], prm['ln1_b'], ns_big, tb_big, alpha)
    tv = (t_valid - 1) % SUBLANES
    shift = h_last[:, tv]
    lru = lru_o[:, SUBLANES - 1]
    conv = conv_o[:, tv - (CONV_W - 2):tv + 1]
    return x1, h2, (shift, wkv, lru, conv)


def kernel(x_prompt, x_sample, c_prompt, c_sample, state_shift, state_wkv, state_lru, state_conv, w_ada, b_ada, w_in, mu_shift, w0, w_w2, a0, w_a2, w_g2, k_k, k_a, r_k, lnx_w, lnx_b, conv_w, conv_b, w_gate_a, b_gate_a, w_gate_i, b_gate_i, lru_lambda, w_out, ln1_w, ln1_b, w_q, sub_keys, peer_u, peer_v, ln2_w, ln2_b):
    depth = w_ada.shape[0]
    alpha = (2 * depth) ** 0.25
    bp, tp, _ = x_prompt.shape
    bs, ts, _ = x_sample.shape
    yp = x_prompt
    ys = jnp.pad(x_sample, ((0, 0), (0, SAMPLE_TPAD - ts), (0, 0)))
    head_of = jnp.arange(D_A) // HEAD
    seg = (head_of[:, None] == jnp.arange(LANES)[None, :]).astype(BF16)
    row = lambda a: a.reshape(1, -1)
    outs = [[] for _ in range(8)]
    for l in range(depth):
        zpad = jnp.zeros((LANES - R_W, D_A), F32)
        prm = {
            'w_in': w_in[l].astype(BF16), 'w_out': w_out[l].astype(BF16), 'w_q': w_q[l].astype(BF16),
            'sub_keys': sub_keys[l],
            'mu': row(mu_shift[l]), 'w0': row(w0[l]),
            'ww2': jnp.concatenate([w_w2[l], zpad], axis=0), 'a0': row(a0[l]),
            'wa2': jnp.concatenate([zpad, w_a2[l]], axis=0), 'wg2': w_g2[l].astype(BF16),
            'k_k': row(k_k[l]), 'k_a': row(k_a[l]), 'r_k': row(r_k[l]),
            'lnx_w': row(lnx_w[l]), 'lnx_b': row(lnx_b[l]), 'seg': seg, 'segt': seg.T,
            'conv_w': conv_w[l], 'conv_b': row(conv_b[l]),
            'wga': _pair_blockdiag(w_gate_a[l]).astype(BF16), 'bga': row(b_gate_a[l]),
            'wgi': _pair_blockdiag(w_gate_i[l]).astype(BF16), 'bgi': row(b_gate_i[l]),
            'lam': row(lru_lambda[l]),
            'ln1_w': ln1_w[l], 'ln1_b': ln1_b[l], 'ln2_w': ln2_w[l], 'ln2_b': ln2_b[l],
        }
        c_all = jnp.concatenate([c_prompt, c_sample], axis=0)
        mod = _ada(c_all, w_ada[l], b_ada[l])
        mod_p = mod[:bp, None, :]
        mod_s = mod[bp:, None, :]
        x1p, h2p, st_p = _mix_group(
            yp, mod_p, jnp.zeros((bp, C_A), F32), jnp.zeros((bp, N_HEADS, HEAD, HEAD), F32),
            jnp.zeros((bp, D_B), F32), jnp.zeros((bp, CONV_W - 1, D_B), F32), prm,
            ns_mix=1, tb_mix=ROWS, chunk=16, t_valid=ROWS, ns_big=1, tb_big=min(512, tp), alpha=alpha,
            streams=2 if bp % 2 == 0 else 1)
        x1s, h2s, st_s = _mix_group(
            ys, mod_s, None, state_wkv[l], state_lru[l], state_conv[l], prm,
            ns_mix=ROWS // SAMPLE_TPAD, tb_mix=SAMPLE_TPAD, chunk=SAMPLE_TPAD, t_valid=ts,
            ns_big=min(64, bs), tb_big=SAMPLE_TPAD, alpha=alpha, shift_rows=state_shift[l])
        h2s = h2s.reshape(bs, SAMPLE_TPAD, D_MODEL)[:, :ts].reshape(bs * ts, D_MODEL)
        h2 = jnp.concatenate([h2p, h2s], axis=0)
        gmat = _route(h2, prm['w_q'], prm['sub_keys'])
        ff = _experts(h2, gmat, peer_u[l], peer_v[l])
        yp = _ln2(x1p, ff, mod_p, ln2_w[l], ln2_b[l], 1, min(512, tp), alpha)
        ys_real = _ln2(x1s[:, :ts], ff[bp * tp:].reshape(bs, ts, D_MODEL), mod_s, ln2_w[l], ln2_b[l],
                       min(64, bs), ts, alpha)
        ys = jnp.pad(ys_real, ((0, 0), (0, SAMPLE_TPAD - ts), (0, 0)))
        for i, a in enumerate(st_p + st_s):
            outs[i].append(a)
    return (yp, ys[:, :ts]) + tuple(jnp.stack(o) for o in outs)
```

```python
import functools

import jax
import jax.numpy as jnp
from jax import lax
from jax.experimental import pallas as pl
from jax.experimental.pallas import tpu as pltpu

F32 = jnp.float32
BF16 = jnp.bfloat16

D_MODEL = 2048
D_A = 1024
D_B = 1024
HEAD = 64
N_HEADS = D_A // HEAD
N_PAIRS = N_HEADS // 2
R_W = 64
R_A = 64
R_G = 128
C_A = 3 * D_A + R_W + R_A + R_G
C_TOT = C_A + 2 * D_B
CONV_W = 4
C_LRU = 8.0
P_HEADS = 8
N_KEYS = 128
N_EXPERTS = N_KEYS * N_KEYS
TOPK = 16
D_QH = 128
LN_EPS = 1e-5
GN_EPS = 64e-5

LANES = 128
SUBLANES = 8
ROWS = 128
SAMPLE_TPAD = 8
VMEM_LIMIT = 56 * 1024 * 1024


def _cparams(sem):
    return pltpu.CompilerParams(dimension_semantics=sem, vmem_limit_bytes=VMEM_LIMIT)


def _split2(x):
    hi = x.astype(BF16)
    lo = (x - hi.astype(F32)).astype(BF16)
    return hi, lo


_NN = (((1,), (0,)), ((), ()))
_NT = (((1,), (1,)), ((), ()))
_TN = (((0,), (0,)), ((), ()))


def _dg(a, b, dims):
    return lax.dot_general(a, b, dims, preferred_element_type=F32)


def _dot1(a, b, dims=_NN):
    return _dg(a.astype(BF16), b.astype(BF16), dims)


def _dot3(a, b, dims=_NN):
    ah, al = _split2(a)
    bh, bl = _split2(b)
    return _dg(ah, bh, dims) + (_dg(ah, bl, dims) + _dg(al, bh, dims))


_state_dot = _dot1


def _dot_exact_rhs(a, b_exact, dims=_NN):
    a1, a2 = _split2(a)
    return _dg(a1, b_exact, dims) + _dg(a2, b_exact, dims)


def _dot_exact_lhs(a_exact, b, dims=_NN):
    b1, b2 = _split2(b)
    return _dg(a_exact, b1, dims) + _dg(a_exact, b2, dims)


def _sigmoid(x):
    return 1.0 / (1.0 + jnp.exp(-x))


def _softplus(x):
    return jnp.maximum(x, 0.0) + jnp.log(1.0 + jnp.exp(-jnp.abs(x)))


def _gelu(x):
    return 0.5 * x * (1.0 + jnp.tanh(0.7978845608028654 * (x + 0.044715 * (x * x * x))))


def _iota(shape, dim):
    return lax.broadcasted_iota(jnp.int32, shape, dim)


def _ada_kernel(c_ref, w_ref, b_ref, o_ref):
    c = c_ref[...]
    s = c * _sigmoid(c)
    o_ref[...] = _dot3(s, w_ref[...]) + b_ref[...]


def _ada(c, w_ada, b_ada):
    n = c.shape[0]
    tn = 1024
    return pl.pallas_call(
        _ada_kernel,
        out_shape=jax.ShapeDtypeStruct((n, 6 * D_MODEL), F32),
        grid=(6 * D_MODEL // tn,),
        in_specs=[pl.BlockSpec((n, D_MODEL), lambda j: (0, 0)),
                  pl.BlockSpec((D_MODEL, tn), lambda j: (0, j)),
                  pl.BlockSpec((1, tn), lambda j: (0, j))],
        out_specs=pl.BlockSpec((n, tn), lambda j: (0, j)),
        compiler_params=_cparams(("arbitrary",)),
        name="ada",
    )(c, w_ada, b_ada.reshape(1, -1))


def _inproj_kernel(x_ref, sc_ref, sh_ref, w_ref, o_ref, h_ref):
    h = x_ref[...] * (1.0 + sc_ref[...]) + sh_ref[...]
    ns, tb, _ = h.shape
    h_ref[...] = h[:, tb - SUBLANES:, :]
    p = _dot1(h.reshape(ns * tb, D_MODEL), w_ref[...])
    o_ref[...] = p.reshape(ns, tb, p.shape[-1])


def _inproj(x, mod, w_bf16, ns, tb):
    S, T, _ = x.shape
    ncol = w_bf16.shape[1]
    tn = ncol // 3
    proj, h_last = pl.pallas_call(
        _inproj_kernel,
        out_shape=(jax.ShapeDtypeStruct((S, T, ncol), F32),
                   jax.ShapeDtypeStruct((3, S, SUBLANES, D_MODEL), F32)),
        grid=(3, S // ns, T // tb),
        in_specs=[pl.BlockSpec((ns, tb, D_MODEL), lambda j, s, t: (s, t, 0)),
                  pl.BlockSpec((ns, 1, D_MODEL), lambda j, s, t: (s, 0, 1)),
                  pl.BlockSpec((ns, 1, D_MODEL), lambda j, s, t: (s, 0, 0)),
                  pl.BlockSpec((D_MODEL, tn), lambda j, s, t: (0, j))],
        out_specs=(pl.BlockSpec((ns, tb, tn), lambda j, s, t: (s, t, j)),
                   pl.BlockSpec((None, ns, SUBLANES, D_MODEL), lambda j, s, t: (j, s, 0, 0))),
        compiler_params=_cparams(("arbitrary", "arbitrary", "arbitrary")),
        name="inproj",
    )(x, mod, mod, w_bf16)
    return proj, h_last[0]


def _pair_tile(ref, s, pr):
    zero_blk = jnp.zeros((HEAD, HEAD), F32)
    top = jnp.concatenate([ref[s, 2 * pr], zero_blk], axis=1)
    bot = jnp.concatenate([zero_blk, ref[s, 2 * pr + 1]], axis=1)
    return jnp.concatenate([top, bot], axis=0)


def _mixer_stream(proj_ref, s0_ref,
                  mu_ref, w0_ref, ww2_ref, a0_ref, wa2_ref, wg2_ref, kk_ref, ka_ref, rk_ref,
                  lnw_ref, lnb_ref, seg_ref, segt_ref,
                  cw_ref, cb_ref, wga_ref, bga_ref, wgi_ref, bgi_ref, lam_ref,
                  y_ref, sout_ref,
                  prev_ref, hst_ref, st_ref,
                  bp_ref, rp_ref, x2_ref, y2_ref, v_ref, kb_ref, ab_ref, wt_ref, yacc_ref,
                  *, ns, tb, chunk, t_valid, carry_state):
    R = ns * tb
    pair_tile = _pair_tile

    cur = proj_ref[...]
    ext = jnp.concatenate([prev_ref[...], cur[:, :, :C_A + D_B]], axis=1)
    prev_ref[...] = cur[:, tb - SUBLANES:, :C_A + D_B]

    row = _iota((R, 1), 0)
    rowin = row % tb
    valid = rowin < t_valid

    pa = cur[:, :, :C_A].reshape(R, C_A)
    shifted = ext[:, SUBLANES - 1:SUBLANES - 1 + tb, :C_A].reshape(R, C_A)
    p = pa + (shifted - pa) * mu_ref[...]
    r = p[:, :D_A]
    k = p[:, D_A:2 * D_A]
    v = p[:, 2 * D_A:3 * D_A]
    xwa = p[:, 3 * D_A:3 * D_A + LANES]
    xg = p[:, 3 * D_A + LANES:]
    w_log = -_softplus(-(w0_ref[...] + _dot3(jnp.tanh(xwa), ww2_ref[...]))) - 0.5
    logw = -jnp.exp(w_log)
    a = _sigmoid(a0_ref[...] + _dot3(xwa, wa2_ref[...]))
    g = _dot1(_sigmoid(xg), wg2_ref[...])
    yield 900
    seg = seg_ref[...]
    segt = segt_ref[...]

    def seg_sum(x):
        return _dot_exact_rhs(_dot_exact_rhs(x, seg), segt)

    kk = k * kk_ref[...]
    kk = kk / jnp.maximum(jnp.sqrt(seg_sum(kk * kk)), 1e-12)
    yield 600
    k = k * (1.0 + (a - 1.0) * ka_ref[...])
    kka = kk * a
    bonus = seg_sum(r * k * rk_ref[...]) * v
    yield 600
    if t_valid < tb:
        logw = jnp.where(valid, logw, 0.0)
        r = jnp.where(valid, r, 0.0)
        k = jnp.where(valid, k, 0.0)
        v = jnp.where(valid, v, 0.0)
        kk = jnp.where(valid, kk, 0.0)
        kka = jnp.where(valid, kka, 0.0)

    ri = _iota((R, R), 0)
    ci = _iota((R, R), 1)
    same = (ri // chunk) == (ci // chunk)
    incl = same & (ci <= ri)
    strict = same & (ci < ri)
    lmat = jnp.concatenate([jnp.where(incl, 1.0, 0.0), jnp.where(same, 1.0, 0.0)], axis=0).astype(BF16)
    cums = _dot_exact_lhs(lmat, logw)
    cum = cums[:R]
    tot = cums[R:]
    yield 500
    e_prev = jnp.exp(cum - logw)
    e_neg = jnp.exp(-cum)
    e_pos = jnp.exp(cum)
    e_rem = jnp.exp(tot - cum)
    beta = kk * e_prev
    alpha = kka * e_neg
    kappa = k * e_neg
    rho = r * e_pos
    yield 600
    v_ref[...] = v
    kb_ref[...] = k * e_rem
    ab_ref[...] = -(kka * e_rem)
    wt_ref[...] = jnp.exp(tot)
    yield 400

    lane = _iota((1, LANES), 1)
    masks = (lane < HEAD, lane >= HEAD)
    eye = jnp.where(ri == ci, 1.0, 0.0)
    nsq = {16: 3, 8: 2}[chunk]
    heads = [(pr, m) for pr in range(N_PAIRS) for m in masks]
    psl = lambda pr: slice(pr * LANES, (pr + 1) * LANES)
    bms = [jnp.where(m, beta[:, psl(pr)], 0.0) for pr, m in heads]
    rms = [jnp.where(m, rho[:, psl(pr)], 0.0) for pr, m in heads]
    yield 300
    gms = []
    for (pr, m), bm, rm in zip(heads, bms, rms):
        gms.append(_dot1(jnp.concatenate([bm, rm], axis=0),
                         jnp.concatenate([alpha[:, psl(pr)], kappa[:, psl(pr)]], axis=0), _NT))
        yield 70
    l_bas = [jnp.where(strict, gm[:R, :R], 0.0) for gm in gms]
    m_ras = [jnp.where(incl, gm[R:, :R], 0.0) for gm in gms]
    xy1s = []
    for (pr, m), gm in zip(heads, gms):
        xy1s.append(_dot1(jnp.concatenate([jnp.where(strict, gm[:R, R:], 0.0),
                                           jnp.where(incl, gm[R:, R:], 0.0)], axis=0), v[:, psl(pr)]))
        yield 70
    pws = []
    for l in l_bas:
        pws.append(_dot1(l, l))
        yield 40

    def lru_steps():
        for pr in range(N_PAIRS):
            sl = psl(pr)
            bsl = slice(C_A + pr * LANES, C_A + (pr + 1) * LANES)
            gsl = slice(C_A + D_B + pr * LANES, C_A + D_B + (pr + 1) * LANES)
            xc = cb_ref[:, sl]
            for j in range(CONV_W):
                off = SUBLANES - (CONV_W - 1) + j
                xc = xc + ext[:, off:off + tb, bsl].reshape(R, LANES) * cw_ref[j:j + 1, sl]
            yield
            rg = _sigmoid(_dot1(xc, wga_ref[pr]) + bga_ref[:, sl])
            ig = _sigmoid(_dot1(xc, wgi_ref[pr]) + bgi_ref[:, sl])
            yield
            log_a = -C_LRU * rg * _softplus(-lam_ref[:, sl])
            av = jnp.exp(log_a)
            bv = jnp.sqrt(1.0 - jnp.exp(2.0 * log_a)) * (ig * xc)
            if t_valid < tb:
                av = jnp.where(valid, av, 1.0)
                bv = jnp.where(valid, bv, 0.0)
            yield
            d = 1
            while d < tb:
                take = rowin >= d
                a_sh = jnp.where(take, pltpu.roll(av, d, axis=0), 1.0)
                b_sh = jnp.where(take, pltpu.roll(bv, d, axis=0), 0.0)
                bv = av * b_sh + bv
                av = av * a_sh
                d *= 2
                yield
            h0 = jnp.broadcast_to(hst_ref[:, SUBLANES - 1:SUBLANES, sl], (ns, tb, LANES)).reshape(R, LANES)
            h3 = (av * h0 + bv).reshape(ns, tb, LANES)
            hst_ref[:, :, sl] = h3[:, tb - SUBLANES:, :]
            gb = cur[:, :, gsl].reshape(R, LANES)
            y_ref[:, :, D_A + pr * LANES:D_A + (pr + 1) * LANES] = h3 * _gelu(gb).reshape(ns, tb, LANES)
            yield

    lru = lru_steps()
    minvs = [eye - l for l in l_bas]
    for i in range(nsq):
        nxt = []
        for mi, pw in zip(minvs, pws):
            nxt.append(mi + _dot1(mi, pw))
            next(lru, None)
            yield 60
        minvs = nxt
        if i + 1 < nsq:
            nxt = []
            for pw in pws:
                nxt.append(_dot1(pw, pw))
                next(lru, None)
                yield 60
            pws = nxt
    for _ in lru:
        yield 40
    t1s = []
    for mi, xy1, bm in zip(minvs, xy1s, bms):
        t1s.append(_dot1(mi, jnp.concatenate([xy1[:R], bm], axis=1)))
        yield 50
    t2s = []
    for m_ra, t1 in zip(m_ras, t1s):
        t2s.append(_dot1(m_ra, t1))
        yield 50
    for pr in range(N_PAIRS):
        h0, h1 = 2 * pr, 2 * pr + 1
        sl = psl(pr)
        m1 = masks[1]
        bp_ref[:, sl] = t1s[h0][:, LANES:] + t1s[h1][:, LANES:]
        rp_ref[:, sl] = (rms[h0] - t2s[h0][:, LANES:]) + (rms[h1] - t2s[h1][:, LANES:])
        x2_ref[:, sl] = jnp.where(m1, t1s[h1][:, :LANES], t1s[h0][:, :LANES])
        y2_ref[:, sl] = jnp.where(m1, xy1s[h1][R:] - t2s[h1][:, :LANES], xy1s[h0][R:] - t2s[h0][:, :LANES])

    bi = _iota((LANES, LANES), 0)
    bj = _iota((LANES, LANES), 1)
    bd = (bi < HEAD) == (bj < HEAD)
    nck = tb // chunk
    sites = [(s, c, pr) for s in range(ns) for c in range(nck) for pr in range(N_PAIRS)]
    rows_of = lambda s, c: slice(s * tb + c * chunk, s * tb + (c + 1) * chunk)
    thetas = {}
    psis = {}
    for s, c, pr in sites:
        rs, sl = rows_of(s, c), psl(pr)
        thetas[s, c, pr] = jnp.where(bd, _state_dot(bp_ref[rs, sl], ab_ref[rs, sl], _TN), 0.0)
        vx = jnp.concatenate([v_ref[rs, sl], x2_ref[rs, sl]], axis=0)
        ka = jnp.concatenate([kb_ref[rs, sl], ab_ref[rs, sl]], axis=0)
        psis[s, c, pr] = jnp.where(bd, _state_dot(vx, ka, _TN), 0.0)
        yield 60
    for s in range(ns):
        sps = [st_ref[s, pr] if carry_state else pair_tile(s0_ref, s, pr) for pr in range(N_PAIRS)]
        for c in range(nck):
            rs = rows_of(s, c)
            for pr in range(N_PAIRS):
                yacc_ref[rs, psl(pr)] = _state_dot(rp_ref[rs, psl(pr)], sps[pr], _NT) + y2_ref[rs, psl(pr)]
            yield 100
            nxt = []
            for pr in range(N_PAIRS):
                nxt.append(sps[pr] * wt_ref[rs.start:rs.start + 1, psl(pr)]
                           + (_state_dot(sps[pr], thetas[s, c, pr]) + psis[s, c, pr]))
                yield 50
            sps = nxt
        for pr in range(N_PAIRS):
            if carry_state:
                st_ref[s, pr] = sps[pr]
            else:
                sout_ref[s, 2 * pr] = sps[pr][:HEAD, :HEAD]
                sout_ref[s, 2 * pr + 1] = sps[pr][HEAD:, HEAD:]
        yield 50

    y = yacc_ref[...]
    mean = seg_sum(y) * (1.0 / HEAD)
    yc = y - mean
    yield 400
    var = seg_sum(yc * yc) * (1.0 / HEAD)
    yn = yc * lax.rsqrt(var + GN_EPS) * lnw_ref[...] + lnb_ref[...]
    ya = (yn + bonus) * g
    y_ref[:, :, :D_A] = ya.reshape(ns, tb, D_A)
    yield 600


def _mixer_kernel(proj_ref, pprev_ref, conv0_ref, s0_ref, lru0_ref, *refs, ns, tb, chunk, t_valid, carry_state, streams):
    n_par = 20
    params = refs[:n_par]
    y_ref, sout_ref, lruo_ref, convo_ref, prev_ref, hst_ref, st_ref = refs[n_par:n_par + 7]
    slabs = refs[n_par + 7:]
    t = pl.program_id(1)

    @pl.when(t == 0)
    def _():
        prev_ref[...] = jnp.zeros_like(prev_ref)
        prev_ref[:, SUBLANES - 1:, :C_A] = pprev_ref[...]
        prev_ref[:, SUBLANES - (CONV_W - 1):, C_A:] = conv0_ref[...]
        hst_ref[...] = jnp.broadcast_to(lru0_ref[...], hst_ref.shape)
        if carry_state:
            for q in range(streams):
                for pr in range(N_PAIRS):
                    st_ref[q, pr] = _pair_tile(s0_ref, q, pr)

    def make(q):
        sq = pl.ds(q * ns, ns)
        return _mixer_stream(
            proj_ref.at[sq], s0_ref.at[sq], *params, y_ref.at[sq], sout_ref.at[sq],
            prev_ref.at[sq], hst_ref.at[sq], st_ref.at[pl.ds(q, 1)], *[r.at[q] for r in slabs],
            ns=ns, tb=tb, chunk=chunk, t_valid=t_valid, carry_state=carry_state)

    gens = [make(q) for q in range(streams)]
    clock = [0.0] * streams
    live = list(range(streams))
    while live:
        q = min(live, key=lambda i: clock[i])
        cost = next(gens[q], None)
        if cost is None:
            live.remove(q)
        else:
            clock[q] += cost

    @pl.when(t == pl.num_programs(1) - 1)
    def _():
        lruo_ref[...] = hst_ref[...]
        convo_ref[...] = proj_ref[:, tb - SUBLANES:, C_A:C_A + D_B]
        if carry_state:
            for q in range(streams):
                for pr in range(N_PAIRS):
                    tile = st_ref[q, pr]
                    sout_ref[q, 2 * pr] = tile[:HEAD, :HEAD]
                    sout_ref[q, 2 * pr + 1] = tile[HEAD:, HEAD:]


def _mixer(proj, p_prev, conv0, s0, lru0, prm, ns, tb, chunk, t_valid, streams=1):
    S, T = s0.shape[0], proj.shape[1]
    R = ns * tb
    assert R == ROWS
    carry_state = T // tb > 1
    assert ns == 1 or not carry_state
    kern = functools.partial(_mixer_kernel, ns=ns, tb=tb, chunk=chunk, t_valid=t_valid, carry_state=carry_state,
                             streams=streams)
    nb = ns * streams

    def full(a):
        nd = a.ndim
        return pl.BlockSpec(a.shape, lambda s, t, nd=nd: (0,) * nd)

    params = [prm[n] for n in ('mu', 'w0', 'ww2', 'a0', 'wa2', 'wg2', 'k_k', 'k_a', 'r_k', 'lnx_w', 'lnx_b',
                               'seg', 'segt', 'conv_w', 'conv_b', 'wga', 'bga', 'wgi', 'bgi', 'lam')]
    slab = lambda w: pltpu.VMEM((streams, R, w), F32)
    return pl.pallas_call(
        kern,
        out_shape=(jax.ShapeDtypeStruct((S, T, D_A + D_B), F32),
                   jax.ShapeDtypeStruct((S, N_HEADS, HEAD, HEAD), F32),
                   jax.ShapeDtypeStruct((S, SUBLANES, D_B), F32),
                   jax.ShapeDtypeStruct((S, SUBLANES, D_B), F32)),
        grid=(S // nb, T // tb),
        in_specs=[pl.BlockSpec((nb, tb, C_TOT), lambda s, t: (s, t, 0)),
                  pl.BlockSpec((nb, 1, C_A), lambda s, t: (s, 0, 0)),
                  pl.BlockSpec((nb, CONV_W - 1, D_B), lambda s, t: (s, 0, 0)),
                  pl.BlockSpec((nb, N_HEADS, HEAD, HEAD), lambda s, t: (s, 0, 0, 0),
                               pipeline_mode=pl.Buffered(1)),
                  pl.BlockSpec((nb, 1, D_B), lambda s, t: (s, 0, 0))]
                 + [full(a) for a in params],
        out_specs=(pl.BlockSpec((nb, tb, D_A + D_B), lambda s, t: (s, t, 0)),
                   pl.BlockSpec((nb, N_HEADS, HEAD, HEAD), lambda s, t: (s, 0, 0, 0),
                                pipeline_mode=pl.Buffered(1)),
                   pl.BlockSpec((nb, SUBLANES, D_B), lambda s, t: (s, 0, 0)),
                   pl.BlockSpec((nb, SUBLANES, D_B), lambda s, t: (s, 0, 0))),
        scratch_shapes=[pltpu.VMEM((nb, SUBLANES, C_A + D_B), F32),
                        pltpu.VMEM((nb, SUBLANES, D_B), F32),
                        pltpu.VMEM((streams, N_PAIRS, LANES, LANES), F32)]
                       + [slab(D_A) for _ in range(9)],
        compiler_params=_cparams(("arbitrary", "arbitrary")),
        name="mixer",
    )(proj, p_prev, conv0, s0, lru0, *params)


def _layernorm(x, w, b):
    mean = jnp.mean(x, axis=-1, keepdims=True)
    xc = x - mean
    var = jnp.mean(xc * xc, axis=-1, keepdims=True)
    return xc * lax.rsqrt(var + LN_EPS) * w + b


def _outproj_kernel(y_ref, x_ref, g1_ref, sc2_ref, sh2_ref, w_ref, lnw_ref, lnb_ref, x1_ref, h2_ref, *, alpha):
    ns, tb, _ = y_ref.shape
    mix = _dot1(y_ref[...].reshape(ns * tb, D_MODEL), w_ref[...]).reshape(ns, tb, D_MODEL)
    x1 = _layernorm(alpha * x_ref[...] + (1.0 + g1_ref[...]) * mix, lnw_ref[...], lnb_ref[...])
    x1_ref[...] = x1
    h2 = x1 * (1.0 + sc2_ref[...]) + sh2_ref[...]
    h2_ref[...] = h2.reshape(ns * tb, D_MODEL).astype(BF16)


def _outproj(y, x, mod, w_bf16, ln_w, ln_b, ns, tb, alpha):
    S, T, _ = x.shape
    nt = T // tb
    blk = pl.BlockSpec((ns, tb, D_MODEL), lambda s, t: (s, t, 0))
    modspec = lambda i: pl.BlockSpec((ns, 1, D_MODEL), lambda s, t, i=i: (s, 0, i))
    vec = pl.BlockSpec((1, D_MODEL), lambda s, t: (0, 0))
    return pl.pallas_call(
        functools.partial(_outproj_kernel, alpha=alpha),
        out_shape=(jax.ShapeDtypeStruct((S, T, D_MODEL), F32), jax.ShapeDtypeStruct((S * T, D_MODEL), BF16)),
        grid=(S // ns, nt),
        in_specs=[blk, blk, modspec(2), modspec(4), modspec(3),
                  pl.BlockSpec((D_MODEL, D_MODEL), lambda s, t: (0, 0)), vec, vec],
        out_specs=(blk, pl.BlockSpec((ns * tb, D_MODEL), lambda s, t: (s * nt + t, 0))),
        compiler_params=_cparams(("arbitrary", "arbitrary")),
        name="outproj",
    )(y, x, mod, mod, mod, w_bf16, ln_w.reshape(1, -1), ln_b.reshape(1, -1))


def _topk_rows(s, ridx, n_rows):
    out_i = _iota((TOPK, s.shape[1]), 0)
    vals = jnp.zeros((TOPK, s.shape[1]), F32)
    idxs = jnp.zeros((TOPK, s.shape[1]), F32)
    for it in range(TOPK):
        m = jnp.max(s, axis=0, keepdims=True)
        idx = jnp.min(jnp.where(s == m, ridx, float(n_rows)), axis=0, keepdims=True)
        vals = jnp.where(out_i == it, m, vals)
        idxs = jnp.where(out_i == it, idx, idxs)
        s = jnp.where(ridx == idx, -jnp.inf, s)
    return vals, idxs


def _route_kernel(h_ref, wq_ref, sk_ref, g_ref, gate_ref, e1_ref, e2_ref):
    R = ROWS
    q = _dot1(h_ref[...], wq_ref[...])
    sk0 = sk_ref[0]
    sk1 = sk_ref[1]
    gates, e1s, e2s = [], [], []
    key_rows = _iota((N_KEYS, R), 0).astype(F32)
    r16 = _iota((TOPK, R), 0).astype(F32)
    r8 = _iota((SUBLANES, R), 0).astype(F32)
    ea_rank = jnp.where(r8 < 3, 2.0, jnp.where(r8 < 5, 3.0, 4.0))
    eb_rank = jnp.where((r8 == 0) | (r8 == 3) | (r8 == 5), 2.0, jnp.where((r8 == 1) | (r8 == 4), 3.0, 4.0))
    flat = jnp.concatenate([r16, TOPK + r8, TOPK * r16, TOPK * r8 + 1.0, TOPK * ea_rank + eb_rank], axis=0)
    neg = -jnp.inf
    for hd in range(P_HEADS):
        base = hd * 2 * D_QH
        s1 = _dot1(sk0, q[:, base:base + D_QH], _NT)
        s2 = _dot1(sk1, q[:, base + D_QH:base + 2 * D_QH], _NT)
        v1, i1 = _topk_rows(s1, key_rows, N_KEYS)
        v2, i2 = _topk_rows(s2, key_rows, N_KEYS)
        ea = jnp.where(r8 < 3, v1[2:3], jnp.where(r8 < 5, v1[3:4], v1[4:5]))
        eb = jnp.where(eb_rank == 2.0, v2[2:3], jnp.where(eb_rank == 3.0, v2[3:4], v2[4:5]))
        cand = jnp.concatenate([
            v1[0:1] + v2,
            v1[1:2] + v2[:SUBLANES],
            jnp.where(r16 >= 2, v1 + v2[0:1], neg),
            jnp.where(r8 >= 2, v1[:SUBLANES] + v2[1:2], neg),
            jnp.where(r8 < 6, ea + eb, neg),
        ], axis=0)
        sv, ci = _topk_rows(cand, flat, TOPK * TOPK)
        ca = jnp.floor(ci * (1.0 / TOPK))
        cb = ci - TOPK * ca
        e1 = jnp.zeros((TOPK, R), F32)
        e2 = jnp.zeros((TOPK, R), F32)
        for a in range(TOPK):
            e1 = jnp.where(ca == float(a), i1[a:a + 1, :], e1)
            e2 = jnp.where(cb == float(a), i2[a:a + 1, :], e2)
        ex = jnp.exp(sv - jnp.max(sv, axis=0, keepdims=True))
        gates.append(ex / jnp.sum(ex, axis=0, keepdims=True))
        e1s.append(e1)
        e2s.append(e2)
    gate_ref[...] = jnp.concatenate(gates, axis=0).T
    e1_ref[...] = jnp.concatenate(e1s, axis=0).T
    e2_ref[...] = jnp.concatenate(e2s, axis=0).T
    key_i = _iota((N_KEYS, P_HEADS * TOPK), 0).astype(F32)

    def per_octet(o, carry):
        base = pl.multiple_of(o * SUBLANES, SUBLANES)
        tiles = []
        for j in range(SUBLANES):
            sel1 = key_i == e1_ref[pl.ds(base + j, 1), :]
            sel2 = key_i == e2_ref[pl.ds(base + j, 1), :]
            m1 = jnp.where(sel1, gate_ref[pl.ds(base + j, 1), :], 0.0)
            m2 = jnp.where(sel2, 1.0, 0.0)
            tiles.append(_dot1(m1, m2, _NT))
        by_key = jnp.swapaxes(jnp.stack(tiles, axis=0), 0, 1)
        for i1v in range(N_KEYS):
            g_ref[pl.ds(base, SUBLANES), i1v * N_KEYS:(i1v + 1) * N_KEYS] = by_key[i1v]
        return carry

    lax.fori_loop(0, R // SUBLANES, per_octet, 0, unroll=8)


def _route(h2, wq_bf16, sub_keys):
    n = h2.shape[0]
    assert n % ROWS == 0
    return pl.pallas_call(
        _route_kernel,
        out_shape=jax.ShapeDtypeStruct((n, N_EXPERTS), F32),
        grid=(n // ROWS,),
        in_specs=[pl.BlockSpec((ROWS, D_MODEL), lambda i: (i, 0)),
                  pl.BlockSpec((D_MODEL, D_MODEL), lambda i: (0, 0)),
                  pl.BlockSpec((2, N_KEYS, D_QH), lambda i: (0, 0, 0))],
        out_specs=pl.BlockSpec((ROWS, N_EXPERTS), lambda i: (i, 0)),
        scratch_shapes=[pltpu.VMEM((ROWS, P_HEADS * TOPK), F32) for _ in range(3)],
        compiler_params=_cparams(("arbitrary",)),
        name="route",
    )(h2, wq_bf16, sub_keys)


EXPERT_CHUNK = 512
EXPERT_ROWS_MAX = 1152


EXPERT_SPLIT = 4


def _experts_kernel(h_ref, g_ref, u_ref, v_ref, o_ref):
    @pl.when(pl.program_id(1) == 0)
    def _():
        o_ref[...] = jnp.zeros_like(o_ref)

    u = u_ref[...].astype(BF16)
    v = v_ref[...].astype(BF16)
    rb = h_ref.shape[0] // EXPERT_SPLIT
    rows = [pl.ds(b * rb, rb) for b in range(EXPERT_SPLIT)]
    act = _dg(h_ref[rows[0], :], u, _NT)
    for b in range(EXPERT_SPLIT):
        nxt = _dg(h_ref[rows[b + 1], :], u, _NT) if b + 1 < EXPERT_SPLIT else None
        w = (g_ref[rows[b], :] * _gelu(act)).astype(BF16)
        o_ref[rows[b], :] += _dg(w, v, _NN)
        act = nxt


def _token_block(n, cap):
    for tb in range(min(n, cap), 0, -1):
        if n % tb == 0 and tb % (2 * SUBLANES * EXPERT_SPLIT) == 0:
            return tb
    raise ValueError(n)


def _experts(h2, gmat, peer_u, peer_v):
    n = h2.shape[0]
    tb = _token_block(n, EXPERT_ROWS_MAX)
    ec = EXPERT_CHUNK
    return pl.pallas_call(
        _experts_kernel,
        out_shape=jax.ShapeDtypeStruct((n, D_MODEL), F32),
        grid=(n // tb, N_EXPERTS // ec),
        in_specs=[pl.BlockSpec((tb, D_MODEL), lambda i, e: (i, 0)),
                  pl.BlockSpec((tb, ec), lambda i, e: (i, e)),
                  pl.BlockSpec((ec, D_MODEL), lambda i, e: (e, 0)),
                  pl.BlockSpec((ec, D_MODEL), lambda i, e: (e, 0))],
        out_specs=pl.BlockSpec((tb, D_MODEL), lambda i, e: (i, 0), pipeline_mode=pl.Buffered(1)),
        compiler_params=_cparams(("arbitrary", "arbitrary")),
        name="experts",
    )(h2, gmat, peer_u, peer_v)


def _ln2_kernel(x1_ref, ff_ref, g2_ref, lnw_ref, lnb_ref, o_ref, *, alpha):
    ff = ff_ref[...].reshape(x1_ref.shape)
    o_ref[...] = _layernorm(alpha * x1_ref[...] + (1.0 + g2_ref[...]) * ff, lnw_ref[...], lnb_ref[...])


def _ln2(x1, ff, mod, ln_w, ln_b, ns, tb, alpha):
    S, T, _ = x1.shape
    nt = T // tb
    blk = pl.BlockSpec((ns, tb, D_MODEL), lambda s, t: (s, t, 0))
    vec = pl.BlockSpec((1, D_MODEL), lambda s, t: (0, 0))
    if ff.ndim == 2:
        assert ns == 1
        ffspec = pl.BlockSpec((tb, D_MODEL), lambda s, t: (s * nt + t, 0))
    else:
        ffspec = blk
    return pl.pallas_call(
        functools.partial(_ln2_kernel, alpha=alpha),
        out_shape=jax.ShapeDtypeStruct((S, T, D_MODEL), F32),
        grid=(S // ns, nt),
        in_specs=[blk, ffspec, pl.BlockSpec((ns, 1, D_MODEL), lambda s, t: (s, 0, 5)), vec, vec],
        out_specs=blk,
        compiler_params=_cparams(("arbitrary", "arbitrary")),
        name="ln2",
    )(x1, ff, mod, ln_w.reshape(1, -1), ln_b.reshape(1, -1))


def _pair_blockdiag(w):
    n = w.shape[0] // 2
    w = w.reshape(n, 2, HEAD, HEAD)
    z = jnp.zeros((n, HEAD, HEAD), w.dtype)
    top = jnp.concatenate([w[:, 0], z], axis=2)
    bot = jnp.concatenate([z, w[:, 1]], axis=2)
    return jnp.concatenate([top, bot], axis=1)


def _mix_group(x, mod, p_prev, wkv0, lru0, conv0, prm, *, ns_mix, tb_mix, chunk, t_valid, ns_big, tb_big, alpha,
               streams=1, shift_rows=None):
    S, T, _ = x.shape
    if shift_rows is None:
        proj, h_last = _inproj(x, mod, prm['w_in'], ns_big, tb_big)
    else:
        extra = S // T
        x_ext = jnp.concatenate([x, shift_rows.reshape(extra, T, D_MODEL)], axis=0)
        mod_ext = jnp.concatenate([mod, jnp.zeros((extra, 1, 6 * D_MODEL), F32)], axis=0)
        ns_ext = max(d for d in range(1, ns_big + extra // 2 + 1) if (S + extra) % d == 0)
        proj, h_last = _inproj(x_ext, mod_ext, prm['w_in'], ns_ext, tb_big)
        p_prev = proj[S:].reshape(S, C_TOT)[:, :C_A]
        h_last = h_last[:S]
    y, wkv, lru_o, conv_o = _mixer(proj, p_prev[:, None, :], conv0, wkv0, lru0[:, None, :], prm,
                                   ns_mix, tb_mix, chunk, t_valid, streams)
    x1, h2 = _outproj(y, x, mod, prm['w_out'], prm['ln1_w'], prm['ln1_b'], ns_big, tb_big, alpha)
    tv = (t_valid - 1) % SUBLANES
    shift = h_last[:, tv]
    lru = lru_o[:, SUBLANES - 1]
    conv = conv_o[:, tv - (CONV_W - 2):tv + 1]
    return x1, h2, (shift, wkv, lru, conv)


def kernel(x_prompt, x_sample, c_prompt, c_sample, state_shift, state_wkv, state_lru, state_conv, w_ada, b_ada, w_in, mu_shift, w0, w_w2, a0, w_a2, w_g2, k_k, k_a, r_k, lnx_w, lnx_b, conv_w, conv_b, w_gate_a, b_gate_a, w_gate_i, b_gate_i, lru_lambda, w_out, ln1_w, ln1_b, w_q, sub_keys, peer_u, peer_v, ln2_w, ln2_b):
    depth = w_ada.shape[0]
    alpha = (2 * depth) ** 0.25
    bp, tp, _ = x_prompt.shape
    bs, ts, _ = x_sample.shape
    yp = x_prompt
    ys = jnp.pad(x_sample, ((0, 0), (0, SAMPLE_TPAD - ts), (0, 0)))
    head_of = jnp.arange(D_A) // HEAD
    seg = (head_of[:, None] == jnp.arange(LANES)[None, :]).astype(BF16)
    row = lambda a: a.reshape(1, -1)
    outs = [[] for _ in range(8)]
    for l in range(depth):
        zpad = jnp.zeros((LANES - R_W, D_A), F32)
        prm = {
            'w_in': w_in[l].astype(BF16), 'w_out': w_out[l].astype(BF16), 'w_q': w_q[l].astype(BF16),
            'sub_keys': sub_keys[l],
            'mu': row(mu_shift[l]), 'w0': row(w0[l]),
            'ww2': jnp.concatenate([w_w2[l], zpad], axis=0), 'a0': row(a0[l]),
            'wa2': jnp.concatenate([zpad, w_a2[l]], axis=0), 'wg2': w_g2[l].astype(BF16),
            'k_k': row(k_k[l]), 'k_a': row(k_a[l]), 'r_k': row(r_k[l]),
            'lnx_w': row(lnx_w[l]), 'lnx_b': row(lnx_b[l]), 'seg': seg, 'segt': seg.T,
            'conv_w': conv_w[l], 'conv_b': row(conv_b[l]),
            'wga': _pair_blockdiag(w_gate_a[l]).astype(BF16), 'bga': row(b_gate_a[l]),
            'wgi': _pair_blockdiag(w_gate_i[l]).astype(BF16), 'bgi': row(b_gate_i[l]),
            'lam': row(lru_lambda[l]),
            'ln1_w': ln1_w[l], 'ln1_b': ln1_b[l], 'ln2_w': ln2_w[l], 'ln2_b': ln2_b[l],
        }
        c_all = jnp.concatenate([c_prompt, c_sample], axis=0)
        mod = _ada(c_all, w_ada[l], b_ada[l])
        mod_p = mod[:bp, None, :]
        mod_s = mod[bp:, None, :]
        x1p, h2p, st_p = _mix_group(
            yp, mod_p, jnp.zeros((bp, C_A), F32), jnp.zeros((bp, N_HEADS, HEAD, HEAD), F32),
            jnp.zeros((bp, D_B), F32), jnp.zeros((bp, CONV_W - 1, D_B), F32), prm,
            ns_mix=1, tb_mix=ROWS, chunk=16, t_valid=ROWS, ns_big=1, tb_big=min(512, tp), alpha=alpha,
            streams=2 if bp % 2 == 0 else 1)
        x1s, h2s, st_s = _mix_group(
            ys, mod_s, None, state_wkv[l], state_lru[l], state_conv[l], prm,
            ns_mix=ROWS // SAMPLE_TPAD, tb_mix=SAMPLE_TPAD, chunk=SAMPLE_TPAD, t_valid=ts,
            ns_big=min(64, bs), tb_big=SAMPLE_TPAD, alpha=alpha, shift_rows=state_shift[l])
        h2s = h2s.reshape(bs, SAMPLE_TPAD, D_MODEL)[:, :ts].reshape(bs * ts, D_MODEL)
        h2 = jnp.concatenate([h2p, h2s], axis=0)
        gmat = _route(h2, prm['w_q'], prm['sub_keys'])
        ff = _experts(h2, gmat, peer_u[l], peer_v[l])
        yp = _ln2(x1p, ff, mod_p, ln2_w[l], ln2_b[l], 1, min(512, tp), alpha)
        ys_real = _ln2(x1s[:, :ts], ff[bp * tp:].reshape(bs, ts, D_MODEL), mod_s, ln2_w[l], ln2_b[l],
                       min(64, bs), ts, alpha)
        ys = jnp.pad(ys_real, ((0, 0), (0, SAMPLE_TPAD - ts), (0, 0)))
        for i, a in enumerate(st_p + st_s):
            outs[i].append(a)
    return (yp, ys[:, :ts]) + tuple(jnp.stack(o) for o in outs)
```

```python
import functools

import jax
import jax.numpy as jnp
from jax import lax
from jax.experimental import pallas as pl
from jax.experimental.pallas import tpu as pltpu

F32 = jnp.float32
BF16 = jnp.bfloat16

D_MODEL = 2048
D_A = 1024
D_B = 1024
HEAD = 64
N_HEADS = D_A // HEAD
N_PAIRS = N_HEADS // 2
R_W = 64
R_A = 64
R_G = 128
C_A = 3 * D_A + R_W + R_A + R_G
C_TOT = C_A + 2 * D_B
CONV_W = 4
C_LRU = 8.0
P_HEADS = 8
N_KEYS = 128
N_EXPERTS = N_KEYS * N_KEYS
TOPK = 16
D_QH = 128
LN_EPS = 1e-5
GN_EPS = 64e-5

LANES = 128
SUBLANES = 8
ROWS = 128
SAMPLE_TPAD = 8
VMEM_LIMIT = 56 * 1024 * 1024


def _cparams(sem):
    return pltpu.CompilerParams(dimension_semantics=sem, vmem_limit_bytes=VMEM_LIMIT)


def _split2(x):
    hi = x.astype(BF16)
    lo = (x - hi.astype(F32)).astype(BF16)
    return hi, lo


_NN = (((1,), (0,)), ((), ()))
_NT = (((1,), (1,)), ((), ()))
_TN = (((0,), (0,)), ((), ()))


def _dg(a, b, dims):
    return lax.dot_general(a, b, dims, preferred_element_type=F32)


def _dot1(a, b, dims=_NN):
    return _dg(a.astype(BF16), b.astype(BF16), dims)


def _dot3(a, b, dims=_NN):
    ah, al = _split2(a)
    bh, bl = _split2(b)
    return _dg(ah, bh, dims) + (_dg(ah, bl, dims) + _dg(al, bh, dims))


_state_dot = _dot1


def _dot_exact_rhs(a, b_exact, dims=_NN):
    a1, a2 = _split2(a)
    return _dg(a1, b_exact, dims) + _dg(a2, b_exact, dims)


def _dot_exact_lhs(a_exact, b, dims=_NN):
    b1, b2 = _split2(b)
    return _dg(a_exact, b1, dims) + _dg(a_exact, b2, dims)


def _sigmoid(x):
    return 1.0 / (1.0 + jnp.exp(-x))


def _softplus(x):
    return jnp.maximum(x, 0.0) + jnp.log(1.0 + jnp.exp(-jnp.abs(x)))


def _gelu(x):
    return 0.5 * x * (1.0 + jnp.tanh(0.7978845608028654 * (x + 0.044715 * (x * x * x))))


def _iota(shape, dim):
    return lax.broadcasted_iota(jnp.int32, shape, dim)


def _ada_kernel(c_ref, w_ref, b_ref, o_ref):
    c = c_ref[...]
    s = c * _sigmoid(c)
    o_ref[...] = _dot3(s, w_ref[...]) + b_ref[...]


def _ada(c, w_ada, b_ada):
    n = c.shape[0]
    tn = 1024
    return pl.pallas_call(
        _ada_kernel,
        out_shape=jax.ShapeDtypeStruct((n, 6 * D_MODEL), F32),
        grid=(6 * D_MODEL // tn,),
        in_specs=[pl.BlockSpec((n, D_MODEL), lambda j: (0, 0)),
                  pl.BlockSpec((D_MODEL, tn), lambda j: (0, j)),
                  pl.BlockSpec((1, tn), lambda j: (0, j))],
        out_specs=pl.BlockSpec((n, tn), lambda j: (0, j)),
        compiler_params=_cparams(("arbitrary",)),
        name="ada",
    )(c, w_ada, b_ada.reshape(1, -1))


def _inproj_kernel(x_ref, sc_ref, sh_ref, w_ref, o_ref, h_ref):
    h = x_ref[...] * (1.0 + sc_ref[...]) + sh_ref[...]
    ns, tb, _ = h.shape
    h_ref[...] = h[:, tb - SUBLANES:, :]
    p = _dot1(h.reshape(ns * tb, D_MODEL), w_ref[...])
    o_ref[...] = p.reshape(ns, tb, p.shape[-1])


def _inproj(x, mod, w_bf16, ns, tb):
    S, T, _ = x.shape
    ncol = w_bf16.shape[1]
    tn = ncol // 3
    proj, h_last = pl.pallas_call(
        _inproj_kernel,
        out_shape=(jax.ShapeDtypeStruct((S, T, ncol), F32),
                   jax.ShapeDtypeStruct((3, S, SUBLANES, D_MODEL), F32)),
        grid=(3, S // ns, T // tb),
        in_specs=[pl.BlockSpec((ns, tb, D_MODEL), lambda j, s, t: (s, t, 0)),
                  pl.BlockSpec((ns, 1, D_MODEL), lambda j, s, t: (s, 0, 1)),
                  pl.BlockSpec((ns, 1, D_MODEL), lambda j, s, t: (s, 0, 0)),
                  pl.BlockSpec((D_MODEL, tn), lambda j, s, t: (0, j))],
        out_specs=(pl.BlockSpec((ns, tb, tn), lambda j, s, t: (s, t, j)),
                   pl.BlockSpec((None, ns, SUBLANES, D_MODEL), lambda j, s, t: (j, s, 0, 0))),
        compiler_params=_cparams(("arbitrary", "arbitrary", "arbitrary")),
        name="inproj",
    )(x, mod, mod, w_bf16)
    return proj, h_last[0]


def _pair_tile(ref, s, pr):
    zero_blk = jnp.zeros((HEAD, HEAD), F32)
    top = jnp.concatenate([ref[s, 2 * pr], zero_blk], axis=1)
    bot = jnp.concatenate([zero_blk, ref[s, 2 * pr + 1]], axis=1)
    return jnp.concatenate([top, bot], axis=0)


def _mixer_stream(proj_ref, s0_ref,
                  mu_ref, w0_ref, ww2_ref, a0_ref, wa2_ref, wg2_ref, kk_ref, ka_ref, rk_ref,
                  lnw_ref, lnb_ref, seg_ref, segt_ref,
                  cw_ref, cb_ref, wga_ref, bga_ref, wgi_ref, bgi_ref, lam_ref,
                  y_ref, sout_ref,
                  prev_ref, hst_ref, st_ref,
                  bp_ref, rp_ref, x2_ref, y2_ref, v_ref, kb_ref, ab_ref, wt_ref, yacc_ref,
                  *, ns, tb, chunk, t_valid, carry_state):
    R = ns * tb
    pair_tile = _pair_tile

    cur = proj_ref[...]
    ext = jnp.concatenate([prev_ref[...], cur[:, :, :C_A + D_B]], axis=1)
    prev_ref[...] = cur[:, tb - SUBLANES:, :C_A + D_B]

    row = _iota((R, 1), 0)
    rowin = row % tb
    valid = rowin < t_valid

    pa = cur[:, :, :C_A].reshape(R, C_A)
    shifted = ext[:, SUBLANES - 1:SUBLANES - 1 + tb, :C_A].reshape(R, C_A)
    p = pa + (shifted - pa) * mu_ref[...]
    r = p[:, :D_A]
    k = p[:, D_A:2 * D_A]
    v = p[:, 2 * D_A:3 * D_A]
    xwa = p[:, 3 * D_A:3 * D_A + LANES]
    xg = p[:, 3 * D_A + LANES:]
    w_log = -_softplus(-(w0_ref[...] + _dot1(jnp.tanh(xwa), ww2_ref[...]))) - 0.5
    logw = -jnp.exp(w_log)
    a = _sigmoid(a0_ref[...] + _dot1(xwa, wa2_ref[...]))
    g = _dot1(_sigmoid(xg), wg2_ref[...])
    yield 900
    seg = seg_ref[...]
    segt = segt_ref[...]

    def seg_sum(x, split=True):
        part = _dot_exact_rhs(x, seg) if split else _dot1(x, seg)
        return _dot_exact_rhs(part, segt)

    kk = k * kk_ref[...]
    kk = kk * lax.rsqrt(jnp.maximum(seg_sum(kk * kk, split=False), 1e-24))
    yield 600
    k = k * (1.0 + (a - 1.0) * ka_ref[...])
    kka = kk * a
    bonus = seg_sum(r * k * rk_ref[...], split=False) * v
    yield 600
    if t_valid < tb:
        logw = jnp.where(valid, logw, 0.0)
        r = jnp.where(valid, r, 0.0)
        k = jnp.where(valid, k, 0.0)
        v = jnp.where(valid, v, 0.0)
        kk = jnp.where(valid, kk, 0.0)
        kka = jnp.where(valid, kka, 0.0)

    ri = _iota((R, R), 0)
    ci = _iota((R, R), 1)
    same = (ri // chunk) == (ci // chunk)
    incl = same & (ci <= ri)
    strict = same & (ci < ri)
    lmat = jnp.concatenate([jnp.where(incl, 1.0, 0.0), jnp.where(same, 1.0, 0.0)], axis=0).astype(BF16)
    cums = _dot_exact_lhs(lmat, logw)
    cum = cums[:R]
    tot = cums[R:]
    yield 500
    e_prev = jnp.exp(cum - logw)
    e_neg = jnp.exp(-cum)
    e_pos = jnp.exp(cum)
    e_rem = jnp.exp(tot - cum)
    beta = kk * e_prev
    alpha = kka * e_neg
    kappa = k * e_neg
    rho = r * e_pos
    yield 600
    v_ref[...] = v
    kb_ref[...] = k * e_rem
    ab_ref[...] = -(kka * e_rem)
    wt_ref[...] = jnp.exp(tot)
    yield 400

    lane = _iota((1, LANES), 1)
    masks = (lane < HEAD, lane >= HEAD)
    eye = jnp.where(ri == ci, 1.0, 0.0)
    nsq = {16: 3, 8: 2}[chunk]
    heads = [(pr, m) for pr in range(N_PAIRS) for m in masks]
    psl = lambda pr: slice(pr * LANES, (pr + 1) * LANES)
    bms = [jnp.where(m, beta[:, psl(pr)], 0.0) for pr, m in heads]
    rms = [jnp.where(m, rho[:, psl(pr)], 0.0) for pr, m in heads]
    yield 300
    gms = []
    for (pr, m), bm, rm in zip(heads, bms, rms):
        gms.append(_dot1(jnp.concatenate([bm, rm], axis=0),
                         jnp.concatenate([alpha[:, psl(pr)], kappa[:, psl(pr)]], axis=0), _NT))
        yield 70
    l_bas = [jnp.where(strict, gm[:R, :R], 0.0) for gm in gms]
    m_ras = [jnp.where(incl, gm[R:, :R], 0.0) for gm in gms]
    xy1s = []
    for (pr, m), gm in zip(heads, gms):
        xy1s.append(_dot1(jnp.concatenate([jnp.where(strict, gm[:R, R:], 0.0),
                                           jnp.where(incl, gm[R:, R:], 0.0)], axis=0), v[:, psl(pr)]))
        yield 70
    pws = []
    for l in l_bas:
        pws.append(_dot1(l, l))
        yield 40

    def lru_steps():
        for pr in range(N_PAIRS):
            sl = psl(pr)
            bsl = slice(C_A + pr * LANES, C_A + (pr + 1) * LANES)
            gsl = slice(C_A + D_B + pr * LANES, C_A + D_B + (pr + 1) * LANES)
            xc = cb_ref[:, sl]
            for j in range(CONV_W):
                off = SUBLANES - (CONV_W - 1) + j
                xc = xc + ext[:, off:off + tb, bsl].reshape(R, LANES) * cw_ref[j:j + 1, sl]
            yield
            rg = _sigmoid(_dot1(xc, wga_ref[pr]) + bga_ref[:, sl])
            ig = _sigmoid(_dot1(xc, wgi_ref[pr]) + bgi_ref[:, sl])
            yield
            log_a = -C_LRU * rg * _softplus(-lam_ref[:, sl])
            av = jnp.exp(log_a)
            bv = jnp.sqrt(1.0 - jnp.exp(2.0 * log_a)) * (ig * xc)
            if t_valid < tb:
                av = jnp.where(valid, av, 1.0)
                bv = jnp.where(valid, bv, 0.0)
            yield
            d = 1
            while d < tb:
                take = rowin >= d
                a_sh = jnp.where(take, pltpu.roll(av, d, axis=0), 1.0)
                b_sh = jnp.where(take, pltpu.roll(bv, d, axis=0), 0.0)
                bv = av * b_sh + bv
                av = av * a_sh
                d *= 2
                yield
            h0 = jnp.broadcast_to(hst_ref[:, SUBLANES - 1:SUBLANES, sl], (ns, tb, LANES)).reshape(R, LANES)
            h3 = (av * h0 + bv).reshape(ns, tb, LANES)
            hst_ref[:, :, sl] = h3[:, tb - SUBLANES:, :]
            gb = cur[:, :, gsl].reshape(R, LANES)
            y_ref[:, :, D_A + pr * LANES:D_A + (pr + 1) * LANES] = h3 * _gelu(gb).reshape(ns, tb, LANES)
            yield

    lru = lru_steps()
    minvs = [eye - l for l in l_bas]
    for i in range(nsq):
        nxt = []
        for mi, pw in zip(minvs, pws):
            nxt.append(mi + _dot1(mi, pw))
            next(lru, None)
            yield 60
        minvs = nxt
        if i + 1 < nsq:
            nxt = []
            for pw in pws:
                nxt.append(_dot1(pw, pw))
                next(lru, None)
                yield 60
            pws = nxt
    for _ in lru:
        yield 40
    t1s = []
    for mi, xy1, bm in zip(minvs, xy1s, bms):
        t1s.append(_dot1(mi, jnp.concatenate([xy1[:R], bm], axis=1)))
        yield 50
    t2s = []
    for m_ra, t1 in zip(m_ras, t1s):
        t2s.append(_dot1(m_ra, t1))
        yield 50
    for pr in range(N_PAIRS):
        h0, h1 = 2 * pr, 2 * pr + 1
        sl = psl(pr)
        m1 = masks[1]
        bp_ref[:, sl] = t1s[h0][:, LANES:] + t1s[h1][:, LANES:]
        rp_ref[:, sl] = (rms[h0] - t2s[h0][:, LANES:]) + (rms[h1] - t2s[h1][:, LANES:])
        x2_ref[:, sl] = jnp.where(m1, t1s[h1][:, :LANES], t1s[h0][:, :LANES])
        y2_ref[:, sl] = jnp.where(m1, xy1s[h1][R:] - t2s[h1][:, :LANES], xy1s[h0][R:] - t2s[h0][:, :LANES])

    bi = _iota((LANES, LANES), 0)
    bj = _iota((LANES, LANES), 1)
    bd = (bi < HEAD) == (bj < HEAD)
    nck = tb // chunk
    sites = [(s, c, pr) for s in range(ns) for c in range(nck) for pr in range(N_PAIRS)]
    rows_of = lambda s, c: slice(s * tb + c * chunk, s * tb + (c + 1) * chunk)
    thetas = {}
    psis = {}
    for s, c, pr in sites:
        rs, sl = rows_of(s, c), psl(pr)
        thetas[s, c, pr] = jnp.where(bd, _state_dot(bp_ref[rs, sl], ab_ref[rs, sl], _TN), 0.0)
        vx = jnp.concatenate([v_ref[rs, sl], x2_ref[rs, sl]], axis=0)
        ka = jnp.concatenate([kb_ref[rs, sl], ab_ref[rs, sl]], axis=0)
        psis[s, c, pr] = jnp.where(bd, _state_dot(vx, ka, _TN), 0.0)
        yield 60
    for s in range(ns):
        sps = [st_ref[s, pr] if carry_state else pair_tile(s0_ref, s, pr) for pr in range(N_PAIRS)]
        for c in range(nck):
            rs = rows_of(s, c)
            for pr in range(N_PAIRS):
                yacc_ref[rs, psl(pr)] = _state_dot(rp_ref[rs, psl(pr)], sps[pr], _NT) + y2_ref[rs, psl(pr)]
            yield 100
            nxt = []
            for pr in range(N_PAIRS):
                nxt.append(sps[pr] * wt_ref[rs.start:rs.start + 1, psl(pr)]
                           + (_state_dot(sps[pr], thetas[s, c, pr]) + psis[s, c, pr]))
                yield 50
            sps = nxt
        for pr in range(N_PAIRS):
            if carry_state:
                st_ref[s, pr] = sps[pr]
            else:
                sout_ref[s, 2 * pr] = sps[pr][:HEAD, :HEAD]
                sout_ref[s, 2 * pr + 1] = sps[pr][HEAD:, HEAD:]
        yield 50

    y = yacc_ref[...]
    mean = seg_sum(y) * (1.0 / HEAD)
    yc = y - mean
    yield 400
    var = seg_sum(yc * yc) * (1.0 / HEAD)
    yn = yc * lax.rsqrt(var + GN_EPS) * lnw_ref[...] + lnb_ref[...]
    ya = (yn + bonus) * g
    y_ref[:, :, :D_A] = ya.reshape(ns, tb, D_A)
    yield 600


def _mixer_kernel(proj_ref, pprev_ref, conv0_ref, s0_ref, lru0_ref, *refs, ns, tb, chunk, t_valid, carry_state, streams):
    n_par = 20
    params = refs[:n_par]
    y_ref, sout_ref, lruo_ref, convo_ref, prev_ref, hst_ref, st_ref = refs[n_par:n_par + 7]
    slabs = refs[n_par + 7:]
    t = pl.program_id(1)

    @pl.when(t == 0)
    def _():
        prev_ref[...] = jnp.zeros_like(prev_ref)
        prev_ref[:, SUBLANES - 1:, :C_A] = pprev_ref[...]
        prev_ref[:, SUBLANES - (CONV_W - 1):, C_A:] = conv0_ref[...]
        hst_ref[...] = jnp.broadcast_to(lru0_ref[...], hst_ref.shape)
        if carry_state:
            for q in range(streams):
                for pr in range(N_PAIRS):
                    st_ref[q, pr] = _pair_tile(s0_ref, q, pr)

    def make(q):
        sq = pl.ds(q * ns, ns)
        return _mixer_stream(
            proj_ref.at[sq], s0_ref.at[sq], *params, y_ref.at[sq], sout_ref.at[sq],
            prev_ref.at[sq], hst_ref.at[sq], st_ref.at[pl.ds(q, 1)], *[r.at[q] for r in slabs],
            ns=ns, tb=tb, chunk=chunk, t_valid=t_valid, carry_state=carry_state)

    gens = [make(q) for q in range(streams)]
    clock = [0.0] * streams
    live = list(range(streams))
    while live:
        q = min(live, key=lambda i: clock[i])
        cost = next(gens[q], None)
        if cost is None:
            live.remove(q)
        else:
            clock[q] += cost

    @pl.when(t == pl.num_programs(1) - 1)
    def _():
        lruo_ref[...] = hst_ref[...]
        convo_ref[...] = proj_ref[:, tb - SUBLANES:, C_A:C_A + D_B]
        if carry_state:
            for q in range(streams):
                for pr in range(N_PAIRS):
                    tile = st_ref[q, pr]
                    sout_ref[q, 2 * pr] = tile[:HEAD, :HEAD]
                    sout_ref[q, 2 * pr + 1] = tile[HEAD:, HEAD:]


def _mixer(proj, p_prev, conv0, s0, lru0, prm, ns, tb, chunk, t_valid, streams=1):
    S, T = s0.shape[0], proj.shape[1]
    R = ns * tb
    assert R == ROWS
    carry_state = T // tb > 1
    assert ns == 1 or not carry_state
    kern = functools.partial(_mixer_kernel, ns=ns, tb=tb, chunk=chunk, t_valid=t_valid, carry_state=carry_state,
                             streams=streams)
    nb = ns * streams

    def full(a):
        nd = a.ndim
        return pl.BlockSpec(a.shape, lambda s, t, nd=nd: (0,) * nd)

    params = [prm[n] for n in ('mu', 'w0', 'ww2', 'a0', 'wa2', 'wg2', 'k_k', 'k_a', 'r_k', 'lnx_w', 'lnx_b',
                               'seg', 'segt', 'conv_w', 'conv_b', 'wga', 'bga', 'wgi', 'bgi', 'lam')]
    slab = lambda w: pltpu.VMEM((streams, R, w), F32)
    return pl.pallas_call(
        kern,
        out_shape=(jax.ShapeDtypeStruct((S, T, D_A + D_B), F32),
                   jax.ShapeDtypeStruct((S, N_HEADS, HEAD, HEAD), F32),
                   jax.ShapeDtypeStruct((S, SUBLANES, D_B), F32),
                   jax.ShapeDtypeStruct((S, SUBLANES, D_B), F32)),
        grid=(S // nb, T // tb),
        in_specs=[pl.BlockSpec((nb, tb, C_TOT), lambda s, t: (s, t, 0)),
                  pl.BlockSpec((nb, 1, C_A), lambda s, t: (s, 0, 0)),
                  pl.BlockSpec((nb, CONV_W - 1, D_B), lambda s, t: (s, 0, 0)),
                  pl.BlockSpec((nb, N_HEADS, HEAD, HEAD), lambda s, t: (s, 0, 0, 0),
                               pipeline_mode=pl.Buffered(1)),
                  pl.BlockSpec((nb, 1, D_B), lambda s, t: (s, 0, 0))]
                 + [full(a) for a in params],
        out_specs=(pl.BlockSpec((nb, tb, D_A + D_B), lambda s, t: (s, t, 0)),
                   pl.BlockSpec((nb, N_HEADS, HEAD, HEAD), lambda s, t: (s, 0, 0, 0),
                                pipeline_mode=pl.Buffered(1)),
                   pl.BlockSpec((nb, SUBLANES, D_B), lambda s, t: (s, 0, 0)),
                   pl.BlockSpec((nb, SUBLANES, D_B), lambda s, t: (s, 0, 0))),
        scratch_shapes=[pltpu.VMEM((nb, SUBLANES, C_A + D_B), F32),
                        pltpu.VMEM((nb, SUBLANES, D_B), F32),
                        pltpu.VMEM((streams, N_PAIRS, LANES, LANES), F32)]
                       + [slab(D_A) for _ in range(9)],
        compiler_params=_cparams(("arbitrary", "arbitrary")),
        name="mixer",
    )(proj, p_prev, conv0, s0, lru0, *params)


def _layernorm(x, w, b):
    mean = jnp.mean(x, axis=-1, keepdims=True)
    xc = x - mean
    var = jnp.mean(xc * xc, axis=-1, keepdims=True)
    return xc * lax.rsqrt(var + LN_EPS) * w + b


def _outproj_kernel(y_ref, x_ref, g1_ref, sc2_ref, sh2_ref, w_ref, lnw_ref, lnb_ref, x1_ref, h2_ref, *, alpha):
    ns, tb, _ = y_ref.shape
    mix = _dot1(y_ref[...].reshape(ns * tb, D_MODEL), w_ref[...]).reshape(ns, tb, D_MODEL)
    x1 = _layernorm(alpha * x_ref[...] + (1.0 + g1_ref[...]) * mix, lnw_ref[...], lnb_ref[...])
    x1_ref[...] = x1
    h2 = x1 * (1.0 + sc2_ref[...]) + sh2_ref[...]
    h2_ref[...] = h2.reshape(ns * tb, D_MODEL).astype(BF16)


def _outproj(y, x, mod, w_bf16, ln_w, ln_b, ns, tb, alpha):
    S, T, _ = x.shape
    nt = T // tb
    blk = pl.BlockSpec((ns, tb, D_MODEL), lambda s, t: (s, t, 0))
    modspec = lambda i: pl.BlockSpec((ns, 1, D_MODEL), lambda s, t, i=i: (s, 0, i))
    vec = pl.BlockSpec((1, D_MODEL), lambda s, t: (0, 0))
    return pl.pallas_call(
        functools.partial(_outproj_kernel, alpha=alpha),
        out_shape=(jax.ShapeDtypeStruct((S, T, D_MODEL), F32), jax.ShapeDtypeStruct((S * T, D_MODEL), BF16)),
        grid=(S // ns, nt),
        in_specs=[blk, blk, modspec(2), modspec(4), modspec(3),
                  pl.BlockSpec((D_MODEL, D_MODEL), lambda s, t: (0, 0)), vec, vec],
        out_specs=(blk, pl.BlockSpec((ns * tb, D_MODEL), lambda s, t: (s * nt + t, 0))),
        compiler_params=_cparams(("arbitrary", "arbitrary")),
        name="outproj",
    )(y, x, mod, mod, mod, w_bf16, ln_w.reshape(1, -1), ln_b.reshape(1, -1))


def _topk_rows(s, ridx, n_rows):
    out_i = _iota((TOPK, s.shape[1]), 0)
    vals = jnp.zeros((TOPK, s.shape[1]), F32)
    idxs = jnp.zeros((TOPK, s.shape[1]), F32)
    for it in range(TOPK):
        m = jnp.max(s, axis=0, keepdims=True)
        idx = jnp.min(jnp.where(s == m, ridx, float(n_rows)), axis=0, keepdims=True)
        vals = jnp.where(out_i == it, m, vals)
        idxs = jnp.where(out_i == it, idx, idxs)
        s = jnp.where(ridx == idx, -jnp.inf, s)
    return vals, idxs


def _route_kernel(h_ref, wq_ref, sk_ref, g_ref, gate_ref, e1_ref, e2_ref):
    R = ROWS
    q = _dot1(h_ref[...], wq_ref[...])
    sk0 = sk_ref[0]
    sk1 = sk_ref[1]
    gates, e1s, e2s = [], [], []
    key_rows = _iota((N_KEYS, R), 0).astype(F32)
    r16 = _iota((TOPK, R), 0).astype(F32)
    r8 = _iota((SUBLANES, R), 0).astype(F32)
    ea_rank = jnp.where(r8 < 3, 2.0, jnp.where(r8 < 5, 3.0, 4.0))
    eb_rank = jnp.where((r8 == 0) | (r8 == 3) | (r8 == 5), 2.0, jnp.where((r8 == 1) | (r8 == 4), 3.0, 4.0))
    flat = jnp.concatenate([r16, TOPK + r8, TOPK * r16, TOPK * r8 + 1.0, TOPK * ea_rank + eb_rank], axis=0)
    neg = -jnp.inf
    for hd in range(P_HEADS):
        base = hd * 2 * D_QH
        s1 = _dot1(sk0, q[:, base:base + D_QH], _NT)
        s2 = _dot1(sk1, q[:, base + D_QH:base + 2 * D_QH], _NT)
        v1, i1 = _topk_rows(s1, key_rows, N_KEYS)
        v2, i2 = _topk_rows(s2, key_rows, N_KEYS)
        ea = jnp.where(r8 < 3, v1[2:3], jnp.where(r8 < 5, v1[3:4], v1[4:5]))
        eb = jnp.where(eb_rank == 2.0, v2[2:3], jnp.where(eb_rank == 3.0, v2[3:4], v2[4:5]))
        cand = jnp.concatenate([
            v1[0:1] + v2,
            v1[1:2] + v2[:SUBLANES],
            jnp.where(r16 >= 2, v1 + v2[0:1], neg),
            jnp.where(r8 >= 2, v1[:SUBLANES] + v2[1:2], neg),
            jnp.where(r8 < 6, ea + eb, neg),
        ], axis=0)
        sv, ci = _topk_rows(cand, flat, TOPK * TOPK)
        ca = jnp.floor(ci * (1.0 / TOPK))
        cb = ci - TOPK * ca
        e1 = jnp.zeros((TOPK, R), F32)
        e2 = jnp.zeros((TOPK, R), F32)
        for a in range(TOPK):
            e1 = jnp.where(ca == float(a), i1[a:a + 1, :], e1)
            e2 = jnp.where(cb == float(a), i2[a:a + 1, :], e2)
        ex = jnp.exp(sv - jnp.max(sv, axis=0, keepdims=True))
        gates.append(ex / jnp.sum(ex, axis=0, keepdims=True))
        e1s.append(e1)
        e2s.append(e2)
    gate_ref[...] = jnp.concatenate(gates, axis=0).T
    e1_ref[...] = jnp.concatenate(e1s, axis=0).T
    e2_ref[...] = jnp.concatenate(e2s, axis=0).T
    key_i = _iota((N_KEYS, P_HEADS * TOPK), 0).astype(F32)

    def per_octet(o, carry):
        base = pl.multiple_of(o * SUBLANES, SUBLANES)
        tiles = []
        for j in range(SUBLANES):
            sel1 = key_i == e1_ref[pl.ds(base + j, 1), :]
            sel2 = key_i == e2_ref[pl.ds(base + j, 1), :]
            m1 = jnp.where(sel1, gate_ref[pl.ds(base + j, 1), :], 0.0)
            m2 = jnp.where(sel2, 1.0, 0.0)
            tiles.append(_dot1(m1, m2, _NT))
        by_key = jnp.swapaxes(jnp.stack(tiles, axis=0), 0, 1)
        for i1v in range(N_KEYS):
            g_ref[pl.ds(base, SUBLANES), i1v * N_KEYS:(i1v + 1) * N_KEYS] = by_key[i1v]
        return carry

    lax.fori_loop(0, R // SUBLANES, per_octet, 0, unroll=8)


def _route(h2, wq_bf16, sub_keys):
    n = h2.shape[0]
    assert n % ROWS == 0
    return pl.pallas_call(
        _route_kernel,
        out_shape=jax.ShapeDtypeStruct((n, N_EXPERTS), F32),
        grid=(n // ROWS,),
        in_specs=[pl.BlockSpec((ROWS, D_MODEL), lambda i: (i, 0)),
                  pl.BlockSpec((D_MODEL, D_MODEL), lambda i: (0, 0)),
                  pl.BlockSpec((2, N_KEYS, D_QH), lambda i: (0, 0, 0))],
        out_specs=pl.BlockSpec((ROWS, N_EXPERTS), lambda i: (i, 0)),
        scratch_shapes=[pltpu.VMEM((ROWS, P_HEADS * TOPK), F32) for _ in range(3)],
        compiler_params=_cparams(("arbitrary",)),
        name="route",
    )(h2, wq_bf16, sub_keys)


EXPERT_CHUNK = 512
EXPERT_ROWS_MAX = 1152


EXPERT_SPLIT = 4


def _experts_kernel(h_ref, g_ref, u_ref, v_ref, o_ref):
    @pl.when(pl.program_id(1) == 0)
    def _():
        o_ref[...] = jnp.zeros_like(o_ref)

    u = u_ref[...].astype(BF16)
    v = v_ref[...].astype(BF16)
    rb = h_ref.shape[0] // EXPERT_SPLIT
    rows = [pl.ds(b * rb, rb) for b in range(EXPERT_SPLIT)]
    act = _dg(h_ref[rows[0], :], u, _NT)
    for b in range(EXPERT_SPLIT):
        nxt = _dg(h_ref[rows[b + 1], :], u, _NT) if b + 1 < EXPERT_SPLIT else None
        w = (g_ref[rows[b], :] * _gelu(act)).astype(BF16)
        o_ref[rows[b], :] += _dg(w, v, _NN)
        act = nxt


def _token_block(n, cap):
    for tb in range(min(n, cap), 0, -1):
        if n % tb == 0 and tb % (2 * SUBLANES * EXPERT_SPLIT) == 0:
            return tb
    raise ValueError(n)


def _experts(h2, gmat, peer_u, peer_v):
    n = h2.shape[0]
    tb = _token_block(n, EXPERT_ROWS_MAX)
    ec = EXPERT_CHUNK
    return pl.pallas_call(
        _experts_kernel,
        out_shape=jax.ShapeDtypeStruct((n, D_MODEL), F32),
        grid=(n // tb, N_EXPERTS // ec),
        in_specs=[pl.BlockSpec((tb, D_MODEL), lambda i, e: (i, 0)),
                  pl.BlockSpec((tb, ec), lambda i, e: (i, e)),
                  pl.BlockSpec((ec, D_MODEL), lambda i, e: (e, 0)),
                  pl.BlockSpec((ec, D_MODEL), lambda i, e: (e, 0))],
        out_specs=pl.BlockSpec((tb, D_MODEL), lambda i, e: (i, 0), pipeline_mode=pl.Buffered(1)),
        compiler_params=_cparams(("arbitrary", "arbitrary")),
        name="experts",
    )(h2, gmat, peer_u, peer_v)


def _ln2_kernel(x1_ref, ff_ref, g2_ref, lnw_ref, lnb_ref, o_ref, *, alpha):
    ff = ff_ref[...].reshape(x1_ref.shape)
    o_ref[...] = _layernorm(alpha * x1_ref[...] + (1.0 + g2_ref[...]) * ff, lnw_ref[...], lnb_ref[...])


def _ln2(x1, ff, mod, ln_w, ln_b, ns, tb, alpha):
    S, T, _ = x1.shape
    nt = T // tb
    blk = pl.BlockSpec((ns, tb, D_MODEL), lambda s, t: (s, t, 0))
    vec = pl.BlockSpec((1, D_MODEL), lambda s, t: (0, 0))
    if ff.ndim == 2:
        assert ns == 1
        ffspec = pl.BlockSpec((tb, D_MODEL), lambda s, t: (s * nt + t, 0))
    else:
        ffspec = blk
    return pl.pallas_call(
        functools.partial(_ln2_kernel, alpha=alpha),
        out_shape=jax.ShapeDtypeStruct((S, T, D_MODEL), F32),
        grid=(S // ns, nt),
        in_specs=[blk, ffspec, pl.BlockSpec((ns, 1, D_MODEL), lambda s, t: (s, 0, 5)), vec, vec],
        out_specs=blk,
        compiler_params=_cparams(("arbitrary", "arbitrary")),
        name="ln2",
    )(x1, ff, mod, ln_w.reshape(1, -1), ln_b.reshape(1, -1))


def _pair_blockdiag(w):
    n = w.shape[0] // 2
    w = w.reshape(n, 2, HEAD, HEAD)
    z = jnp.zeros((n, HEAD, HEAD), w.dtype)
    top = jnp.concatenate([w[:, 0], z], axis=2)
    bot = jnp.concatenate([z, w[:, 1]], axis=2)
    return jnp.concatenate([top, bot], axis=1)


def _mix_group(x, mod, p_prev, wkv0, lru0, conv0, prm, *, ns_mix, tb_mix, chunk, t_valid, ns_big, tb_big, alpha,
               streams=1, shift_rows=None):
    S, T, _ = x.shape
    if shift_rows is None:
        proj, h_last = _inproj(x, mod, prm['w_in'], ns_big, tb_big)
    else:
        extra = S // T
        x_ext = jnp.concatenate([x, shift_rows.reshape(extra, T, D_MODEL)], axis=0)
        mod_ext = jnp.concatenate([mod, jnp.zeros((extra, 1, 6 * D_MODEL), F32)], axis=0)
        ns_ext = max(d for d in range(1, ns_big + extra // 2 + 1) if (S + extra) % d == 0)
        proj, h_last = _inproj(x_ext, mod_ext, prm['w_in'], ns_ext, tb_big)
        p_prev = proj[S:].reshape(S, C_TOT)[:, :C_A]
        h_last = h_last[:S]
    y, wkv, lru_o, conv_o = _mixer(proj, p_prev[:, None, :], conv0, wkv0, lru0[:, None, :], prm,
                                   ns_mix, tb_mix, chunk, t_valid, streams)
    x1, h2 = _outproj(y, x, mod, prm['w_out'], prm['ln1_w'], prm['ln1_b'], ns_big, tb_big, alpha)
    tv = (t_valid - 1) % SUBLANES
    shift = h_last[:, tv]
    lru = lru_o[:, SUBLANES - 1]
    conv = conv_o[:, tv - (CONV_W - 2):tv + 1]
    return x1, h2, (shift, wkv, lru, conv)


def kernel(x_prompt, x_sample, c_prompt, c_sample, state_shift, state_wkv, state_lru, state_conv, w_ada, b_ada, w_in, mu_shift, w0, w_w2, a0, w_a2, w_g2, k_k, k_a, r_k, lnx_w, lnx_b, conv_w, conv_b, w_gate_a, b_gate_a, w_gate_i, b_gate_i, lru_lambda, w_out, ln1_w, ln1_b, w_q, sub_keys, peer_u, peer_v, ln2_w, ln2_b):
    depth = w_ada.shape[0]
    alpha = (2 * depth) ** 0.25
    bp, tp, _ = x_prompt.shape
    bs, ts, _ = x_sample.shape
    yp = x_prompt
    ys = jnp.pad(x_sample, ((0, 0), (0, SAMPLE_TPAD - ts), (0, 0)))
    head_of = jnp.arange(D_A) // HEAD
    seg = (head_of[:, None] == jnp.arange(LANES)[None, :]).astype(BF16)
    row = lambda a: a.reshape(1, -1)
    outs = [[] for _ in range(8)]
    for l in range(depth):
        zpad = jnp.zeros((LANES - R_W, D_A), F32)
        prm = {
            'w_in': w_in[l].astype(BF16), 'w_out': w_out[l].astype(BF16), 'w_q': w_q[l].astype(BF16),
            'sub_keys': sub_keys[l],
            'mu': row(mu_shift[l]), 'w0': row(w0[l]),
            'ww2': jnp.concatenate([w_w2[l], zpad], axis=0).astype(BF16), 'a0': row(a0[l]),
            'wa2': jnp.concatenate([zpad, w_a2[l]], axis=0).astype(BF16), 'wg2': w_g2[l].astype(BF16),
            'k_k': row(k_k[l]), 'k_a': row(k_a[l]), 'r_k': row(r_k[l]),
            'lnx_w': row(lnx_w[l]), 'lnx_b': row(lnx_b[l]), 'seg': seg, 'segt': seg.T,
            'conv_w': conv_w[l], 'conv_b': row(conv_b[l]),
            'wga': _pair_blockdiag(w_gate_a[l]).astype(BF16), 'bga': row(b_gate_a[l]),
            'wgi': _pair_blockdiag(w_gate_i[l]).astype(BF16), 'bgi': row(b_gate_i[l]),
            'lam': row(lru_lambda[l]),
            'ln1_w': ln1_w[l], 'ln1_b': ln1_b[l], 'ln2_w': ln2_w[l], 'ln2_b': ln2_b[l],
        }
        c_all = jnp.concatenate([c_prompt, c_sample], axis=0)
        mod = _ada(c_all, w_ada[l], b_ada[l])
        mod_p = mod[:bp, None, :]
        mod_s = mod[bp:, None, :]
        x1p, h2p, st_p = _mix_group(
            yp, mod_p, jnp.zeros((bp, C_A), F32), jnp.zeros((bp, N_HEADS, HEAD, HEAD), F32),
            jnp.zeros((bp, D_B), F32), jnp.zeros((bp, CONV_W - 1, D_B), F32), prm,
            ns_mix=1, tb_mix=ROWS, chunk=16, t_valid=ROWS, ns_big=1, tb_big=min(512, tp), alpha=alpha,
            streams=2 if bp % 2 == 0 else 1)
        x1s, h2s, st_s = _mix_group(
            ys, mod_s, None, state_wkv[l], state_lru[l], state_conv[l], prm,
            ns_mix=ROWS // SAMPLE_TPAD, tb_mix=SAMPLE_TPAD, chunk=SAMPLE_TPAD, t_valid=ts,
            ns_big=min(64, bs), tb_big=SAMPLE_TPAD, alpha=alpha, shift_rows=state_shift[l])
        h2s = h2s.reshape(bs, SAMPLE_TPAD, D_MODEL)[:, :ts].reshape(bs * ts, D_MODEL)
        h2 = jnp.concatenate([h2p, h2s], axis=0)
        gmat = _route(h2, prm['w_q'], prm['sub_keys'])
        ff = _experts(h2, gmat, peer_u[l], peer_v[l])
        yp = _ln2(x1p, ff, mod_p, ln2_w[l], ln2_b[l], 1, min(512, tp), alpha)
        ys_real = _ln2(x1s[:, :ts], ff[bp * tp:].reshape(bs, ts, D_MODEL), mod_s, ln2_w[l], ln2_b[l],
                       min(64, bs), ts, alpha)
        ys = jnp.pad(ys_real, ((0, 0), (0, SAMPLE_TPAD - ts), (0, 0)))
        for i, a in enumerate(st_p + st_s):
            outs[i].append(a)
    return (yp, ys[:, :ts]) + tuple(jnp.stack(o) for o in outs)
```

```python
import functools

import jax
import jax.numpy as jnp
from jax import lax
from jax.experimental import pallas as pl
from jax.experimental.pallas import tpu as pltpu

F32 = jnp.float32
BF16 = jnp.bfloat16

D_MODEL = 2048
D_A = 1024
D_B = 1024
HEAD = 64
N_HEADS = D_A // HEAD
N_PAIRS = N_HEADS // 2
R_W = 64
R_A = 64
R_G = 128
C_A = 3 * D_A + R_W + R_A + R_G
C_TOT = C_A + 2 * D_B
CONV_W = 4
C_LRU = 8.0
P_HEADS = 8
N_KEYS = 128
N_EXPERTS = N_KEYS * N_KEYS
TOPK = 16
D_QH = 128
LN_EPS = 1e-5
GN_EPS = 64e-5

LANES = 128
SUBLANES = 8
ROWS = 128
SAMPLE_TPAD = 8
VMEM_LIMIT = 58 * 1024 * 1024


def _cparams(sem):
    return pltpu.CompilerParams(dimension_semantics=sem, vmem_limit_bytes=VMEM_LIMIT)


def _split2(x):
    hi = x.astype(BF16)
    lo = (x - hi.astype(F32)).astype(BF16)
    return hi, lo


_NN = (((1,), (0,)), ((), ()))
_NT = (((1,), (1,)), ((), ()))
_TN = (((0,), (0,)), ((), ()))


def _dg(a, b, dims):
    return lax.dot_general(a, b, dims, preferred_element_type=F32)


def _dot1(a, b, dims=_NN):
    return _dg(a.astype(BF16), b.astype(BF16), dims)


def _dot3(a, b, dims=_NN):
    ah, al = _split2(a)
    bh, bl = _split2(b)
    return _dg(ah, bh, dims) + (_dg(ah, bl, dims) + _dg(al, bh, dims))


_state_dot = _dot1


def _dot_exact_rhs(a, b_exact, dims=_NN):
    a1, a2 = _split2(a)
    return _dg(a1, b_exact, dims) + _dg(a2, b_exact, dims)


def _dot_exact_lhs(a_exact, b, dims=_NN):
    b1, b2 = _split2(b)
    return _dg(a_exact, b1, dims) + _dg(a_exact, b2, dims)


def _sigmoid(x):
    return 1.0 / (1.0 + jnp.exp(-x))


def _softplus(x):
    return jnp.maximum(x, 0.0) + jnp.log(1.0 + jnp.exp(-jnp.abs(x)))


def _gelu(x):
    return 0.5 * x * (1.0 + jnp.tanh(0.7978845608028654 * (x + 0.044715 * (x * x * x))))


def _iota(shape, dim):
    return lax.broadcasted_iota(jnp.int32, shape, dim)


def _ada_kernel(c_ref, w_ref, b_ref, o_ref):
    c = c_ref[...]
    s = c * _sigmoid(c)
    o_ref[...] = _dot3(s, w_ref[...]) + b_ref[...]


def _ada(c, w_ada, b_ada):
    n = c.shape[0]
    tn = 1024
    return pl.pallas_call(
        _ada_kernel,
        out_shape=jax.ShapeDtypeStruct((n, 6 * D_MODEL), F32),
        grid=(6 * D_MODEL // tn,),
        in_specs=[pl.BlockSpec((n, D_MODEL), lambda j: (0, 0)),
                  pl.BlockSpec((D_MODEL, tn), lambda j: (0, j)),
                  pl.BlockSpec((1, tn), lambda j: (0, j))],
        out_specs=pl.BlockSpec((n, tn), lambda j: (0, j)),
        compiler_params=_cparams(("arbitrary",)),
        name="ada",
    )(c, w_ada, b_ada.reshape(1, -1))


def _inproj_kernel(x_ref, sc_ref, sh_ref, w_ref, o_ref, h_ref):
    h = x_ref[...] * (1.0 + sc_ref[...]) + sh_ref[...]
    ns, tb, _ = h.shape
    h_ref[...] = h[:, tb - SUBLANES:, :]
    p = _dot1(h.reshape(ns * tb, D_MODEL), w_ref[...])
    o_ref[...] = p.reshape(ns, tb, p.shape[-1])


def _inproj(x, mod, w_bf16, ns, tb):
    S, T, _ = x.shape
    ncol = w_bf16.shape[1]
    tn = ncol // 3
    proj, h_last = pl.pallas_call(
        _inproj_kernel,
        out_shape=(jax.ShapeDtypeStruct((S, T, ncol), F32),
                   jax.ShapeDtypeStruct((3, S, SUBLANES, D_MODEL), F32)),
        grid=(3, S // ns, T // tb),
        in_specs=[pl.BlockSpec((ns, tb, D_MODEL), lambda j, s, t: (s, t, 0)),
                  pl.BlockSpec((ns, 1, D_MODEL), lambda j, s, t: (s, 0, 1)),
                  pl.BlockSpec((ns, 1, D_MODEL), lambda j, s, t: (s, 0, 0)),
                  pl.BlockSpec((D_MODEL, tn), lambda j, s, t: (0, j))],
        out_specs=(pl.BlockSpec((ns, tb, tn), lambda j, s, t: (s, t, j)),
                   pl.BlockSpec((None, ns, SUBLANES, D_MODEL), lambda j, s, t: (j, s, 0, 0))),
        compiler_params=_cparams(("arbitrary", "arbitrary", "arbitrary")),
        name="inproj",
    )(x, mod, mod, w_bf16)
    return proj, h_last[0]


def _pair_tile(ref, s, pr):
    zero_blk = jnp.zeros((HEAD, HEAD), F32)
    top = jnp.concatenate([ref[s, 2 * pr], zero_blk], axis=1)
    bot = jnp.concatenate([zero_blk, ref[s, 2 * pr + 1]], axis=1)
    return jnp.concatenate([top, bot], axis=0)


def _mixer_stream(proj_ref, s0_ref,
                  mu_ref, w0_ref, ww2_ref, a0_ref, wa2_ref, wg2_ref, kk_ref, ka_ref, rk_ref,
                  lnw_ref, lnb_ref, seg_ref, segt_ref,
                  cw_ref, cb_ref, wga_ref, bga_ref, wgi_ref, bgi_ref, lam_ref,
                  y_ref, sout_ref,
                  prev_ref, hst_ref, st_ref,
                  bp_ref, rp_ref, x2_ref, y2_ref, v_ref, kb_ref, ab_ref, wt_ref, yacc_ref,
                  *, ns, tb, chunk, t_valid, carry_state):
    R = ns * tb
    pair_tile = _pair_tile

    cur = proj_ref[...]
    ext = jnp.concatenate([prev_ref[...], cur[:, :, :C_A + D_B]], axis=1)
    prev_ref[...] = cur[:, tb - SUBLANES:, :C_A + D_B]

    row = _iota((R, 1), 0)
    rowin = row % tb
    valid = rowin < t_valid

    pa = cur[:, :, :C_A].reshape(R, C_A)
    shifted = ext[:, SUBLANES - 1:SUBLANES - 1 + tb, :C_A].reshape(R, C_A)
    p = pa + (shifted - pa) * mu_ref[...]
    r = p[:, :D_A]
    k = p[:, D_A:2 * D_A]
    v = p[:, 2 * D_A:3 * D_A]
    xwa = p[:, 3 * D_A:3 * D_A + LANES]
    xg = p[:, 3 * D_A + LANES:]
    w_log = -_softplus(-(w0_ref[...] + _dot1(jnp.tanh(xwa), ww2_ref[...]))) - 0.5
    logw = -jnp.exp(w_log)
    a = _sigmoid(a0_ref[...] + _dot1(xwa, wa2_ref[...]))
    g = _dot1(_sigmoid(xg), wg2_ref[...])
    yield 900
    seg = seg_ref[...]
    segt = segt_ref[...]

    def seg_sum(x, split=True):
        part = _dot_exact_rhs(x, seg) if split else _dot1(x, seg)
        return _dot_exact_rhs(part, segt)

    kk = k * kk_ref[...]
    kk = kk * lax.rsqrt(jnp.maximum(seg_sum(kk * kk, split=False), 1e-24))
    yield 600
    k = k * (1.0 + (a - 1.0) * ka_ref[...])
    kka = kk * a
    bonus = seg_sum(r * k * rk_ref[...], split=False) * v
    yield 600
    if t_valid < tb:
        logw = jnp.where(valid, logw, 0.0)
        r = jnp.where(valid, r, 0.0)
        k = jnp.where(valid, k, 0.0)
        v = jnp.where(valid, v, 0.0)
        kk = jnp.where(valid, kk, 0.0)
        kka = jnp.where(valid, kka, 0.0)

    ri = _iota((R, R), 0)
    ci = _iota((R, R), 1)
    same = (ri // chunk) == (ci // chunk)
    incl = same & (ci <= ri)
    strict = same & (ci < ri)
    lmat = jnp.concatenate([jnp.where(incl, 1.0, 0.0), jnp.where(same, 1.0, 0.0)], axis=0).astype(BF16)
    cums = _dot_exact_lhs(lmat, logw)
    cum = cums[:R]
    tot = cums[R:]
    yield 500
    e_prev = jnp.exp(cum - logw)
    e_neg = jnp.exp(-cum)
    e_pos = jnp.exp(cum)
    e_rem = jnp.exp(tot - cum)
    beta = kk * e_prev
    alpha = kka * e_neg
    kappa = k * e_neg
    rho = r * e_pos
    yield 600
    v_ref[...] = v
    kb_ref[...] = k * e_rem
    ab_ref[...] = -(kka * e_rem)
    wt_ref[...] = jnp.exp(tot)
    yield 400

    lane = _iota((1, LANES), 1)
    masks = (lane < HEAD, lane >= HEAD)
    eye = jnp.where(ri == ci, 1.0, 0.0)
    nsq = {16: 3, 8: 2}[chunk]
    heads = [(pr, m) for pr in range(N_PAIRS) for m in masks]
    psl = lambda pr: slice(pr * LANES, (pr + 1) * LANES)
    bms = [jnp.where(m, beta[:, psl(pr)], 0.0) for pr, m in heads]
    rms = [jnp.where(m, rho[:, psl(pr)], 0.0) for pr, m in heads]
    yield 300
    gms = []
    for (pr, m), bm, rm in zip(heads, bms, rms):
        gms.append(_dot1(jnp.concatenate([bm, rm], axis=0),
                         jnp.concatenate([alpha[:, psl(pr)], kappa[:, psl(pr)]], axis=0), _NT))
        yield 70
    l_bas = [jnp.where(strict, gm[:R, :R], 0.0) for gm in gms]
    m_ras = [jnp.where(incl, gm[R:, :R], 0.0) for gm in gms]
    xy1s = []
    for (pr, m), gm in zip(heads, gms):
        xy1s.append(_dot1(jnp.concatenate([jnp.where(strict, gm[:R, R:], 0.0),
                                           jnp.where(incl, gm[R:, R:], 0.0)], axis=0), v[:, psl(pr)]))
        yield 70
    pws = []
    for l in l_bas:
        pws.append(_dot1(l, l))
        yield 40

    def lru_steps():
        for pr in range(N_PAIRS):
            sl = psl(pr)
            bsl = slice(C_A + pr * LANES, C_A + (pr + 1) * LANES)
            gsl = slice(C_A + D_B + pr * LANES, C_A + D_B + (pr + 1) * LANES)
            xc = cb_ref[:, sl]
            for j in range(CONV_W):
                off = SUBLANES - (CONV_W - 1) + j
                xc = xc + ext[:, off:off + tb, bsl].reshape(R, LANES) * cw_ref[j:j + 1, sl]
            yield
            rg = _sigmoid(_dot1(xc, wga_ref[pr]) + bga_ref[:, sl])
            ig = _sigmoid(_dot1(xc, wgi_ref[pr]) + bgi_ref[:, sl])
            yield
            log_a = -C_LRU * rg * _softplus(-lam_ref[:, sl])
            av = jnp.exp(log_a)
            bv = jnp.sqrt(1.0 - jnp.exp(2.0 * log_a)) * (ig * xc)
            if t_valid < tb:
                av = jnp.where(valid, av, 1.0)
                bv = jnp.where(valid, bv, 0.0)
            yield
            d = 1
            while d < tb:
                take = rowin >= d
                a_sh = jnp.where(take, pltpu.roll(av, d, axis=0), 1.0)
                b_sh = jnp.where(take, pltpu.roll(bv, d, axis=0), 0.0)
                bv = av * b_sh + bv
                av = av * a_sh
                d *= 2
                yield
            h0 = jnp.broadcast_to(hst_ref[:, SUBLANES - 1:SUBLANES, sl], (ns, tb, LANES)).reshape(R, LANES)
            h3 = (av * h0 + bv).reshape(ns, tb, LANES)
            hst_ref[:, :, sl] = h3[:, tb - SUBLANES:, :]
            gb = cur[:, :, gsl].reshape(R, LANES)
            y_ref[:, :, D_A + pr * LANES:D_A + (pr + 1) * LANES] = h3 * _gelu(gb).reshape(ns, tb, LANES)
            yield

    lru = lru_steps()
    minvs = [eye - l for l in l_bas]
    for i in range(nsq):
        nxt = []
        for mi, pw in zip(minvs, pws):
            nxt.append(mi + _dot1(mi, pw))
            next(lru, None)
            yield 60
        minvs = nxt
        if i + 1 < nsq:
            nxt = []
            for pw in pws:
                nxt.append(_dot1(pw, pw))
                next(lru, None)
                yield 60
            pws = nxt
    for _ in lru:
        yield 40
    t1s = []
    for mi, xy1, bm in zip(minvs, xy1s, bms):
        t1s.append(_dot1(mi, jnp.concatenate([xy1[:R], bm], axis=1)))
        yield 50
    t2s = []
    for m_ra, t1 in zip(m_ras, t1s):
        t2s.append(_dot1(m_ra, t1))
        yield 50
    for pr in range(N_PAIRS):
        h0, h1 = 2 * pr, 2 * pr + 1
        sl = psl(pr)
        m1 = masks[1]
        bp_ref[:, sl] = t1s[h0][:, LANES:] + t1s[h1][:, LANES:]
        rp_ref[:, sl] = (rms[h0] - t2s[h0][:, LANES:]) + (rms[h1] - t2s[h1][:, LANES:])
        x2_ref[:, sl] = jnp.where(m1, t1s[h1][:, :LANES], t1s[h0][:, :LANES])
        y2_ref[:, sl] = jnp.where(m1, xy1s[h1][R:] - t2s[h1][:, :LANES], xy1s[h0][R:] - t2s[h0][:, :LANES])

    bi = _iota((LANES, LANES), 0)
    bj = _iota((LANES, LANES), 1)
    bd = (bi < HEAD) == (bj < HEAD)
    nck = tb // chunk
    sites = [(s, c, pr) for s in range(ns) for c in range(nck) for pr in range(N_PAIRS)]
    rows_of = lambda s, c: slice(s * tb + c * chunk, s * tb + (c + 1) * chunk)
    thetas = {}
    psis = {}
    for s, c, pr in sites:
        rs, sl = rows_of(s, c), psl(pr)
        thetas[s, c, pr] = jnp.where(bd, _state_dot(bp_ref[rs, sl], ab_ref[rs, sl], _TN), 0.0)
        vx = jnp.concatenate([v_ref[rs, sl], x2_ref[rs, sl]], axis=0)
        ka = jnp.concatenate([kb_ref[rs, sl], ab_ref[rs, sl]], axis=0)
        psis[s, c, pr] = jnp.where(bd, _state_dot(vx, ka, _TN), 0.0)
        yield 60
    for s in range(ns):
        sps = [st_ref[s, pr] if carry_state else pair_tile(s0_ref, s, pr) for pr in range(N_PAIRS)]
        for c in range(nck):
            rs = rows_of(s, c)
            for pr in range(N_PAIRS):
                yacc_ref[rs, psl(pr)] = _state_dot(rp_ref[rs, psl(pr)], sps[pr], _NT) + y2_ref[rs, psl(pr)]
            yield 100
            nxt = []
            for pr in range(N_PAIRS):
                nxt.append(sps[pr] * wt_ref[rs.start:rs.start + 1, psl(pr)]
                           + (_state_dot(sps[pr], thetas[s, c, pr]) + psis[s, c, pr]))
                yield 50
            sps = nxt
        for pr in range(N_PAIRS):
            if carry_state:
                st_ref[s, pr] = sps[pr]
            else:
                sout_ref[s, 2 * pr] = sps[pr][:HEAD, :HEAD]
                sout_ref[s, 2 * pr + 1] = sps[pr][HEAD:, HEAD:]
        yield 50

    y = yacc_ref[...]
    mean = seg_sum(y) * (1.0 / HEAD)
    yc = y - mean
    yield 400
    var = seg_sum(yc * yc) * (1.0 / HEAD)
    yn = yc * lax.rsqrt(var + GN_EPS) * lnw_ref[...] + lnb_ref[...]
    ya = (yn + bonus) * g
    y_ref[:, :, :D_A] = ya.reshape(ns, tb, D_A)
    yield 600


def _mixer_kernel(proj_ref, pprev_ref, conv0_ref, s0_ref, lru0_ref, *refs, ns, tb, chunk, t_valid, carry_state, streams):
    n_par = 20
    params = refs[:n_par]
    y_ref, sout_ref, lruo_ref, convo_ref, prev_ref, hst_ref, st_ref = refs[n_par:n_par + 7]
    slabs = refs[n_par + 7:]
    t = pl.program_id(1)

    @pl.when(t == 0)
    def _():
        prev_ref[...] = jnp.zeros_like(prev_ref)
        prev_ref[:, SUBLANES - 1:, :C_A] = pprev_ref[...]
        prev_ref[:, SUBLANES - (CONV_W - 1):, C_A:] = conv0_ref[...]
        hst_ref[...] = jnp.broadcast_to(lru0_ref[...], hst_ref.shape)
        if carry_state:
            for q in range(streams):
                for pr in range(N_PAIRS):
                    st_ref[q, pr] = _pair_tile(s0_ref, q, pr)

    def make(q):
        sq = pl.ds(q * ns, ns)
        return _mixer_stream(
            proj_ref.at[sq], s0_ref.at[sq], *params, y_ref.at[sq], sout_ref.at[sq],
            prev_ref.at[sq], hst_ref.at[sq], st_ref.at[pl.ds(q, 1)], *[r.at[q] for r in slabs],
            ns=ns, tb=tb, chunk=chunk, t_valid=t_valid, carry_state=carry_state)

    gens = [make(q) for q in range(streams)]
    clock = [0.0] * streams
    live = list(range(streams))
    while live:
        q = min(live, key=lambda i: clock[i])
        cost = next(gens[q], None)
        if cost is None:
            live.remove(q)
        else:
            clock[q] += cost

    @pl.when(t == pl.num_programs(1) - 1)
    def _():
        lruo_ref[...] = hst_ref[...]
        convo_ref[...] = proj_ref[:, tb - SUBLANES:, C_A:C_A + D_B]
        if carry_state:
            for q in range(streams):
                for pr in range(N_PAIRS):
                    tile = st_ref[q, pr]
                    sout_ref[q, 2 * pr] = tile[:HEAD, :HEAD]
                    sout_ref[q, 2 * pr + 1] = tile[HEAD:, HEAD:]


def _mixer(proj, p_prev, conv0, s0, lru0, prm, ns, tb, chunk, t_valid, streams=1):
    S, T = s0.shape[0], proj.shape[1]
    R = ns * tb
    assert R == ROWS
    carry_state = T // tb > 1
    assert ns == 1 or not carry_state
    kern = functools.partial(_mixer_kernel, ns=ns, tb=tb, chunk=chunk, t_valid=t_valid, carry_state=carry_state,
                             streams=streams)
    nb = ns * streams

    def full(a):
        nd = a.ndim
        return pl.BlockSpec(a.shape, lambda s, t, nd=nd: (0,) * nd)

    params = [prm[n] for n in ('mu', 'w0', 'ww2', 'a0', 'wa2', 'wg2', 'k_k', 'k_a', 'r_k', 'lnx_w', 'lnx_b',
                               'seg', 'segt', 'conv_w', 'conv_b', 'wga', 'bga', 'wgi', 'bgi', 'lam')]
    slab = lambda w: pltpu.VMEM((streams, R, w), F32)
    return pl.pallas_call(
        kern,
        out_shape=(jax.ShapeDtypeStruct((S, T, D_A + D_B), F32),
                   jax.ShapeDtypeStruct((S, N_HEADS, HEAD, HEAD), F32),
                   jax.ShapeDtypeStruct((S, SUBLANES, D_B), F32),
                   jax.ShapeDtypeStruct((S, SUBLANES, D_B), F32)),
        grid=(S // nb, T // tb),
        in_specs=[pl.BlockSpec((nb, tb, C_TOT), lambda s, t: (s, t, 0)),
                  pl.BlockSpec((nb, 1, C_A), lambda s, t: (s, 0, 0)),
                  pl.BlockSpec((nb, CONV_W - 1, D_B), lambda s, t: (s, 0, 0)),
                  pl.BlockSpec((nb, N_HEADS, HEAD, HEAD), lambda s, t: (s, 0, 0, 0)),
                  pl.BlockSpec((nb, 1, D_B), lambda s, t: (s, 0, 0))]
                 + [full(a) for a in params],
        out_specs=(pl.BlockSpec((nb, tb, D_A + D_B), lambda s, t: (s, t, 0)),
                   pl.BlockSpec((nb, N_HEADS, HEAD, HEAD), lambda s, t: (s, 0, 0, 0),
                                pipeline_mode=pl.Buffered(1)),
                   pl.BlockSpec((nb, SUBLANES, D_B), lambda s, t: (s, 0, 0)),
                   pl.BlockSpec((nb, SUBLANES, D_B), lambda s, t: (s, 0, 0))),
        scratch_shapes=[pltpu.VMEM((nb, SUBLANES, C_A + D_B), F32),
                        pltpu.VMEM((nb, SUBLANES, D_B), F32),
                        pltpu.VMEM((streams, N_PAIRS, LANES, LANES), F32)]
                       + [slab(D_A) for _ in range(9)],
        compiler_params=_cparams(("arbitrary", "arbitrary")),
        name="mixer",
    )(proj, p_prev, conv0, s0, lru0, *params)


def _layernorm(x, w, b):
    mean = jnp.mean(x, axis=-1, keepdims=True)
    xc = x - mean
    var = jnp.mean(xc * xc, axis=-1, keepdims=True)
    return xc * lax.rsqrt(var + LN_EPS) * w + b


def _outproj_kernel(y_ref, x_ref, g1_ref, sc2_ref, sh2_ref, w_ref, lnw_ref, lnb_ref, x1_ref, h2_ref, *, alpha):
    ns, tb, _ = y_ref.shape
    mix = _dot1(y_ref[...].reshape(ns * tb, D_MODEL), w_ref[...]).reshape(ns, tb, D_MODEL)
    x1 = _layernorm(alpha * x_ref[...] + (1.0 + g1_ref[...]) * mix, lnw_ref[...], lnb_ref[...])
    x1_ref[...] = x1
    h2 = x1 * (1.0 + sc2_ref[...]) + sh2_ref[...]
    h2_ref[...] = h2.reshape(ns * tb, D_MODEL).astype(BF16)


def _outproj(y, x, mod, w_bf16, ln_w, ln_b, ns, tb, alpha):
    S, T, _ = x.shape
    nt = T // tb
    blk = pl.BlockSpec((ns, tb, D_MODEL), lambda s, t: (s, t, 0))
    modspec = lambda i: pl.BlockSpec((ns, 1, D_MODEL), lambda s, t, i=i: (s, 0, i))
    vec = pl.BlockSpec((1, D_MODEL), lambda s, t: (0, 0))
    return pl.pallas_call(
        functools.partial(_outproj_kernel, alpha=alpha),
        out_shape=(jax.ShapeDtypeStruct((S, T, D_MODEL), F32), jax.ShapeDtypeStruct((S * T, D_MODEL), BF16)),
        grid=(S // ns, nt),
        in_specs=[blk, blk, modspec(2), modspec(4), modspec(3),
                  pl.BlockSpec((D_MODEL, D_MODEL), lambda s, t: (0, 0)), vec, vec],
        out_specs=(blk, pl.BlockSpec((ns * tb, D_MODEL), lambda s, t: (s * nt + t, 0))),
        compiler_params=_cparams(("arbitrary", "arbitrary")),
        name="outproj",
    )(y, x, mod, mod, mod, w_bf16, ln_w.reshape(1, -1), ln_b.reshape(1, -1))


def _topk_rows(s, ridx, n_rows):
    out_i = _iota((TOPK, s.shape[1]), 0)
    vals = jnp.zeros((TOPK, s.shape[1]), F32)
    idxs = jnp.zeros((TOPK, s.shape[1]), F32)
    for it in range(TOPK):
        m = jnp.max(s, axis=0, keepdims=True)
        idx = jnp.min(jnp.where(s == m, ridx, float(n_rows)), axis=0, keepdims=True)
        vals = jnp.where(out_i == it, m, vals)
        idxs = jnp.where(out_i == it, idx, idxs)
        s = jnp.where(ridx == idx, -jnp.inf, s)
    return vals, idxs


def _route_kernel(h_ref, wq_ref, sk_ref, g_ref, gate_ref, e1_ref, e2_ref):
    R = ROWS
    q = _dot1(h_ref[...], wq_ref[...])
    sk0 = sk_ref[0]
    sk1 = sk_ref[1]
    gates, e1s, e2s = [], [], []
    key_rows = _iota((N_KEYS, R), 0).astype(F32)
    r16 = _iota((TOPK, R), 0).astype(F32)
    r8 = _iota((SUBLANES, R), 0).astype(F32)
    ea_rank = jnp.where(r8 < 3, 2.0, jnp.where(r8 < 5, 3.0, 4.0))
    eb_rank = jnp.where((r8 == 0) | (r8 == 3) | (r8 == 5), 2.0, jnp.where((r8 == 1) | (r8 == 4), 3.0, 4.0))
    flat = jnp.concatenate([r16, TOPK + r8, TOPK * r16, TOPK * r8 + 1.0, TOPK * ea_rank + eb_rank], axis=0)
    neg = -jnp.inf
    for hd in range(P_HEADS):
        base = hd * 2 * D_QH
        s1 = _dot1(sk0, q[:, base:base + D_QH], _NT)
        s2 = _dot1(sk1, q[:, base + D_QH:base + 2 * D_QH], _NT)
        v1, i1 = _topk_rows(s1, key_rows, N_KEYS)
        v2, i2 = _topk_rows(s2, key_rows, N_KEYS)
        ea = jnp.where(r8 < 3, v1[2:3], jnp.where(r8 < 5, v1[3:4], v1[4:5]))
        eb = jnp.where(eb_rank == 2.0, v2[2:3], jnp.where(eb_rank == 3.0, v2[3:4], v2[4:5]))
        cand = jnp.concatenate([
            v1[0:1] + v2,
            v1[1:2] + v2[:SUBLANES],
            jnp.where(r16 >= 2, v1 + v2[0:1], neg),
            jnp.where(r8 >= 2, v1[:SUBLANES] + v2[1:2], neg),
            jnp.where(r8 < 6, ea + eb, neg),
        ], axis=0)
        sv, ci = _topk_rows(cand, flat, TOPK * TOPK)
        ca = jnp.floor(ci * (1.0 / TOPK))
        cb = ci - TOPK * ca
        e1 = jnp.zeros((TOPK, R), F32)
        e2 = jnp.zeros((TOPK, R), F32)
        for a in range(TOPK):
            e1 = jnp.where(ca == float(a), i1[a:a + 1, :], e1)
            e2 = jnp.where(cb == float(a), i2[a:a + 1, :], e2)
        ex = jnp.exp(sv - jnp.max(sv, axis=0, keepdims=True))
        gates.append(ex / jnp.sum(ex, axis=0, keepdims=True))
        e1s.append(e1)
        e2s.append(e2)
    gate_ref[...] = jnp.concatenate(gates, axis=0).T
    e1_ref[...] = jnp.concatenate(e1s, axis=0).T
    e2_ref[...] = jnp.concatenate(e2s, axis=0).T
    key_i = _iota((N_KEYS, P_HEADS * TOPK), 0).astype(F32)

    def per_octet(o, carry):
        base = pl.multiple_of(o * SUBLANES, SUBLANES)
        tiles = []
        for j in range(SUBLANES):
            sel1 = key_i == e1_ref[pl.ds(base + j, 1), :]
            sel2 = key_i == e2_ref[pl.ds(base + j, 1), :]
            m1 = jnp.where(sel1, gate_ref[pl.ds(base + j, 1), :], 0.0)
            m2 = jnp.where(sel2, 1.0, 0.0)
            tiles.append(_dot1(m1, m2, _NT))
        by_key = jnp.swapaxes(jnp.stack(tiles, axis=0), 0, 1)
        for i1v in range(N_KEYS):
            g_ref[pl.ds(base, SUBLANES), i1v * N_KEYS:(i1v + 1) * N_KEYS] = by_key[i1v]
        return carry

    lax.fori_loop(0, R // SUBLANES, per_octet, 0, unroll=8)


def _route(h2, wq_bf16, sub_keys):
    n = h2.shape[0]
    assert n % ROWS == 0
    return pl.pallas_call(
        _route_kernel,
        out_shape=jax.ShapeDtypeStruct((n, N_EXPERTS), F32),
        grid=(n // ROWS,),
        in_specs=[pl.BlockSpec((ROWS, D_MODEL), lambda i: (i, 0)),
                  pl.BlockSpec((D_MODEL, D_MODEL), lambda i: (0, 0)),
                  pl.BlockSpec((2, N_KEYS, D_QH), lambda i: (0, 0, 0))],
        out_specs=pl.BlockSpec((ROWS, N_EXPERTS), lambda i: (i, 0)),
        scratch_shapes=[pltpu.VMEM((ROWS, P_HEADS * TOPK), F32) for _ in range(3)],
        compiler_params=_cparams(("arbitrary",)),
        name="route",
    )(h2, wq_bf16, sub_keys)


EXPERT_CHUNK = 512
EXPERT_ROWS_MAX = 1152


EXPERT_SPLIT = 4


def _experts_kernel(h_ref, g_ref, u_ref, v_ref, o_ref):
    @pl.when(pl.program_id(1) == 0)
    def _():
        o_ref[...] = jnp.zeros_like(o_ref)

    u = u_ref[...].astype(BF16)
    v = v_ref[...].astype(BF16)
    rb = h_ref.shape[0] // EXPERT_SPLIT
    rows = [pl.ds(b * rb, rb) for b in range(EXPERT_SPLIT)]
    act = _dg(h_ref[rows[0], :], u, _NT)
    for b in range(EXPERT_SPLIT):
        nxt = _dg(h_ref[rows[b + 1], :], u, _NT) if b + 1 < EXPERT_SPLIT else None
        w = (g_ref[rows[b], :] * _gelu(act)).astype(BF16)
        o_ref[rows[b], :] += _dg(w, v, _NN)
        act = nxt


def _token_block(n, cap):
    for tb in range(min(n, cap), 0, -1):
        if n % tb == 0 and tb % (2 * SUBLANES * EXPERT_SPLIT) == 0:
            return tb
    raise ValueError(n)


def _experts(h2, gmat, peer_u, peer_v):
    n = h2.shape[0]
    tb = _token_block(n, EXPERT_ROWS_MAX)
    ec = EXPERT_CHUNK
    return pl.pallas_call(
        _experts_kernel,
        out_shape=jax.ShapeDtypeStruct((n, D_MODEL), F32),
        grid=(n // tb, N_EXPERTS // ec),
        in_specs=[pl.BlockSpec((tb, D_MODEL), lambda i, e: (i, 0)),
                  pl.BlockSpec((tb, ec), lambda i, e: (i, e)),
                  pl.BlockSpec((ec, D_MODEL), lambda i, e: (e, 0)),
                  pl.BlockSpec((ec, D_MODEL), lambda i, e: (e, 0))],
        out_specs=pl.BlockSpec((tb, D_MODEL), lambda i, e: (i, 0), pipeline_mode=pl.Buffered(1)),
        compiler_params=_cparams(("arbitrary", "arbitrary")),
        name="experts",
    )(h2, gmat, peer_u, peer_v)


def _ln2_kernel(x1_ref, ff_ref, g2_ref, lnw_ref, lnb_ref, o_ref, *, alpha):
    ff = ff_ref[...].reshape(x1_ref.shape)
    o_ref[...] = _layernorm(alpha * x1_ref[...] + (1.0 + g2_ref[...]) * ff, lnw_ref[...], lnb_ref[...])


def _ln2(x1, ff, mod, ln_w, ln_b, ns, tb, alpha):
    S, T, _ = x1.shape
    nt = T // tb
    blk = pl.BlockSpec((ns, tb, D_MODEL), lambda s, t: (s, t, 0))
    vec = pl.BlockSpec((1, D_MODEL), lambda s, t: (0, 0))
    if ff.ndim == 2:
        assert ns == 1
        ffspec = pl.BlockSpec((tb, D_MODEL), lambda s, t: (s * nt + t, 0))
    else:
        ffspec = blk
    return pl.pallas_call(
        functools.partial(_ln2_kernel, alpha=alpha),
        out_shape=jax.ShapeDtypeStruct((S, T, D_MODEL), F32),
        grid=(S // ns, nt),
        in_specs=[blk, ffspec, pl.BlockSpec((ns, 1, D_MODEL), lambda s, t: (s, 0, 5)), vec, vec],
        out_specs=blk,
        compiler_params=_cparams(("arbitrary", "arbitrary")),
        name="ln2",
    )(x1, ff, mod, ln_w.reshape(1, -1), ln_b.reshape(1, -1))


def _pair_blockdiag(w):
    n = w.shape[0] // 2
    w = w.reshape(n, 2, HEAD, HEAD)
    z = jnp.zeros((n, HEAD, HEAD), w.dtype)
    top = jnp.concatenate([w[:, 0], z], axis=2)
    bot = jnp.concatenate([z, w[:, 1]], axis=2)
    return jnp.concatenate([top, bot], axis=1)


def _mix_group(x, mod, p_prev, wkv0, lru0, conv0, prm, *, ns_mix, tb_mix, chunk, t_valid, ns_big, tb_big, alpha,
               streams=1, shift_rows=None):
    S, T, _ = x.shape
    if shift_rows is None:
        proj, h_last = _inproj(x, mod, prm['w_in'], ns_big, tb_big)
    else:
        extra = S // T
        x_ext = jnp.concatenate([x, shift_rows.reshape(extra, T, D_MODEL)], axis=0)
        mod_ext = jnp.concatenate([mod, jnp.zeros((extra, 1, 6 * D_MODEL), F32)], axis=0)
        ns_ext = max(d for d in range(1, ns_big + extra // 2 + 1) if (S + extra) % d == 0)
        proj, h_last = _inproj(x_ext, mod_ext, prm['w_in'], ns_ext, tb_big)
        p_prev = proj[S:].reshape(S, C_TOT)[:, :C_A]
        h_last = h_last[:S]
    y, wkv, lru_o, conv_o = _mixer(proj, p_prev[:, None, :], conv0, wkv0, lru0[:, None, :], prm,
                                   ns_mix, tb_mix, chunk, t_valid, streams)
    x1, h2 = _outproj(y, x, mod, prm['w_out'], prm['ln1_w'], prm['ln1_b'], ns_big, tb_big, alpha)
    tv = (t_valid - 1) % SUBLANES
    shift = h_last[:, tv]
    lru = lru_o[:, SUBLANES - 1]
    conv = conv_o[:, tv - (CONV_W - 2):tv + 1]
    return x1, h2, (shift, wkv, lru, conv)


def kernel(x_prompt, x_sample, c_prompt, c_sample, state_shift, state_wkv, state_lru, state_conv, w_ada, b_ada, w_in, mu_shift, w0, w_w2, a0, w_a2, w_g2, k_k, k_a, r_k, lnx_w, lnx_b, conv_w, conv_b, w_gate_a, b_gate_a, w_gate_i, b_gate_i, lru_lambda, w_out, ln1_w, ln1_b, w_q, sub_keys, peer_u, peer_v, ln2_w, ln2_b):
    depth = w_ada.shape[0]
    alpha = (2 * depth) ** 0.25
    bp, tp, _ = x_prompt.shape
    bs, ts, _ = x_sample.shape
    yp = x_prompt
    ys = jnp.pad(x_sample, ((0, 0), (0, SAMPLE_TPAD - ts), (0, 0)))
    head_of = jnp.arange(D_A) // HEAD
    seg = (head_of[:, None] == jnp.arange(LANES)[None, :]).astype(BF16)
    row = lambda a: a.reshape(1, -1)
    outs = [[] for _ in range(8)]
    for l in range(depth):
        zpad = jnp.zeros((LANES - R_W, D_A), F32)
        prm = {
            'w_in': w_in[l].astype(BF16), 'w_out': w_out[l].astype(BF16), 'w_q': w_q[l].astype(BF16),
            'sub_keys': sub_keys[l],
            'mu': row(mu_shift[l]), 'w0': row(w0[l]),
            'ww2': jnp.concatenate([w_w2[l], zpad], axis=0).astype(BF16), 'a0': row(a0[l]),
            'wa2': jnp.concatenate([zpad, w_a2[l]], axis=0).astype(BF16), 'wg2': w_g2[l].astype(BF16),
            'k_k': row(k_k[l]), 'k_a': row(k_a[l]), 'r_k': row(r_k[l]),
            'lnx_w': row(lnx_w[l]), 'lnx_b': row(lnx_b[l]), 'seg': seg, 'segt': seg.T,
            'conv_w': conv_w[l], 'conv_b': row(conv_b[l]),
            'wga': _pair_blockdiag(w_gate_a[l]).astype(BF16), 'bga': row(b_gate_a[l]),
            'wgi': _pair_blockdiag(w_gate_i[l]).astype(BF16), 'bgi': row(b_gate_i[l]),
            'lam': row(lru_lambda[l]),
            'ln1_w': ln1_w[l], 'ln1_b': ln1_b[l], 'ln2_w': ln2_w[l], 'ln2_b': ln2_b[l],
        }
        c_all = jnp.concatenate([c_prompt, c_sample], axis=0)
        mod = _ada(c_all, w_ada[l], b_ada[l])
        mod_p = mod[:bp, None, :]
        mod_s = mod[bp:, None, :]
        x1p, h2p, st_p = _mix_group(
            yp, mod_p, jnp.zeros((bp, C_A), F32), jnp.zeros((bp, N_HEADS, HEAD, HEAD), F32),
            jnp.zeros((bp, D_B), F32), jnp.zeros((bp, CONV_W - 1, D_B), F32), prm,
            ns_mix=1, tb_mix=ROWS, chunk=16, t_valid=ROWS, ns_big=1, tb_big=min(512, tp), alpha=alpha,
            streams=2 if bp % 2 == 0 else 1)
        x1s, h2s, st_s = _mix_group(
            ys, mod_s, None, state_wkv[l], state_lru[l], state_conv[l], prm,
            ns_mix=ROWS // SAMPLE_TPAD, tb_mix=SAMPLE_TPAD, chunk=SAMPLE_TPAD, t_valid=ts,
            ns_big=min(64, bs), tb_big=SAMPLE_TPAD, alpha=alpha, shift_rows=state_shift[l])
        h2s = h2s.reshape(bs, SAMPLE_TPAD, D_MODEL)[:, :ts].reshape(bs * ts, D_MODEL)
        h2 = jnp.concatenate([h2p, h2s], axis=0)
        gmat = _route(h2, prm['w_q'], prm['sub_keys'])
        ff = _experts(h2, gmat, peer_u[l], peer_v[l])
        yp = _ln2(x1p, ff, mod_p, ln2_w[l], ln2_b[l], 1, min(512, tp), alpha)
        ys_real = _ln2(x1s[:, :ts], ff[bp * tp:].reshape(bs, ts, D_MODEL), mod_s, ln2_w[l], ln2_b[l],
                       min(64, bs), ts, alpha)
        ys = jnp.pad(ys_real, ((0, 0), (0, SAMPLE_TPAD - ts), (0, 0)))
        for i, a in enumerate(st_p + st_s):
            outs[i].append(a)
    return (yp, ys[:, :ts]) + tuple(jnp.stack(o) for o in outs)
```

```python
import functools

import jax
import jax.numpy as jnp
from jax import lax
from jax.experimental import pallas as pl
from jax.experimental.pallas import tpu as pltpu

F32 = jnp.float32
BF16 = jnp.bfloat16

D_MODEL = 2048
D_A = 1024
D_B = 1024
HEAD = 64
N_HEADS = D_A // HEAD
N_PAIRS = N_HEADS // 2
R_W = 64
R_A = 64
R_G = 128
C_A = 3 * D_A + R_W + R_A + R_G
C_TOT = C_A + 2 * D_B
CONV_W = 4
C_LRU = 8.0
P_HEADS = 8
N_KEYS = 128
N_EXPERTS = N_KEYS * N_KEYS
TOPK = 16
D_QH = 128
LN_EPS = 1e-5
GN_EPS = 64e-5

LANES = 128
SUBLANES = 8
ROWS = 128
SAMPLE_TPAD = 8
VMEM_LIMIT = 58 * 1024 * 1024


def _cparams(sem):
    return pltpu.CompilerParams(dimension_semantics=sem, vmem_limit_bytes=VMEM_LIMIT)


def _split2(x):
    hi = x.astype(BF16)
    lo = (x - hi.astype(F32)).astype(BF16)
    return hi, lo


_NN = (((1,), (0,)), ((), ()))
_NT = (((1,), (1,)), ((), ()))
_TN = (((0,), (0,)), ((), ()))


def _dg(a, b, dims):
    return lax.dot_general(a, b, dims, preferred_element_type=F32)


def _dot1(a, b, dims=_NN):
    return _dg(a.astype(BF16), b.astype(BF16), dims)


def _dot3(a, b, dims=_NN):
    ah, al = _split2(a)
    bh, bl = _split2(b)
    return _dg(ah, bh, dims) + (_dg(ah, bl, dims) + _dg(al, bh, dims))


_state_dot = _dot1


def _dot_exact_rhs(a, b_exact, dims=_NN):
    a1, a2 = _split2(a)
    return _dg(a1, b_exact, dims) + _dg(a2, b_exact, dims)


def _dot_exact_lhs(a_exact, b, dims=_NN):
    b1, b2 = _split2(b)
    return _dg(a_exact, b1, dims) + _dg(a_exact, b2, dims)


def _sigmoid(x):
    return 1.0 / (1.0 + jnp.exp(-x))


def _softplus(x):
    return jnp.maximum(x, 0.0) + jnp.log(1.0 + jnp.exp(-jnp.abs(x)))


def _gelu(x):
    return 0.5 * x * (1.0 + jnp.tanh(0.7978845608028654 * (x + 0.044715 * (x * x * x))))


def _iota(shape, dim):
    return lax.broadcasted_iota(jnp.int32, shape, dim)


def _ada_kernel(c_ref, w_ref, b_ref, o_ref):
    c = c_ref[...]
    s = c * _sigmoid(c)
    o_ref[...] = _dot3(s, w_ref[...]) + b_ref[...]


def _ada(c, w_ada, b_ada):
    n = c.shape[0]
    tn = 1024
    return pl.pallas_call(
        _ada_kernel,
        out_shape=jax.ShapeDtypeStruct((n, 6 * D_MODEL), F32),
        grid=(6 * D_MODEL // tn,),
        in_specs=[pl.BlockSpec((n, D_MODEL), lambda j: (0, 0)),
                  pl.BlockSpec((D_MODEL, tn), lambda j: (0, j)),
                  pl.BlockSpec((1, tn), lambda j: (0, j))],
        out_specs=pl.BlockSpec((n, tn), lambda j: (0, j)),
        compiler_params=_cparams(("arbitrary",)),
        name="ada",
    )(c, w_ada, b_ada.reshape(1, -1))


def _inproj_kernel(x_ref, sc_ref, sh_ref, w_ref, o_ref, h_ref):
    h = x_ref[...] * (1.0 + sc_ref[...]) + sh_ref[...]
    ns, tb, _ = h.shape
    h_ref[...] = h[:, tb - SUBLANES:, :]
    p = _dot1(h.reshape(ns * tb, D_MODEL), w_ref[...])
    o_ref[...] = p.reshape(ns, tb, p.shape[-1])


def _inproj(x, mod, w_bf16, ns, tb):
    S, T, _ = x.shape
    ncol = w_bf16.shape[1]
    tn = ncol // 3
    proj, h_last = pl.pallas_call(
        _inproj_kernel,
        out_shape=(jax.ShapeDtypeStruct((S, T, ncol), F32),
                   jax.ShapeDtypeStruct((3, S, SUBLANES, D_MODEL), F32)),
        grid=(3, S // ns, T // tb),
        in_specs=[pl.BlockSpec((ns, tb, D_MODEL), lambda j, s, t: (s, t, 0)),
                  pl.BlockSpec((ns, 1, D_MODEL), lambda j, s, t: (s, 0, 1)),
                  pl.BlockSpec((ns, 1, D_MODEL), lambda j, s, t: (s, 0, 0)),
                  pl.BlockSpec((D_MODEL, tn), lambda j, s, t: (0, j))],
        out_specs=(pl.BlockSpec((ns, tb, tn), lambda j, s, t: (s, t, j)),
                   pl.BlockSpec((None, ns, SUBLANES, D_MODEL), lambda j, s, t: (j, s, 0, 0))),
        compiler_params=_cparams(("arbitrary", "arbitrary", "arbitrary")),
        name="inproj",
    )(x, mod, mod, w_bf16)
    return proj, h_last[0]


def _pair_tile(ref, s, pr):
    zero_blk = jnp.zeros((HEAD, HEAD), F32)
    top = jnp.concatenate([ref[s, 2 * pr], zero_blk], axis=1)
    bot = jnp.concatenate([zero_blk, ref[s, 2 * pr + 1]], axis=1)
    return jnp.concatenate([top, bot], axis=0)


def _mixer_stream(proj_ref, s0_ref,
                  mu_ref, w0_ref, ww2_ref, a0_ref, wa2_ref, wg2_ref, kk_ref, ka_ref, rk_ref,
                  lnw_ref, lnb_ref, seg_ref, segt_ref,
                  cw_ref, cb_ref, wga_ref, bga_ref, wgi_ref, bgi_ref, lam_ref,
                  y_ref, sout_ref,
                  prev_ref, hst_ref, st_ref,
                  bp_ref, rp_ref, x2_ref, y2_ref, v_ref, kb_ref, ab_ref, wt_ref, yacc_ref,
                  *, ns, tb, chunk, t_valid, carry_state):
    R = ns * tb
    pair_tile = _pair_tile

    cur = proj_ref[...]
    ext = jnp.concatenate([prev_ref[...], cur[:, :, :C_A + D_B]], axis=1)
    prev_ref[...] = cur[:, tb - SUBLANES:, :C_A + D_B]

    row = _iota((R, 1), 0)
    rowin = row % tb
    valid = rowin < t_valid

    pa = cur[:, :, :C_A].reshape(R, C_A)
    shifted = ext[:, SUBLANES - 1:SUBLANES - 1 + tb, :C_A].reshape(R, C_A)
    p = pa + (shifted - pa) * mu_ref[...]
    r = p[:, :D_A]
    k = p[:, D_A:2 * D_A]
    v = p[:, 2 * D_A:3 * D_A]
    xwa = p[:, 3 * D_A:3 * D_A + LANES]
    xg = p[:, 3 * D_A + LANES:]
    w_log = -_softplus(-(w0_ref[...] + _dot1(jnp.tanh(xwa), ww2_ref[...]))) - 0.5
    logw = -jnp.exp(w_log)
    a = _sigmoid(a0_ref[...] + _dot1(xwa, wa2_ref[...]))
    g = _dot1(_sigmoid(xg), wg2_ref[...])
    yield 900
    seg = seg_ref[...]
    segt = segt_ref[...]

    def seg_sum(x, split=True):
        part = _dot_exact_rhs(x, seg) if split else _dot1(x, seg)
        return _dot_exact_rhs(part, segt)

    kk = k * kk_ref[...]
    kk = kk * lax.rsqrt(jnp.maximum(seg_sum(kk * kk, split=False), 1e-24))
    yield 600
    k = k * (1.0 + (a - 1.0) * ka_ref[...])
    kka = kk * a
    bonus = seg_sum(r * k * rk_ref[...], split=False) * v
    yield 600
    if t_valid < tb:
        logw = jnp.where(valid, logw, 0.0)
        r = jnp.where(valid, r, 0.0)
        k = jnp.where(valid, k, 0.0)
        v = jnp.where(valid, v, 0.0)
        kk = jnp.where(valid, kk, 0.0)
        kka = jnp.where(valid, kka, 0.0)

    ri = _iota((R, R), 0)
    ci = _iota((R, R), 1)
    same = (ri // chunk) == (ci // chunk)
    incl = same & (ci <= ri)
    strict = same & (ci < ri)
    lmat = jnp.concatenate([jnp.where(incl, 1.0, 0.0), jnp.where(same, 1.0, 0.0)], axis=0).astype(BF16)
    cums = _dot_exact_lhs(lmat, logw)
    cum = cums[:R]
    tot = cums[R:]
    yield 500
    e_prev = jnp.exp(cum - logw)
    e_neg = jnp.exp(-cum)
    e_pos = jnp.exp(cum)
    e_rem = jnp.exp(tot - cum)
    beta = kk * e_prev
    alpha = kka * e_neg
    kappa = k * e_neg
    rho = r * e_pos
    yield 600
    v_ref[...] = v
    kb_ref[...] = k * e_rem
    ab_ref[...] = -(kka * e_rem)
    wt_ref[...] = jnp.exp(tot)
    yield 400

    lane = _iota((1, LANES), 1)
    masks = (lane < HEAD, lane >= HEAD)
    eye = jnp.where(ri == ci, 1.0, 0.0)
    nsq = {16: 3, 8: 2}[chunk]
    heads = [(pr, m) for pr in range(N_PAIRS) for m in masks]
    psl = lambda pr: slice(pr * LANES, (pr + 1) * LANES)
    bms = [jnp.where(m, beta[:, psl(pr)], 0.0) for pr, m in heads]
    rms = [jnp.where(m, rho[:, psl(pr)], 0.0) for pr, m in heads]
    yield 300
    gms = []
    for (pr, m), bm, rm in zip(heads, bms, rms):
        gms.append(_dot1(jnp.concatenate([bm, rm], axis=0),
                         jnp.concatenate([alpha[:, psl(pr)], kappa[:, psl(pr)]], axis=0), _NT))
        yield 70
    l_bas = [jnp.where(strict, gm[:R, :R], 0.0) for gm in gms]
    m_ras = [jnp.where(incl, gm[R:, :R], 0.0) for gm in gms]
    xy1s = []
    for (pr, m), gm in zip(heads, gms):
        xy1s.append(_dot1(jnp.concatenate([jnp.where(strict, gm[:R, R:], 0.0),
                                           jnp.where(incl, gm[R:, R:], 0.0)], axis=0), v[:, psl(pr)]))
        yield 70
    pws = []
    for l in l_bas:
        pws.append(_dot1(l, l))
        yield 40

    def lru_steps():
        for pr in range(N_PAIRS):
            sl = psl(pr)
            bsl = slice(C_A + pr * LANES, C_A + (pr + 1) * LANES)
            gsl = slice(C_A + D_B + pr * LANES, C_A + D_B + (pr + 1) * LANES)
            xc = cb_ref[:, sl]
            for j in range(CONV_W):
                off = SUBLANES - (CONV_W - 1) + j
                xc = xc + ext[:, off:off + tb, bsl].reshape(R, LANES) * cw_ref[j:j + 1, sl]
            yield
            rg = _sigmoid(_dot1(xc, wga_ref[pr]) + bga_ref[:, sl])
            ig = _sigmoid(_dot1(xc, wgi_ref[pr]) + bgi_ref[:, sl])
            yield
            log_a = -C_LRU * rg * _softplus(-lam_ref[:, sl])
            av = jnp.exp(log_a)
            bv = jnp.sqrt(1.0 - jnp.exp(2.0 * log_a)) * (ig * xc)
            if t_valid < tb:
                av = jnp.where(valid, av, 1.0)
                bv = jnp.where(valid, bv, 0.0)
            yield
            d = 1
            while d < tb:
                take = rowin >= d
                a_sh = jnp.where(take, pltpu.roll(av, d, axis=0), 1.0)
                b_sh = jnp.where(take, pltpu.roll(bv, d, axis=0), 0.0)
                bv = av * b_sh + bv
                av = av * a_sh
                d *= 2
                yield
            h0 = jnp.broadcast_to(hst_ref[:, SUBLANES - 1:SUBLANES, sl], (ns, tb, LANES)).reshape(R, LANES)
            h3 = (av * h0 + bv).reshape(ns, tb, LANES)
            hst_ref[:, :, sl] = h3[:, tb - SUBLANES:, :]
            gb = cur[:, :, gsl].reshape(R, LANES)
            y_ref[:, :, D_A + pr * LANES:D_A + (pr + 1) * LANES] = h3 * _gelu(gb).reshape(ns, tb, LANES)
            yield

    lru = lru_steps()
    minvs = [eye - l for l in l_bas]
    for i in range(nsq):
        nxt = []
        for mi, pw in zip(minvs, pws):
            nxt.append(mi + _dot1(mi, pw))
            next(lru, None)
            yield 60
        minvs = nxt
        if i + 1 < nsq:
            nxt = []
            for pw in pws:
                nxt.append(_dot1(pw, pw))
                next(lru, None)
                yield 60
            pws = nxt
    for _ in lru:
        yield 40
    t1s = []
    for mi, xy1, bm in zip(minvs, xy1s, bms):
        t1s.append(_dot1(mi, jnp.concatenate([xy1[:R], bm], axis=1)))
        yield 50
    t2s = []
    for m_ra, t1 in zip(m_ras, t1s):
        t2s.append(_dot1(m_ra, t1))
        yield 50
    for pr in range(N_PAIRS):
        h0, h1 = 2 * pr, 2 * pr + 1
        sl = psl(pr)
        m1 = masks[1]
        bp_ref[:, sl] = t1s[h0][:, LANES:] + t1s[h1][:, LANES:]
        rp_ref[:, sl] = (rms[h0] - t2s[h0][:, LANES:]) + (rms[h1] - t2s[h1][:, LANES:])
        x2_ref[:, sl] = jnp.where(m1, t1s[h1][:, :LANES], t1s[h0][:, :LANES])
        y2_ref[:, sl] = jnp.where(m1, xy1s[h1][R:] - t2s[h1][:, :LANES], xy1s[h0][R:] - t2s[h0][:, :LANES])

    bi = _iota((LANES, LANES), 0)
    bj = _iota((LANES, LANES), 1)
    bd = (bi < HEAD) == (bj < HEAD)
    nck = tb // chunk
    sites = [(s, c, pr) for s in range(ns) for c in range(nck) for pr in range(N_PAIRS)]
    rows_of = lambda s, c: slice(s * tb + c * chunk, s * tb + (c + 1) * chunk)
    thetas = {}
    psis = {}
    for s, c, pr in sites:
        rs, sl = rows_of(s, c), psl(pr)
        thetas[s, c, pr] = jnp.where(bd, _state_dot(bp_ref[rs, sl], ab_ref[rs, sl], _TN), 0.0)
        vx = jnp.concatenate([v_ref[rs, sl], x2_ref[rs, sl]], axis=0)
        ka = jnp.concatenate([kb_ref[rs, sl], ab_ref[rs, sl]], axis=0)
        psis[s, c, pr] = jnp.where(bd, _state_dot(vx, ka, _TN), 0.0)
        yield 60
    for s in range(ns):
        sps = [st_ref[s, pr] if carry_state else pair_tile(s0_ref, s, pr) for pr in range(N_PAIRS)]
        for c in range(nck):
            rs = rows_of(s, c)
            for pr in range(N_PAIRS):
                yacc_ref[rs, psl(pr)] = _state_dot(rp_ref[rs, psl(pr)], sps[pr], _NT) + y2_ref[rs, psl(pr)]
            yield 100
            nxt = []
            for pr in range(N_PAIRS):
                nxt.append(sps[pr] * wt_ref[rs.start:rs.start + 1, psl(pr)]
                           + (_state_dot(sps[pr], thetas[s, c, pr]) + psis[s, c, pr]))
                yield 50
            sps = nxt
        for pr in range(N_PAIRS):
            if carry_state:
                st_ref[s, pr] = sps[pr]
            else:
                sout_ref[s, 2 * pr] = sps[pr][:HEAD, :HEAD]
                sout_ref[s, 2 * pr + 1] = sps[pr][HEAD:, HEAD:]
        yield 50

    y = yacc_ref[...]
    mean = seg_sum(y) * (1.0 / HEAD)
    yc = y - mean
    yield 400
    var = seg_sum(yc * yc) * (1.0 / HEAD)
    yn = yc * lax.rsqrt(var + GN_EPS) * lnw_ref[...] + lnb_ref[...]
    ya = (yn + bonus) * g
    y_ref[:, :, :D_A] = ya.reshape(ns, tb, D_A)
    yield 600


def _mixer_kernel(proj_ref, pprev_ref, conv0_ref, s0_ref, lru0_ref, *refs, ns, tb, chunk, t_valid, carry_state, streams):
    n_par = 20
    params = refs[:n_par]
    y_ref, sout_ref, lruo_ref, convo_ref, prev_ref, hst_ref, st_ref = refs[n_par:n_par + 7]
    slabs = refs[n_par + 7:]
    t = pl.program_id(1)

    @pl.when(t == 0)
    def _():
        prev_ref[...] = jnp.zeros_like(prev_ref)
        prev_ref[:, SUBLANES - 1:, :C_A] = pprev_ref[...]
        prev_ref[:, SUBLANES - (CONV_W - 1):, C_A:] = conv0_ref[...]
        hst_ref[...] = jnp.broadcast_to(lru0_ref[...], hst_ref.shape)
        if carry_state:
            for q in range(streams):
                for pr in range(N_PAIRS):
                    st_ref[q, pr] = _pair_tile(s0_ref, q, pr)

    def make(q):
        sq = pl.ds(q * ns, ns)
        return _mixer_stream(
            proj_ref.at[sq], s0_ref.at[sq], *params, y_ref.at[sq], sout_ref.at[sq],
            prev_ref.at[sq], hst_ref.at[sq], st_ref.at[pl.ds(q, 1)], *[r.at[q] for r in slabs],
            ns=ns, tb=tb, chunk=chunk, t_valid=t_valid, carry_state=carry_state)

    gens = [make(q) for q in range(streams)]
    clock = [0.0] * streams
    live = list(range(streams))
    while live:
        q = min(live, key=lambda i: clock[i])
        cost = next(gens[q], None)
        if cost is None:
            live.remove(q)
        else:
            clock[q] += cost

    @pl.when(t == pl.num_programs(1) - 1)
    def _():
        lruo_ref[...] = hst_ref[...]
        convo_ref[...] = proj_ref[:, tb - SUBLANES:, C_A:C_A + D_B]
        if carry_state:
            for q in range(streams):
                for pr in range(N_PAIRS):
                    tile = st_ref[q, pr]
                    sout_ref[q, 2 * pr] = tile[:HEAD, :HEAD]
                    sout_ref[q, 2 * pr + 1] = tile[HEAD:, HEAD:]


def _mixer(proj, p_prev, conv0, s0, lru0, prm, ns, tb, chunk, t_valid, streams=1):
    S, T = s0.shape[0], proj.shape[1]
    R = ns * tb
    assert R == ROWS
    carry_state = T // tb > 1
    assert ns == 1 or not carry_state
    kern = functools.partial(_mixer_kernel, ns=ns, tb=tb, chunk=chunk, t_valid=t_valid, carry_state=carry_state,
                             streams=streams)
    nb = ns * streams

    def full(a):
        nd = a.ndim
        return pl.BlockSpec(a.shape, lambda s, t, nd=nd: (0,) * nd)

    params = [prm[n] for n in ('mu', 'w0', 'ww2', 'a0', 'wa2', 'wg2', 'k_k', 'k_a', 'r_k', 'lnx_w', 'lnx_b',
                               'seg', 'segt', 'conv_w', 'conv_b', 'wga', 'bga', 'wgi', 'bgi', 'lam')]
    slab = lambda w: pltpu.VMEM((streams, R, w), F32)
    return pl.pallas_call(
        kern,
        out_shape=(jax.ShapeDtypeStruct((S, T, D_A + D_B), F32),
                   jax.ShapeDtypeStruct((S, N_HEADS, HEAD, HEAD), F32),
                   jax.ShapeDtypeStruct((S, SUBLANES, D_B), F32),
                   jax.ShapeDtypeStruct((S, SUBLANES, D_B), F32)),
        grid=(S // nb, T // tb),
        in_specs=[pl.BlockSpec((nb, tb, C_TOT), lambda s, t: (s, t, 0)),
                  pl.BlockSpec((nb, 1, C_A), lambda s, t: (s, 0, 0)),
                  pl.BlockSpec((nb, CONV_W - 1, D_B), lambda s, t: (s, 0, 0)),
                  pl.BlockSpec((nb, N_HEADS, HEAD, HEAD), lambda s, t: (s, 0, 0, 0)),
                  pl.BlockSpec((nb, 1, D_B), lambda s, t: (s, 0, 0))]
                 + [full(a) for a in params],
        out_specs=(pl.BlockSpec((nb, tb, D_A + D_B), lambda s, t: (s, t, 0)),
                   pl.BlockSpec((nb, N_HEADS, HEAD, HEAD), lambda s, t: (s, 0, 0, 0),
                                pipeline_mode=pl.Buffered(1)),
                   pl.BlockSpec((nb, SUBLANES, D_B), lambda s, t: (s, 0, 0)),
                   pl.BlockSpec((nb, SUBLANES, D_B), lambda s, t: (s, 0, 0))),
        scratch_shapes=[pltpu.VMEM((nb, SUBLANES, C_A + D_B), F32),
                        pltpu.VMEM((nb, SUBLANES, D_B), F32),
                        pltpu.VMEM((streams, N_PAIRS, LANES, LANES), F32)]
                       + [slab(D_A) for _ in range(9)],
        compiler_params=_cparams(("arbitrary", "arbitrary")),
        name="mixer",
    )(proj, p_prev, conv0, s0, lru0, *params)


def _layernorm(x, w, b):
    mean = jnp.mean(x, axis=-1, keepdims=True)
    xc = x - mean
    var = jnp.mean(xc * xc, axis=-1, keepdims=True)
    return xc * lax.rsqrt(var + LN_EPS) * w + b


def _outproj_kernel(y_ref, x_ref, g1_ref, sc2_ref, sh2_ref, w_ref, lnw_ref, lnb_ref, x1_ref, h2_ref, *, alpha):
    ns, tb, _ = y_ref.shape
    mix = _dot1(y_ref[...].reshape(ns * tb, D_MODEL), w_ref[...]).reshape(ns, tb, D_MODEL)
    x1 = _layernorm(alpha * x_ref[...] + (1.0 + g1_ref[...]) * mix, lnw_ref[...], lnb_ref[...])
    x1_ref[...] = x1
    h2 = x1 * (1.0 + sc2_ref[...]) + sh2_ref[...]
    h2_ref[...] = h2.reshape(ns * tb, D_MODEL).astype(BF16)


def _outproj(y, x, mod, w_bf16, ln_w, ln_b, ns, tb, alpha):
    S, T, _ = x.shape
    nt = T // tb
    blk = pl.BlockSpec((ns, tb, D_MODEL), lambda s, t: (s, t, 0))
    modspec = lambda i: pl.BlockSpec((ns, 1, D_MODEL), lambda s, t, i=i: (s, 0, i))
    vec = pl.BlockSpec((1, D_MODEL), lambda s, t: (0, 0))
    return pl.pallas_call(
        functools.partial(_outproj_kernel, alpha=alpha),
        out_shape=(jax.ShapeDtypeStruct((S, T, D_MODEL), F32), jax.ShapeDtypeStruct((S * T, D_MODEL), BF16)),
        grid=(S // ns, nt),
        in_specs=[blk, blk, modspec(2), modspec(4), modspec(3),
                  pl.BlockSpec((D_MODEL, D_MODEL), lambda s, t: (0, 0)), vec, vec],
        out_specs=(blk, pl.BlockSpec((ns * tb, D_MODEL), lambda s, t: (s * nt + t, 0))),
        compiler_params=_cparams(("arbitrary", "arbitrary")),
        name="outproj",
    )(y, x, mod, mod, mod, w_bf16, ln_w.reshape(1, -1), ln_b.reshape(1, -1))


def _topk_rows(s, ridx, n_rows):
    out_i = _iota((TOPK, s.shape[1]), 0)
    vals = jnp.zeros((TOPK, s.shape[1]), F32)
    idxs = jnp.zeros((TOPK, s.shape[1]), F32)
    for it in range(TOPK):
        m = jnp.max(s, axis=0, keepdims=True)
        idx = jnp.min(jnp.where(s == m, ridx, float(n_rows)), axis=0, keepdims=True)
        vals = jnp.where(out_i == it, m, vals)
        idxs = jnp.where(out_i == it, idx, idxs)
        s = jnp.where(ridx == idx, -jnp.inf, s)
    return vals, idxs


def _route_kernel(h_ref, wq_ref, sk_ref, g_ref, gate_ref, e1_ref, e2_ref):
    R = ROWS
    q = _dot1(h_ref[...], wq_ref[...])
    sk0 = sk_ref[0]
    sk1 = sk_ref[1]
    gates, e1s, e2s = [], [], []
    key_rows = _iota((N_KEYS, R), 0).astype(F32)
    r16 = _iota((TOPK, R), 0).astype(F32)
    r8 = _iota((SUBLANES, R), 0).astype(F32)
    ea_rank = jnp.where(r8 < 3, 2.0, jnp.where(r8 < 5, 3.0, 4.0))
    eb_rank = jnp.where((r8 == 0) | (r8 == 3) | (r8 == 5), 2.0, jnp.where((r8 == 1) | (r8 == 4), 3.0, 4.0))
    flat = jnp.concatenate([r16, TOPK + r8, TOPK * r16, TOPK * r8 + 1.0, TOPK * ea_rank + eb_rank], axis=0)
    neg = -jnp.inf
    for hd in range(P_HEADS):
        base = hd * 2 * D_QH
        s1 = _dot1(sk0, q[:, base:base + D_QH], _NT)
        s2 = _dot1(sk1, q[:, base + D_QH:base + 2 * D_QH], _NT)
        v1, i1 = _topk_rows(s1, key_rows, N_KEYS)
        v2, i2 = _topk_rows(s2, key_rows, N_KEYS)
        ea = jnp.where(r8 < 3, v1[2:3], jnp.where(r8 < 5, v1[3:4], v1[4:5]))
        eb = jnp.where(eb_rank == 2.0, v2[2:3], jnp.where(eb_rank == 3.0, v2[3:4], v2[4:5]))
        cand = jnp.concatenate([
            v1[0:1] + v2,
            v1[1:2] + v2[:SUBLANES],
            jnp.where(r16 >= 2, v1 + v2[0:1], neg),
            jnp.where(r8 >= 2, v1[:SUBLANES] + v2[1:2], neg),
            jnp.where(r8 < 6, ea + eb, neg),
        ], axis=0)
        sv, ci = _topk_rows(cand, flat, TOPK * TOPK)
        ca = jnp.floor(ci * (1.0 / TOPK))
        cb = ci - TOPK * ca
        e1 = jnp.zeros((TOPK, R), F32)
        e2 = jnp.zeros((TOPK, R), F32)
        for a in range(TOPK):
            e1 = jnp.where(ca == float(a), i1[a:a + 1, :], e1)
            e2 = jnp.where(cb == float(a), i2[a:a + 1, :], e2)
        ex = jnp.exp(sv - jnp.max(sv, axis=0, keepdims=True))
        gates.append(ex / jnp.sum(ex, axis=0, keepdims=True))
        e1s.append(e1)
        e2s.append(e2)
    gate_ref[...] = jnp.concatenate(gates, axis=0).T
    e1_ref[...] = jnp.concatenate(e1s, axis=0).T
    e2_ref[...] = jnp.concatenate(e2s, axis=0).T
    key_i = _iota((N_KEYS, P_HEADS * TOPK), 0).astype(F32)

    def per_octet(o, carry):
        base = pl.multiple_of(o * SUBLANES, SUBLANES)
        tiles = []
        for j in range(SUBLANES):
            sel1 = key_i == e1_ref[pl.ds(base + j, 1), :]
            sel2 = key_i == e2_ref[pl.ds(base + j, 1), :]
            m1 = jnp.where(sel1, gate_ref[pl.ds(base + j, 1), :], 0.0)
            m2 = jnp.where(sel2, 1.0, 0.0)
            tiles.append(_dot1(m1, m2, _NT))
        by_key = jnp.swapaxes(jnp.stack(tiles, axis=0), 0, 1)
        for i1v in range(N_KEYS):
            g_ref[pl.ds(base, SUBLANES), i1v * N_KEYS:(i1v + 1) * N_KEYS] = by_key[i1v]
        return carry

    lax.fori_loop(0, R // SUBLANES, per_octet, 0, unroll=8)


def _route(h2, wq_bf16, sub_keys):
    n = h2.shape[0]
    assert n % ROWS == 0
    return pl.pallas_call(
        _route_kernel,
        out_shape=jax.ShapeDtypeStruct((n, N_EXPERTS), F32),
        grid=(n // ROWS,),
        in_specs=[pl.BlockSpec((ROWS, D_MODEL), lambda i: (i, 0)),
                  pl.BlockSpec((D_MODEL, D_MODEL), lambda i: (0, 0)),
                  pl.BlockSpec((2, N_KEYS, D_QH), lambda i: (0, 0, 0))],
        out_specs=pl.BlockSpec((ROWS, N_EXPERTS), lambda i: (i, 0)),
        scratch_shapes=[pltpu.VMEM((ROWS, P_HEADS * TOPK), F32) for _ in range(3)],
        compiler_params=_cparams(("arbitrary",)),
        name="route",
    )(h2, wq_bf16, sub_keys)


EXPERT_CHUNK = 512
EXPERT_ROWS_MAX = 1152


EXPERT_SPLIT = 4


def _experts_kernel(h_ref, g_ref, u_ref, v_ref, o_ref):
    @pl.when(pl.program_id(1) == 0)
    def _():
        o_ref[...] = jnp.zeros_like(o_ref)

    u = u_ref[...].astype(BF16)
    v = v_ref[...].astype(BF16)
    rb = h_ref.shape[0] // EXPERT_SPLIT
    rows = [pl.ds(b * rb, rb) for b in range(EXPERT_SPLIT)]
    act = _dg(h_ref[rows[0], :], u, _NT)
    for b in range(EXPERT_SPLIT):
        nxt = _dg(h_ref[rows[b + 1], :], u, _NT) if b + 1 < EXPERT_SPLIT else None
        w = (g_ref[rows[b], :] * _gelu(act)).astype(BF16)
        o_ref[rows[b], :] += _dg(w, v, _NN)
        act = nxt


def _token_block(n, cap):
    for tb in range(min(n, cap), 0, -1):
        if n % tb == 0 and tb % (2 * SUBLANES * EXPERT_SPLIT) == 0:
            return tb
    raise ValueError(n)


def _experts(h2, gmat, peer_u, peer_v):
    n = h2.shape[0]
    tb = _token_block(n, EXPERT_ROWS_MAX)
    ec = EXPERT_CHUNK
    return pl.pallas_call(
        _experts_kernel,
        out_shape=jax.ShapeDtypeStruct((n, D_MODEL), F32),
        grid=(n // tb, N_EXPERTS // ec),
        in_specs=[pl.BlockSpec((tb, D_MODEL), lambda i, e: (i, 0)),
                  pl.BlockSpec((tb, ec), lambda i, e: (i, e)),
                  pl.BlockSpec((ec, D_MODEL), lambda i, e: (e, 0)),
                  pl.BlockSpec((ec, D_MODEL), lambda i, e: (e, 0))],
        out_specs=pl.BlockSpec((tb, D_MODEL), lambda i, e: (i, 0)),
        compiler_params=_cparams(("arbitrary", "arbitrary")),
        name="experts",
    )(h2, gmat, peer_u, peer_v)


def _ln2_kernel(x1_ref, ff_ref, g2_ref, lnw_ref, lnb_ref, o_ref, *, alpha):
    ff = ff_ref[...].reshape(x1_ref.shape)
    o_ref[...] = _layernorm(alpha * x1_ref[...] + (1.0 + g2_ref[...]) * ff, lnw_ref[...], lnb_ref[...])


def _ln2(x1, ff, mod, ln_w, ln_b, ns, tb, alpha):
    S, T, _ = x1.shape
    nt = T // tb
    blk = pl.BlockSpec((ns, tb, D_MODEL), lambda s, t: (s, t, 0))
    vec = pl.BlockSpec((1, D_MODEL), lambda s, t: (0, 0))
    if ff.ndim == 2:
        assert ns == 1
        ffspec = pl.BlockSpec((tb, D_MODEL), lambda s, t: (s * nt + t, 0))
    else:
        ffspec = blk
    return pl.pallas_call(
        functools.partial(_ln2_kernel, alpha=alpha),
        out_shape=jax.ShapeDtypeStruct((S, T, D_MODEL), F32),
        grid=(S // ns, nt),
        in_specs=[blk, ffspec, pl.BlockSpec((ns, 1, D_MODEL), lambda s, t: (s, 0, 5)), vec, vec],
        out_specs=blk,
        compiler_params=_cparams(("arbitrary", "arbitrary")),
        name="ln2",
    )(x1, ff, mod, ln_w.reshape(1, -1), ln_b.reshape(1, -1))


def _pair_blockdiag(w):
    n = w.shape[0] // 2
    w = w.reshape(n, 2, HEAD, HEAD)
    z = jnp.zeros((n, HEAD, HEAD), w.dtype)
    top = jnp.concatenate([w[:, 0], z], axis=2)
    bot = jnp.concatenate([z, w[:, 1]], axis=2)
    return jnp.concatenate([top, bot], axis=1)


def _mix_group(x, mod, p_prev, wkv0, lru0, conv0, prm, *, ns_mix, tb_mix, chunk, t_valid, ns_big, tb_big, alpha,
               streams=1, shift_rows=None):
    S, T, _ = x.shape
    if shift_rows is None:
        proj, h_last = _inproj(x, mod, prm['w_in'], ns_big, tb_big)
    else:
        extra = S // T
        x_ext = jnp.concatenate([x, shift_rows.reshape(extra, T, D_MODEL)], axis=0)
        mod_ext = jnp.concatenate([mod, jnp.zeros((extra, 1, 6 * D_MODEL), F32)], axis=0)
        ns_ext = max(d for d in range(1, ns_big + extra // 2 + 1) if (S + extra) % d == 0)
        proj, h_last = _inproj(x_ext, mod_ext, prm['w_in'], ns_ext, tb_big)
        p_prev = proj[S:].reshape(S, C_TOT)[:, :C_A]
        h_last = h_last[:S]
    y, wkv, lru_o, conv_o = _mixer(proj, p_prev[:, None, :], conv0, wkv0, lru0[:, None, :], prm,
                                   ns_mix, tb_mix, chunk, t_valid, streams)
    x1, h2 = _outproj(y, x, mod, prm['w_out'], prm['ln1_w'], prm['ln1_b'], ns_big, tb_big, alpha)
    tv = (t_valid - 1) % SUBLANES
    shift = h_last[:, tv]
    lru = lru_o[:, SUBLANES - 1]
    conv = conv_o[:, tv - (CONV_W - 2):tv + 1]
    return x1, h2, (shift, wkv, lru, conv)


def kernel(x_prompt, x_sample, c_prompt, c_sample, state_shift, state_wkv, state_lru, state_conv, w_ada, b_ada, w_in, mu_shift, w0, w_w2, a0, w_a2, w_g2, k_k, k_a, r_k, lnx_w, lnx_b, conv_w, conv_b, w_gate_a, b_gate_a, w_gate_i, b_gate_i, lru_lambda, w_out, ln1_w, ln1_b, w_q, sub_keys, peer_u, peer_v, ln2_w, ln2_b):
    depth = w_ada.shape[0]
    alpha = (2 * depth) ** 0.25
    bp, tp, _ = x_prompt.shape
    bs, ts, _ = x_sample.shape
    yp = x_prompt
    ys = jnp.pad(x_sample, ((0, 0), (0, SAMPLE_TPAD - ts), (0, 0)))
    head_of = jnp.arange(D_A) // HEAD
    seg = (head_of[:, None] == jnp.arange(LANES)[None, :]).astype(BF16)
    row = lambda a: a.reshape(1, -1)
    outs = [[] for _ in range(8)]
    for l in range(depth):
        zpad = jnp.zeros((LANES - R_W, D_A), F32)
        prm = {
            'w_in': w_in[l].astype(BF16), 'w_out': w_out[l].astype(BF16), 'w_q': w_q[l].astype(BF16),
            'sub_keys': sub_keys[l],
            'mu': row(mu_shift[l]), 'w0': row(w0[l]),
            'ww2': jnp.concatenate([w_w2[l], zpad], axis=0).astype(BF16), 'a0': row(a0[l]),
            'wa2': jnp.concatenate([zpad, w_a2[l]], axis=0).astype(BF16), 'wg2': w_g2[l].astype(BF16),
            'k_k': row(k_k[l]), 'k_a': row(k_a[l]), 'r_k': row(r_k[l]),
            'lnx_w': row(lnx_w[l]), 'lnx_b': row(lnx_b[l]), 'seg': seg, 'segt': seg.T,
            'conv_w': conv_w[l], 'conv_b': row(conv_b[l]),
            'wga': _pair_blockdiag(w_gate_a[l]).astype(BF16), 'bga': row(b_gate_a[l]),
            'wgi': _pair_blockdiag(w_gate_i[l]).astype(BF16), 'bgi': row(b_gate_i[l]),
            'lam': row(lru_lambda[l]),
            'ln1_w': ln1_w[l], 'ln1_b': ln1_b[l], 'ln2_w': ln2_w[l], 'ln2_b': ln2_b[l],
        }
        c_all = jnp.concatenate([c_prompt, c_sample], axis=0)
        mod = _ada(c_all, w_ada[l], b_ada[l])
        mod_p = mod[:bp, None, :]
        mod_s = mod[bp:, None, :]
        x1p, h2p, st_p = _mix_group(
            yp, mod_p, jnp.zeros((bp, C_A), F32), jnp.zeros((bp, N_HEADS, HEAD, HEAD), F32),
            jnp.zeros((bp, D_B), F32), jnp.zeros((bp, CONV_W - 1, D_B), F32), prm,
            ns_mix=1, tb_mix=ROWS, chunk=16, t_valid=ROWS, ns_big=1, tb_big=min(512, tp), alpha=alpha,
            streams=2 if bp % 2 == 0 else 1)
        x1s, h2s, st_s = _mix_group(
            ys, mod_s, None, state_wkv[l], state_lru[l], state_conv[l], prm,
            ns_mix=ROWS // SAMPLE_TPAD, tb_mix=SAMPLE_TPAD, chunk=SAMPLE_TPAD, t_valid=ts,
            ns_big=min(64, bs), tb_big=SAMPLE_TPAD, alpha=alpha, shift_rows=state_shift[l])
        h2s = h2s.reshape(bs, SAMPLE_TPAD, D_MODEL)[:, :ts].reshape(bs * ts, D_MODEL)
        h2 = jnp.concatenate([h2p, h2s], axis=0)
        gmat = _route(h2, prm['w_q'], prm['sub_keys'])
        ff = _experts(h2, gmat, peer_u[l], peer_v[l])
        yp = _ln2(x1p, ff, mod_p, ln2_w[l], ln2_b[l], 1, min(512, tp), alpha)
        ys_real = _ln2(x1s[:, :ts], ff[bp * tp:].reshape(bs, ts, D_MODEL), mod_s, ln2_w[l], ln2_b[l],
                       min(64, bs), ts, alpha)
        ys = jnp.pad(ys_real, ((0, 0), (0, SAMPLE_TPAD - ts), (0, 0)))
        for i, a in enumerate(st_p + st_s):
            outs[i].append(a)
    return (yp, ys[:, :ts]) + tuple(jnp.stack(o) for o in outs)
```

```python
import functools

import jax
import jax.numpy as jnp
from jax import lax
from jax.experimental import pallas as pl
from jax.experimental.pallas import tpu as pltpu

F32 = jnp.float32
BF16 = jnp.bfloat16

D_MODEL = 2048
D_A = 1024
D_B = 1024
HEAD = 64
N_HEADS = D_A // HEAD
N_PAIRS = N_HEADS // 2
R_W = 64
R_A = 64
R_G = 128
C_A = 3 * D_A + R_W + R_A + R_G
C_TOT = C_A + 2 * D_B
CONV_W = 4
C_LRU = 8.0
P_HEADS = 8
N_KEYS = 128
N_EXPERTS = N_KEYS * N_KEYS
TOPK = 16
D_QH = 128
LN_EPS = 1e-5
GN_EPS = 64e-5

LANES = 128
SUBLANES = 8
ROWS = 128
SAMPLE_TPAD = 8
VMEM_LIMIT = 58 * 1024 * 1024


def _cparams(sem):
    return pltpu.CompilerParams(dimension_semantics=sem, vmem_limit_bytes=VMEM_LIMIT)


def _split2(x):
    hi = x.astype(BF16)
    lo = (x - hi.astype(F32)).astype(BF16)
    return hi, lo


_NN = (((1,), (0,)), ((), ()))
_NT = (((1,), (1,)), ((), ()))
_TN = (((0,), (0,)), ((), ()))


def _dg(a, b, dims):
    return lax.dot_general(a, b, dims, preferred_element_type=F32)


def _dot1(a, b, dims=_NN):
    return _dg(a.astype(BF16), b.astype(BF16), dims)


def _dot3(a, b, dims=_NN):
    ah, al = _split2(a)
    bh, bl = _split2(b)
    return _dg(ah, bh, dims) + (_dg(ah, bl, dims) + _dg(al, bh, dims))


_state_dot = _dot1


def _dot_exact_rhs(a, b_exact, dims=_NN):
    a1, a2 = _split2(a)
    return _dg(a1, b_exact, dims) + _dg(a2, b_exact, dims)


def _dot_exact_lhs(a_exact, b, dims=_NN):
    b1, b2 = _split2(b)
    return _dg(a_exact, b1, dims) + _dg(a_exact, b2, dims)


def _sigmoid(x):
    return 1.0 / (1.0 + jnp.exp(-x))


def _softplus(x):
    return jnp.maximum(x, 0.0) + jnp.log(1.0 + jnp.exp(-jnp.abs(x)))


def _gelu(x):
    return 0.5 * x * (1.0 + jnp.tanh(0.7978845608028654 * (x + 0.044715 * (x * x * x))))


def _iota(shape, dim):
    return lax.broadcasted_iota(jnp.int32, shape, dim)


def _ada_kernel(c_ref, w_ref, b_ref, o_ref):
    c = c_ref[...]
    s = c * _sigmoid(c)
    o_ref[...] = _dot3(s, w_ref[...]) + b_ref[...]


def _ada(c, w_ada, b_ada):
    n = c.shape[0]
    tn = 1024
    return pl.pallas_call(
        _ada_kernel,
        out_shape=jax.ShapeDtypeStruct((n, 6 * D_MODEL), F32),
        grid=(6 * D_MODEL // tn,),
        in_specs=[pl.BlockSpec((n, D_MODEL), lambda j: (0, 0)),
                  pl.BlockSpec((D_MODEL, tn), lambda j: (0, j)),
                  pl.BlockSpec((1, tn), lambda j: (0, j))],
        out_specs=pl.BlockSpec((n, tn), lambda j: (0, j)),
        compiler_params=_cparams(("arbitrary",)),
        name="ada",
    )(c, w_ada, b_ada.reshape(1, -1))


def _inproj_kernel(x_ref, sc_ref, sh_ref, w_ref, o_ref, h_ref):
    h = x_ref[...] * (1.0 + sc_ref[...]) + sh_ref[...]
    ns, tb, _ = h.shape
    h_ref[...] = h[:, tb - SUBLANES:, :]
    p = _dot1(h.reshape(ns * tb, D_MODEL), w_ref[...])
    o_ref[...] = p.reshape(ns, tb, p.shape[-1])


def _inproj(x, mod, w_bf16, ns, tb):
    S, T, _ = x.shape
    ncol = w_bf16.shape[1]
    tn = ncol // 3
    proj, h_last = pl.pallas_call(
        _inproj_kernel,
        out_shape=(jax.ShapeDtypeStruct((S, T, ncol), F32),
                   jax.ShapeDtypeStruct((3, S, SUBLANES, D_MODEL), F32)),
        grid=(3, S // ns, T // tb),
        in_specs=[pl.BlockSpec((ns, tb, D_MODEL), lambda j, s, t: (s, t, 0)),
                  pl.BlockSpec((ns, 1, D_MODEL), lambda j, s, t: (s, 0, 1)),
                  pl.BlockSpec((ns, 1, D_MODEL), lambda j, s, t: (s, 0, 0)),
                  pl.BlockSpec((D_MODEL, tn), lambda j, s, t: (0, j))],
        out_specs=(pl.BlockSpec((ns, tb, tn), lambda j, s, t: (s, t, j)),
                   pl.BlockSpec((None, ns, SUBLANES, D_MODEL), lambda j, s, t: (j, s, 0, 0))),
        compiler_params=_cparams(("arbitrary", "arbitrary", "arbitrary")),
        name="inproj",
    )(x, mod, mod, w_bf16)
    return proj, h_last[0]


def _pair_tile(ref, s, pr):
    zero_blk = jnp.zeros((HEAD, HEAD), F32)
    top = jnp.concatenate([ref[s, 2 * pr], zero_blk], axis=1)
    bot = jnp.concatenate([zero_blk, ref[s, 2 * pr + 1]], axis=1)
    return jnp.concatenate([top, bot], axis=0)


def _mixer_stream(proj_ref, s0_ref,
                  mu_ref, w0_ref, ww2_ref, a0_ref, wa2_ref, wg2_ref, kk_ref, ka_ref, rk_ref,
                  lnw_ref, lnb_ref, seg_ref, segt_ref,
                  cw_ref, cb_ref, wga_ref, bga_ref, wgi_ref, bgi_ref, lam_ref,
                  y_ref, sout_ref,
                  prev_ref, hst_ref, st_ref,
                  bp_ref, rp_ref, x2_ref, y2_ref, v_ref, kb_ref, ab_ref, wt_ref, yacc_ref,
                  *, ns, tb, chunk, t_valid, carry_state):
    R = ns * tb
    pair_tile = _pair_tile

    cur = proj_ref[...]
    ext = jnp.concatenate([prev_ref[...], cur[:, :, :C_A + D_B]], axis=1)
    prev_ref[...] = cur[:, tb - SUBLANES:, :C_A + D_B]

    row = _iota((R, 1), 0)
    rowin = row % tb
    valid = rowin < t_valid

    pa = cur[:, :, :C_A].reshape(R, C_A)
    shifted = ext[:, SUBLANES - 1:SUBLANES - 1 + tb, :C_A].reshape(R, C_A)
    p = pa + (shifted - pa) * mu_ref[...]
    r = p[:, :D_A]
    k = p[:, D_A:2 * D_A]
    v = p[:, 2 * D_A:3 * D_A]
    xwa = p[:, 3 * D_A:3 * D_A + LANES]
    xg = p[:, 3 * D_A + LANES:]
    w_log = -_softplus(-(w0_ref[...] + _dot1(jnp.tanh(xwa), ww2_ref[...]))) - 0.5
    logw = -jnp.exp(w_log)
    a = _sigmoid(a0_ref[...] + _dot1(xwa, wa2_ref[...]))
    g = _dot1(_sigmoid(xg), wg2_ref[...])
    yield 900
    seg = seg_ref[...]
    segt = segt_ref[...]

    def seg_sum(x, split=True):
        part = _dot_exact_rhs(x, seg) if split else _dot1(x, seg)
        return _dot_exact_rhs(part, segt)

    kk = k * kk_ref[...]
    kk = kk * lax.rsqrt(jnp.maximum(seg_sum(kk * kk, split=False), 1e-24))
    yield 600
    k = k * (1.0 + (a - 1.0) * ka_ref[...])
    kka = kk * a
    bonus = seg_sum(r * k * rk_ref[...], split=False) * v
    yield 600
    if t_valid < tb:
        logw = jnp.where(valid, logw, 0.0)
        r = jnp.where(valid, r, 0.0)
        k = jnp.where(valid, k, 0.0)
        v = jnp.where(valid, v, 0.0)
        kk = jnp.where(valid, kk, 0.0)
        kka = jnp.where(valid, kka, 0.0)

    ri = _iota((R, R), 0)
    ci = _iota((R, R), 1)
    same = (ri // chunk) == (ci // chunk)
    incl = same & (ci <= ri)
    strict = same & (ci < ri)
    lmat = jnp.concatenate([jnp.where(incl, 1.0, 0.0), jnp.where(same, 1.0, 0.0)], axis=0).astype(BF16)
    cums = _dot_exact_lhs(lmat, logw)
    cum = cums[:R]
    tot = cums[R:]
    yield 500
    e_prev = jnp.exp(cum - logw)
    e_neg = jnp.exp(-cum)
    e_pos = jnp.exp(cum)
    e_rem = jnp.exp(tot - cum)
    beta = kk * e_prev
    alpha = kka * e_neg
    kappa = k * e_neg
    rho = r * e_pos
    yield 600
    v_ref[...] = v
    kb_ref[...] = k * e_rem
    ab_ref[...] = -(kka * e_rem)
    wt_ref[...] = jnp.exp(tot)
    yield 400

    lane = _iota((1, LANES), 1)
    masks = (lane < HEAD, lane >= HEAD)
    eye = jnp.where(ri == ci, 1.0, 0.0)
    nsq = {32: 4, 16: 3, 8: 2}[chunk]
    heads = [(pr, m) for pr in range(N_PAIRS) for m in masks]
    psl = lambda pr: slice(pr * LANES, (pr + 1) * LANES)
    bms = [jnp.where(m, beta[:, psl(pr)], 0.0) for pr, m in heads]
    rms = [jnp.where(m, rho[:, psl(pr)], 0.0) for pr, m in heads]
    yield 300
    gms = []
    for (pr, m), bm, rm in zip(heads, bms, rms):
        gms.append(_dot1(jnp.concatenate([bm, rm], axis=0),
                         jnp.concatenate([alpha[:, psl(pr)], kappa[:, psl(pr)]], axis=0), _NT))
        yield 70
    l_bas = [jnp.where(strict, gm[:R, :R], 0.0) for gm in gms]
    m_ras = [jnp.where(incl, gm[R:, :R], 0.0) for gm in gms]
    xy1s = []
    for (pr, m), gm in zip(heads, gms):
        xy1s.append(_dot1(jnp.concatenate([jnp.where(strict, gm[:R, R:], 0.0),
                                           jnp.where(incl, gm[R:, R:], 0.0)], axis=0), v[:, psl(pr)]))
        yield 70
    pws = []
    for l in l_bas:
        pws.append(_dot1(l, l))
        yield 40

    def lru_steps():
        for pr in range(N_PAIRS):
            sl = psl(pr)
            bsl = slice(C_A + pr * LANES, C_A + (pr + 1) * LANES)
            gsl = slice(C_A + D_B + pr * LANES, C_A + D_B + (pr + 1) * LANES)
            xc = cb_ref[:, sl]
            for j in range(CONV_W):
                off = SUBLANES - (CONV_W - 1) + j
                xc = xc + ext[:, off:off + tb, bsl].reshape(R, LANES) * cw_ref[j:j + 1, sl]
            yield
            rg = _sigmoid(_dot1(xc, wga_ref[pr]) + bga_ref[:, sl])
            ig = _sigmoid(_dot1(xc, wgi_ref[pr]) + bgi_ref[:, sl])
            yield
            log_a = -C_LRU * rg * _softplus(-lam_ref[:, sl])
            av = jnp.exp(log_a)
            bv = jnp.sqrt(1.0 - jnp.exp(2.0 * log_a)) * (ig * xc)
            if t_valid < tb:
                av = jnp.where(valid, av, 1.0)
                bv = jnp.where(valid, bv, 0.0)
            yield
            d = 1
            while d < tb:
                take = rowin >= d
                a_sh = jnp.where(take, pltpu.roll(av, d, axis=0), 1.0)
                b_sh = jnp.where(take, pltpu.roll(bv, d, axis=0), 0.0)
                bv = av * b_sh + bv
                av = av * a_sh
                d *= 2
                yield
            h0 = jnp.broadcast_to(hst_ref[:, SUBLANES - 1:SUBLANES, sl], (ns, tb, LANES)).reshape(R, LANES)
            h3 = (av * h0 + bv).reshape(ns, tb, LANES)
            hst_ref[:, :, sl] = h3[:, tb - SUBLANES:, :]
            gb = cur[:, :, gsl].reshape(R, LANES)
            y_ref[:, :, D_A + pr * LANES:D_A + (pr + 1) * LANES] = h3 * _gelu(gb).reshape(ns, tb, LANES)
            yield

    lru = lru_steps()
    minvs = [eye - l for l in l_bas]
    for i in range(nsq):
        nxt = []
        for mi, pw in zip(minvs, pws):
            nxt.append(mi + _dot1(mi, pw))
            next(lru, None)
            yield 60
        minvs = nxt
        if i + 1 < nsq:
            nxt = []
            for pw in pws:
                nxt.append(_dot1(pw, pw))
                next(lru, None)
                yield 60
            pws = nxt
    for _ in lru:
        yield 40
    t1s = []
    for mi, xy1, bm in zip(minvs, xy1s, bms):
        t1s.append(_dot1(mi, jnp.concatenate([xy1[:R], bm], axis=1)))
        yield 50
    t2s = []
    for m_ra, t1 in zip(m_ras, t1s):
        t2s.append(_dot1(m_ra, t1))
        yield 50
    for pr in range(N_PAIRS):
        h0, h1 = 2 * pr, 2 * pr + 1
        sl = psl(pr)
        m1 = masks[1]
        bp_ref[:, sl] = t1s[h0][:, LANES:] + t1s[h1][:, LANES:]
        rp_ref[:, sl] = (rms[h0] - t2s[h0][:, LANES:]) + (rms[h1] - t2s[h1][:, LANES:])
        x2_ref[:, sl] = jnp.where(m1, t1s[h1][:, :LANES], t1s[h0][:, :LANES])
        y2_ref[:, sl] = jnp.where(m1, xy1s[h1][R:] - t2s[h1][:, :LANES], xy1s[h0][R:] - t2s[h0][:, :LANES])

    bi = _iota((LANES, LANES), 0)
    bj = _iota((LANES, LANES), 1)
    bd = (bi < HEAD) == (bj < HEAD)
    nck = tb // chunk
    sites = [(s, c, pr) for s in range(ns) for c in range(nck) for pr in range(N_PAIRS)]
    rows_of = lambda s, c: slice(s * tb + c * chunk, s * tb + (c + 1) * chunk)
    thetas = {}
    psis = {}
    for s, c, pr in sites:
        rs, sl = rows_of(s, c), psl(pr)
        thetas[s, c, pr] = jnp.where(bd, _state_dot(bp_ref[rs, sl], ab_ref[rs, sl], _TN), 0.0)
        vx = jnp.concatenate([v_ref[rs, sl], x2_ref[rs, sl]], axis=0)
        ka = jnp.concatenate([kb_ref[rs, sl], ab_ref[rs, sl]], axis=0)
        psis[s, c, pr] = jnp.where(bd, _state_dot(vx, ka, _TN), 0.0)
        yield 60
    for s in range(ns):
        sps = [st_ref[s, pr] if carry_state else pair_tile(s0_ref, s, pr) for pr in range(N_PAIRS)]
        for c in range(nck):
            rs = rows_of(s, c)
            for pr in range(N_PAIRS):
                yacc_ref[rs, psl(pr)] = _state_dot(rp_ref[rs, psl(pr)], sps[pr], _NT) + y2_ref[rs, psl(pr)]
            yield 100
            nxt = []
            for pr in range(N_PAIRS):
                nxt.append(sps[pr] * wt_ref[rs.start:rs.start + 1, psl(pr)]
                           + (_state_dot(sps[pr], thetas[s, c, pr]) + psis[s, c, pr]))
                yield 50
            sps = nxt
        for pr in range(N_PAIRS):
            if carry_state:
                st_ref[s, pr] = sps[pr]
            else:
                sout_ref[s, 2 * pr] = sps[pr][:HEAD, :HEAD]
                sout_ref[s, 2 * pr + 1] = sps[pr][HEAD:, HEAD:]
        yield 50

    y = yacc_ref[...]
    mean = seg_sum(y) * (1.0 / HEAD)
    yc = y - mean
    yield 400
    var = seg_sum(yc * yc) * (1.0 / HEAD)
    yn = yc * lax.rsqrt(var + GN_EPS) * lnw_ref[...] + lnb_ref[...]
    ya = (yn + bonus) * g
    y_ref[:, :, :D_A] = ya.reshape(ns, tb, D_A)
    yield 600


def _mixer_kernel(proj_ref, pprev_ref, conv0_ref, s0_ref, lru0_ref, *refs, ns, tb, chunk, t_valid, carry_state, streams):
    n_par = 20
    params = refs[:n_par]
    y_ref, sout_ref, lruo_ref, convo_ref, prev_ref, hst_ref, st_ref = refs[n_par:n_par + 7]
    slabs = refs[n_par + 7:]
    t = pl.program_id(1)

    @pl.when(t == 0)
    def _():
        prev_ref[...] = jnp.zeros_like(prev_ref)
        prev_ref[:, SUBLANES - 1:, :C_A] = pprev_ref[...]
        prev_ref[:, SUBLANES - (CONV_W - 1):, C_A:] = conv0_ref[...]
        hst_ref[...] = jnp.broadcast_to(lru0_ref[...], hst_ref.shape)
        if carry_state:
            for q in range(streams):
                for pr in range(N_PAIRS):
                    st_ref[q, pr] = _pair_tile(s0_ref, q, pr)

    def make(q):
        sq = pl.ds(q * ns, ns)
        return _mixer_stream(
            proj_ref.at[sq], s0_ref.at[sq], *params, y_ref.at[sq], sout_ref.at[sq],
            prev_ref.at[sq], hst_ref.at[sq], st_ref.at[pl.ds(q, 1)], *[r.at[q] for r in slabs],
            ns=ns, tb=tb, chunk=chunk, t_valid=t_valid, carry_state=carry_state)

    gens = [make(q) for q in range(streams)]
    clock = [0.0] * streams
    live = list(range(streams))
    while live:
        q = min(live, key=lambda i: clock[i])
        cost = next(gens[q], None)
        if cost is None:
            live.remove(q)
        else:
            clock[q] += cost

    @pl.when(t == pl.num_programs(1) - 1)
    def _():
        lruo_ref[...] = hst_ref[...]
        convo_ref[...] = proj_ref[:, tb - SUBLANES:, C_A:C_A + D_B]
        if carry_state:
            for q in range(streams):
                for pr in range(N_PAIRS):
                    tile = st_ref[q, pr]
                    sout_ref[q, 2 * pr] = tile[:HEAD, :HEAD]
                    sout_ref[q, 2 * pr + 1] = tile[HEAD:, HEAD:]


def _mixer(proj, p_prev, conv0, s0, lru0, prm, ns, tb, chunk, t_valid, streams=1):
    S, T = s0.shape[0], proj.shape[1]
    R = ns * tb
    assert R == ROWS
    carry_state = T // tb > 1
    assert ns == 1 or not carry_state
    kern = functools.partial(_mixer_kernel, ns=ns, tb=tb, chunk=chunk, t_valid=t_valid, carry_state=carry_state,
                             streams=streams)
    nb = ns * streams

    def full(a):
        nd = a.ndim
        return pl.BlockSpec(a.shape, lambda s, t, nd=nd: (0,) * nd)

    params = [prm[n] for n in ('mu', 'w0', 'ww2', 'a0', 'wa2', 'wg2', 'k_k', 'k_a', 'r_k', 'lnx_w', 'lnx_b',
                               'seg', 'segt', 'conv_w', 'conv_b', 'wga', 'bga', 'wgi', 'bgi', 'lam')]
    slab = lambda w: pltpu.VMEM((streams, R, w), F32)
    return pl.pallas_call(
        kern,
        out_shape=(jax.ShapeDtypeStruct((S, T, D_A + D_B), F32),
                   jax.ShapeDtypeStruct((S, N_HEADS, HEAD, HEAD), F32),
                   jax.ShapeDtypeStruct((S, SUBLANES, D_B), F32),
                   jax.ShapeDtypeStruct((S, SUBLANES, D_B), F32)),
        grid=(S // nb, T // tb),
        in_specs=[pl.BlockSpec((nb, tb, C_TOT), lambda s, t: (s, t, 0)),
                  pl.BlockSpec((nb, 1, C_A), lambda s, t: (s, 0, 0)),
                  pl.BlockSpec((nb, CONV_W - 1, D_B), lambda s, t: (s, 0, 0)),
                  pl.BlockSpec((nb, N_HEADS, HEAD, HEAD), lambda s, t: (s, 0, 0, 0)),
                  pl.BlockSpec((nb, 1, D_B), lambda s, t: (s, 0, 0))]
                 + [full(a) for a in params],
        out_specs=(pl.BlockSpec((nb, tb, D_A + D_B), lambda s, t: (s, t, 0)),
                   pl.BlockSpec((nb, N_HEADS, HEAD, HEAD), lambda s, t: (s, 0, 0, 0),
                                pipeline_mode=pl.Buffered(1)),
                   pl.BlockSpec((nb, SUBLANES, D_B), lambda s, t: (s, 0, 0)),
                   pl.BlockSpec((nb, SUBLANES, D_B), lambda s, t: (s, 0, 0))),
        scratch_shapes=[pltpu.VMEM((nb, SUBLANES, C_A + D_B), F32),
                        pltpu.VMEM((nb, SUBLANES, D_B), F32),
                        pltpu.VMEM((streams, N_PAIRS, LANES, LANES), F32)]
                       + [slab(D_A) for _ in range(9)],
        compiler_params=_cparams(("arbitrary", "arbitrary")),
        name="mixer",
    )(proj, p_prev, conv0, s0, lru0, *params)


def _layernorm(x, w, b):
    mean = jnp.mean(x, axis=-1, keepdims=True)
    xc = x - mean
    var = jnp.mean(xc * xc, axis=-1, keepdims=True)
    return xc * lax.rsqrt(var + LN_EPS) * w + b


def _outproj_kernel(y_ref, x_ref, g1_ref, sc2_ref, sh2_ref, w_ref, lnw_ref, lnb_ref, x1_ref, h2_ref, *, alpha):
    ns, tb, _ = y_ref.shape
    mix = _dot1(y_ref[...].reshape(ns * tb, D_MODEL), w_ref[...]).reshape(ns, tb, D_MODEL)
    x1 = _layernorm(alpha * x_ref[...] + (1.0 + g1_ref[...]) * mix, lnw_ref[...], lnb_ref[...])
    x1_ref[...] = x1
    h2 = x1 * (1.0 + sc2_ref[...]) + sh2_ref[...]
    h2_ref[...] = h2.reshape(ns * tb, D_MODEL).astype(BF16)


def _outproj(y, x, mod, w_bf16, ln_w, ln_b, ns, tb, alpha):
    S, T, _ = x.shape
    nt = T // tb
    blk = pl.BlockSpec((ns, tb, D_MODEL), lambda s, t: (s, t, 0))
    modspec = lambda i: pl.BlockSpec((ns, 1, D_MODEL), lambda s, t, i=i: (s, 0, i))
    vec = pl.BlockSpec((1, D_MODEL), lambda s, t: (0, 0))
    return pl.pallas_call(
        functools.partial(_outproj_kernel, alpha=alpha),
        out_shape=(jax.ShapeDtypeStruct((S, T, D_MODEL), F32), jax.ShapeDtypeStruct((S * T, D_MODEL), BF16)),
        grid=(S // ns, nt),
        in_specs=[blk, blk, modspec(2), modspec(4), modspec(3),
                  pl.BlockSpec((D_MODEL, D_MODEL), lambda s, t: (0, 0)), vec, vec],
        out_specs=(blk, pl.BlockSpec((ns * tb, D_MODEL), lambda s, t: (s * nt + t, 0))),
        compiler_params=_cparams(("arbitrary", "arbitrary")),
        name="outproj",
    )(y, x, mod, mod, mod, w_bf16, ln_w.reshape(1, -1), ln_b.reshape(1, -1))


def _topk_rows(s, ridx, n_rows):
    out_i = _iota((TOPK, s.shape[1]), 0)
    vals = jnp.zeros((TOPK, s.shape[1]), F32)
    idxs = jnp.zeros((TOPK, s.shape[1]), F32)
    for it in range(TOPK):
        m = jnp.max(s, axis=0, keepdims=True)
        idx = jnp.min(jnp.where(s == m, ridx, float(n_rows)), axis=0, keepdims=True)
        vals = jnp.where(out_i == it, m, vals)
        idxs = jnp.where(out_i == it, idx, idxs)
        s = jnp.where(ridx == idx, -jnp.inf, s)
    return vals, idxs


def _route_kernel(h_ref, wq_ref, sk_ref, g_ref, gate_ref, e1_ref, e2_ref):
    R = ROWS
    q = _dot1(h_ref[...], wq_ref[...])
    sk0 = sk_ref[0]
    sk1 = sk_ref[1]
    gates, e1s, e2s = [], [], []
    key_rows = _iota((N_KEYS, R), 0).astype(F32)
    r16 = _iota((TOPK, R), 0).astype(F32)
    r8 = _iota((SUBLANES, R), 0).astype(F32)
    ea_rank = jnp.where(r8 < 3, 2.0, jnp.where(r8 < 5, 3.0, 4.0))
    eb_rank = jnp.where((r8 == 0) | (r8 == 3) | (r8 == 5), 2.0, jnp.where((r8 == 1) | (r8 == 4), 3.0, 4.0))
    flat = jnp.concatenate([r16, TOPK + r8, TOPK * r16, TOPK * r8 + 1.0, TOPK * ea_rank + eb_rank], axis=0)
    neg = -jnp.inf
    for hd in range(P_HEADS):
        base = hd * 2 * D_QH
        s1 = _dot1(sk0, q[:, base:base + D_QH], _NT)
        s2 = _dot1(sk1, q[:, base + D_QH:base + 2 * D_QH], _NT)
        v1, i1 = _topk_rows(s1, key_rows, N_KEYS)
        v2, i2 = _topk_rows(s2, key_rows, N_KEYS)
        ea = jnp.where(r8 < 3, v1[2:3], jnp.where(r8 < 5, v1[3:4], v1[4:5]))
        eb = jnp.where(eb_rank == 2.0, v2[2:3], jnp.where(eb_rank == 3.0, v2[3:4], v2[4:5]))
        cand = jnp.concatenate([
            v1[0:1] + v2,
            v1[1:2] + v2[:SUBLANES],
            jnp.where(r16 >= 2, v1 + v2[0:1], neg),
            jnp.where(r8 >= 2, v1[:SUBLANES] + v2[1:2], neg),
            jnp.where(r8 < 6, ea + eb, neg),
        ], axis=0)
        sv, ci = _topk_rows(cand, flat, TOPK * TOPK)
        ca = jnp.floor(ci * (1.0 / TOPK))
        cb = ci - TOPK * ca
        e1 = jnp.zeros((TOPK, R), F32)
        e2 = jnp.zeros((TOPK, R), F32)
        for a in range(TOPK):
            e1 = jnp.where(ca == float(a), i1[a:a + 1, :], e1)
            e2 = jnp.where(cb == float(a), i2[a:a + 1, :], e2)
        ex = jnp.exp(sv - jnp.max(sv, axis=0, keepdims=True))
        gates.append(ex / jnp.sum(ex, axis=0, keepdims=True))
        e1s.append(e1)
        e2s.append(e2)
    gate_ref[...] = jnp.concatenate(gates, axis=0).T
    e1_ref[...] = jnp.concatenate(e1s, axis=0).T
    e2_ref[...] = jnp.concatenate(e2s, axis=0).T
    key_i = _iota((N_KEYS, P_HEADS * TOPK), 0).astype(F32)

    def per_octet(o, carry):
        base = pl.multiple_of(o * SUBLANES, SUBLANES)
        tiles = []
        for j in range(SUBLANES):
            sel1 = key_i == e1_ref[pl.ds(base + j, 1), :]
            sel2 = key_i == e2_ref[pl.ds(base + j, 1), :]
            m1 = jnp.where(sel1, gate_ref[pl.ds(base + j, 1), :], 0.0)
            m2 = jnp.where(sel2, 1.0, 0.0)
            tiles.append(_dot1(m1, m2, _NT))
        by_key = jnp.swapaxes(jnp.stack(tiles, axis=0), 0, 1)
        for i1v in range(N_KEYS):
            g_ref[pl.ds(base, SUBLANES), i1v * N_KEYS:(i1v + 1) * N_KEYS] = by_key[i1v]
        return carry

    lax.fori_loop(0, R // SUBLANES, per_octet, 0, unroll=8)


def _route(h2, wq_bf16, sub_keys):
    n = h2.shape[0]
    assert n % ROWS == 0
    return pl.pallas_call(
        _route_kernel,
        out_shape=jax.ShapeDtypeStruct((n, N_EXPERTS), F32),
        grid=(n // ROWS,),
        in_specs=[pl.BlockSpec((ROWS, D_MODEL), lambda i: (i, 0)),
                  pl.BlockSpec((D_MODEL, D_MODEL), lambda i: (0, 0)),
                  pl.BlockSpec((2, N_KEYS, D_QH), lambda i: (0, 0, 0))],
        out_specs=pl.BlockSpec((ROWS, N_EXPERTS), lambda i: (i, 0)),
        scratch_shapes=[pltpu.VMEM((ROWS, P_HEADS * TOPK), F32) for _ in range(3)],
        compiler_params=_cparams(("arbitrary",)),
        name="route",
    )(h2, wq_bf16, sub_keys)


EXPERT_CHUNK = 512
EXPERT_ROWS_MAX = 1152


EXPERT_SPLIT = 4


def _experts_kernel(h_ref, g_ref, u_ref, v_ref, o_ref):
    @pl.when(pl.program_id(1) == 0)
    def _():
        o_ref[...] = jnp.zeros_like(o_ref)

    u = u_ref[...].astype(BF16)
    v = v_ref[...].astype(BF16)
    rb = h_ref.shape[0] // EXPERT_SPLIT
    rows = [pl.ds(b * rb, rb) for b in range(EXPERT_SPLIT)]
    act = _dg(h_ref[rows[0], :], u, _NT)
    for b in range(EXPERT_SPLIT):
        nxt = _dg(h_ref[rows[b + 1], :], u, _NT) if b + 1 < EXPERT_SPLIT else None
        w = (g_ref[rows[b], :] * _gelu(act)).astype(BF16)
        o_ref[rows[b], :] += _dg(w, v, _NN)
        act = nxt


def _token_block(n, cap):
    for tb in range(min(n, cap), 0, -1):
        if n % tb == 0 and tb % (2 * SUBLANES * EXPERT_SPLIT) == 0:
            return tb
    raise ValueError(n)


def _experts(h2, gmat, peer_u, peer_v):
    n = h2.shape[0]
    tb = _token_block(n, EXPERT_ROWS_MAX)
    ec = EXPERT_CHUNK
    return pl.pallas_call(
        _experts_kernel,
        out_shape=jax.ShapeDtypeStruct((n, D_MODEL), F32),
        grid=(n // tb, N_EXPERTS // ec),
        in_specs=[pl.BlockSpec((tb, D_MODEL), lambda i, e: (i, 0)),
                  pl.BlockSpec((tb, ec), lambda i, e: (i, e)),
                  pl.BlockSpec((ec, D_MODEL), lambda i, e: (e, 0)),
                  pl.BlockSpec((ec, D_MODEL), lambda i, e: (e, 0))],
        out_specs=pl.BlockSpec((tb, D_MODEL), lambda i, e: (i, 0)),
        compiler_params=_cparams(("arbitrary", "arbitrary")),
        name="experts",
    )(h2, gmat, peer_u, peer_v)


def _ln2_kernel(x1_ref, ff_ref, g2_ref, lnw_ref, lnb_ref, o_ref, *, alpha):
    ff = ff_ref[...].reshape(x1_ref.shape)
    o_ref[...] = _layernorm(alpha * x1_ref[...] + (1.0 + g2_ref[...]) * ff, lnw_ref[...], lnb_ref[...])


def _ln2(x1, ff, mod, ln_w, ln_b, ns, tb, alpha):
    S, T, _ = x1.shape
    nt = T // tb
    blk = pl.BlockSpec((ns, tb, D_MODEL), lambda s, t: (s, t, 0))
    vec = pl.BlockSpec((1, D_MODEL), lambda s, t: (0, 0))
    if ff.ndim == 2:
        assert ns == 1
        ffspec = pl.BlockSpec((tb, D_MODEL), lambda s, t: (s * nt + t, 0))
    else:
        ffspec = blk
    return pl.pallas_call(
        functools.partial(_ln2_kernel, alpha=alpha),
        out_shape=jax.ShapeDtypeStruct((S, T, D_MODEL), F32),
        grid=(S // ns, nt),
        in_specs=[blk, ffspec, pl.BlockSpec((ns, 1, D_MODEL), lambda s, t: (s, 0, 5)), vec, vec],
        out_specs=blk,
        compiler_params=_cparams(("arbitrary", "arbitrary")),
        name="ln2",
    )(x1, ff, mod, ln_w.reshape(1, -1), ln_b.reshape(1, -1))


def _pair_blockdiag(w):
    n = w.shape[0] // 2
    w = w.reshape(n, 2, HEAD, HEAD)
    z = jnp.zeros((n, HEAD, HEAD), w.dtype)
    top = jnp.concatenate([w[:, 0], z], axis=2)
    bot = jnp.concatenate([z, w[:, 1]], axis=2)
    return jnp.concatenate([top, bot], axis=1)


def _mix_group(x, mod, p_prev, wkv0, lru0, conv0, prm, *, ns_mix, tb_mix, chunk, t_valid, ns_big, tb_big, alpha,
               streams=1, shift_rows=None):
    S, T, _ = x.shape
    if shift_rows is None:
        proj, h_last = _inproj(x, mod, prm['w_in'], ns_big, tb_big)
    else:
        extra = S // T
        x_ext = jnp.concatenate([x, shift_rows.reshape(extra, T, D_MODEL)], axis=0)
        mod_ext = jnp.concatenate([mod, jnp.zeros((extra, 1, 6 * D_MODEL), F32)], axis=0)
        ns_ext = max(d for d in range(1, ns_big + extra // 2 + 1) if (S + extra) % d == 0)
        proj, h_last = _inproj(x_ext, mod_ext, prm['w_in'], ns_ext, tb_big)
        p_prev = proj[S:].reshape(S, C_TOT)[:, :C_A]
        h_last = h_last[:S]
    y, wkv, lru_o, conv_o = _mixer(proj, p_prev[:, None, :], conv0, wkv0, lru0[:, None, :], prm,
                                   ns_mix, tb_mix, chunk, t_valid, streams)
    x1, h2 = _outproj(y, x, mod, prm['w_out'], prm['ln1_w'], prm['ln1_b'], ns_big, tb_big, alpha)
    tv = (t_valid - 1) % SUBLANES
    shift = h_last[:, tv]
    lru = lru_o[:, SUBLANES - 1]
    conv = conv_o[:, tv - (CONV_W - 2):tv + 1]
    return x1, h2, (shift, wkv, lru, conv)


def kernel(x_prompt, x_sample, c_prompt, c_sample, state_shift, state_wkv, state_lru, state_conv, w_ada, b_ada, w_in, mu_shift, w0, w_w2, a0, w_a2, w_g2, k_k, k_a, r_k, lnx_w, lnx_b, conv_w, conv_b, w_gate_a, b_gate_a, w_gate_i, b_gate_i, lru_lambda, w_out, ln1_w, ln1_b, w_q, sub_keys, peer_u, peer_v, ln2_w, ln2_b):
    depth = w_ada.shape[0]
    alpha = (2 * depth) ** 0.25
    bp, tp, _ = x_prompt.shape
    bs, ts, _ = x_sample.shape
    yp = x_prompt
    ys = jnp.pad(x_sample, ((0, 0), (0, SAMPLE_TPAD - ts), (0, 0)))
    head_of = jnp.arange(D_A) // HEAD
    seg = (head_of[:, None] == jnp.arange(LANES)[None, :]).astype(BF16)
    row = lambda a: a.reshape(1, -1)
    outs = [[] for _ in range(8)]
    for l in range(depth):
        zpad = jnp.zeros((LANES - R_W, D_A), F32)
        prm = {
            'w_in': w_in[l].astype(BF16), 'w_out': w_out[l].astype(BF16), 'w_q': w_q[l].astype(BF16),
            'sub_keys': sub_keys[l],
            'mu': row(mu_shift[l]), 'w0': row(w0[l]),
            'ww2': jnp.concatenate([w_w2[l], zpad], axis=0).astype(BF16), 'a0': row(a0[l]),
            'wa2': jnp.concatenate([zpad, w_a2[l]], axis=0).astype(BF16), 'wg2': w_g2[l].astype(BF16),
            'k_k': row(k_k[l]), 'k_a': row(k_a[l]), 'r_k': row(r_k[l]),
            'lnx_w': row(lnx_w[l]), 'lnx_b': row(lnx_b[l]), 'seg': seg, 'segt': seg.T,
            'conv_w': conv_w[l], 'conv_b': row(conv_b[l]),
            'wga': _pair_blockdiag(w_gate_a[l]).astype(BF16), 'bga': row(b_gate_a[l]),
            'wgi': _pair_blockdiag(w_gate_i[l]).astype(BF16), 'bgi': row(b_gate_i[l]),
            'lam': row(lru_lambda[l]),
            'ln1_w': ln1_w[l], 'ln1_b': ln1_b[l], 'ln2_w': ln2_w[l], 'ln2_b': ln2_b[l],
        }
        c_all = jnp.concatenate([c_prompt, c_sample], axis=0)
        mod = _ada(c_all, w_ada[l], b_ada[l])
        mod_p = mod[:bp, None, :]
        mod_s = mod[bp:, None, :]
        x1p, h2p, st_p = _mix_group(
            yp, mod_p, jnp.zeros((bp, C_A), F32), jnp.zeros((bp, N_HEADS, HEAD, HEAD), F32),
            jnp.zeros((bp, D_B), F32), jnp.zeros((bp, CONV_W - 1, D_B), F32), prm,
            ns_mix=1, tb_mix=ROWS, chunk=32, t_valid=ROWS, ns_big=1, tb_big=min(512, tp), alpha=alpha,
            streams=2 if bp % 2 == 0 else 1)
        x1s, h2s, st_s = _mix_group(
            ys, mod_s, None, state_wkv[l], state_lru[l], state_conv[l], prm,
            ns_mix=ROWS // SAMPLE_TPAD, tb_mix=SAMPLE_TPAD, chunk=SAMPLE_TPAD, t_valid=ts,
            ns_big=min(64, bs), tb_big=SAMPLE_TPAD, alpha=alpha, shift_rows=state_shift[l])
        h2s = h2s.reshape(bs, SAMPLE_TPAD, D_MODEL)[:, :ts].reshape(bs * ts, D_MODEL)
        h2 = jnp.concatenate([h2p, h2s], axis=0)
        gmat = _route(h2, prm['w_q'], prm['sub_keys'])
        ff = _experts(h2, gmat, peer_u[l], peer_v[l])
        yp = _ln2(x1p, ff, mod_p, ln2_w[l], ln2_b[l], 1, min(512, tp), alpha)
        ys_real = _ln2(x1s[:, :ts], ff[bp * tp:].reshape(bs, ts, D_MODEL), mod_s, ln2_w[l], ln2_b[l],
                       min(64, bs), ts, alpha)
        ys = jnp.pad(ys_real, ((0, 0), (0, SAMPLE_TPAD - ts), (0, 0)))
        for i, a in enumerate(st_p + st_s):
            outs[i].append(a)
    return (yp, ys[:, :ts]) + tuple(jnp.stack(o) for o in outs)
```
